```python
import jax, jax.numpy as jnp
from jax import lax
import numpy as np

D_MODEL = 1024
BATCH = 4
SEQ = 4096
DEPTH = 1

HEAD_DIM = 64
N_HEADS_SB = 8
N_HEADS_DSA = 8
N_HEADS_IDX = 8
D_IDX = 64
TOPK_MAX = 256
D_FF = 2816
Q_BLOCK = 128
N_MOD = 9
RMS_EPS = 1e-6

W_SB = N_HEADS_SB * HEAD_DIM
W_DSA = N_HEADS_DSA * HEAD_DIM
IN_SIZES = (W_SB, W_SB, W_SB,
            W_DSA, HEAD_DIM, HEAD_DIM,
            N_HEADS_IDX * D_IDX, D_IDX,
            N_HEADS_IDX)
D_IN = int(sum(IN_SIZES))
SPLIT_POINTS = tuple(int(v) for v in np.cumsum(IN_SIZES)[:-1])

kernel_name = "hybrid_stickbreaking_dsa_macaron_block"


def rmsnorm(x, g):
    xf = x.astype(jnp.float32)
    y = xf * lax.rsqrt(jnp.mean(xf * xf, axis=-1, keepdims=True) + RMS_EPS)
    return (y * g.astype(jnp.float32)).astype(x.dtype)


def modulate(x, shift, scale):
    return x * (1.0 + scale[:, None, :]) + shift[:, None, :]


def swiglu(x, w1, w3, w2):
    return (jax.nn.silu(x @ w1) * (x @ w3)) @ w2


def to_blocks(a):
    b, s = a.shape[:2]
    return a.reshape((b, s // Q_BLOCK, Q_BLOCK) + a.shape[2:]).swapaxes(0, 1)


def from_blocks(a):
    a = a.swapaxes(0, 1)
    return a.reshape((a.shape[0], a.shape[1] * a.shape[2]) + a.shape[3:])


def stick_breaking_attention(q, k, v):
    S, dh = q.shape[1], q.shape[-1]
    scale = dh ** -0.5
    kpos = jnp.arange(S)
    nb = S // Q_BLOCK

    def block(args):
        q_blk, i = args
        qpos = i * Q_BLOCK + jnp.arange(Q_BLOCK)
        z = jnp.einsum('bqhd,bshd->bhqs', q_blk, k).astype(jnp.float32) * scale
        strict = (kpos[None, :] < qpos[:, None])[None, None]
        log1m = jnp.where(strict, jax.nn.log_sigmoid(-z), 0.0)
        after = lax.cumsum(log1m, axis=3, reverse=True) - log1m
        a = jnp.where(strict, jnp.exp(jax.nn.log_sigmoid(z) + after), 0.0)
        return jnp.einsum('bhqs,bshd->bqhd', a.astype(v.dtype), v)

    out = lax.map(block, (to_blocks(q), jnp.arange(nb)))
    return from_blocks(out)


def dsa_attention(q, k, v, q_idx, k_idx, w_idx):
    B, S, H, dh = q.shape
    topk = min(TOPK_MAX, S // 4)
    scale = dh ** -0.5
    idx_scale = (D_IDX ** -0.5) * (N_HEADS_IDX ** -0.5)
    slopes = jnp.exp2(-8.0 * jnp.arange(1, H + 1, dtype=jnp.float32) / H)
    kpos = jnp.arange(S)
    bidx = jnp.arange(B)[:, None, None]
    nb = S // Q_BLOCK

    def block(args):
        q_blk, qi_blk, wi_blk, i = args
        qpos = i * Q_BLOCK + jnp.arange(Q_BLOCK)
        causal = (kpos[None, :] <= qpos[:, None])[None]
        rel = jax.nn.relu(jnp.einsum('bqhd,bsd->bqhs', qi_blk, k_idx).astype(jnp.float32))
        score = jnp.einsum('bqhs,bqh->bqs', rel, wi_blk.astype(jnp.float32)) * idx_scale
        score = jnp.where(causal, score, -jnp.inf)
        top_val, sel = lax.top_k(score, topk)
        valid = top_val > -jnp.inf
        k_sel = k[bidx, sel]
        v_sel = v[bidx, sel]
        logits = jnp.einsum('bqhd,bqkd->bhqk', q_blk, k_sel).astype(jnp.float32) * scale
        dist = (qpos[None, :, None] - sel).astype(jnp.float32)
        logits = logits - slopes[None, :, None, None] * dist[:, None]
        logits = jnp.where(valid[:, None], logits, -jnp.inf)
        p = jax.nn.softmax(logits, axis=-1)
        return jnp.einsum('bhqk,bqkd->bqhd', p.astype(v.dtype), v_sel)

    out = lax.map(block, (to_blocks(q), to_blocks(q_idx), to_blocks(w_idx), jnp.arange(nb)))
    return from_blocks(out)


def setup_inputs(seed: int = 0) -> dict:
    key = jax.random.key(seed)
    ks = jax.random.split(key, 24)
    f32 = jnp.float32

    def dense(k, shape, fan_in, mult=1.0):
        return jax.random.normal(k, shape, f32) * (mult * fan_in ** -0.5)

    def gain(k, shape):
        return 1.0 + 0.05 * jax.random.normal(k, shape, f32)

    L, D = DEPTH, D_MODEL
    return {
        "x": jax.random.normal(ks[0], (BATCH, SEQ, D), f32),
        "c": jax.random.normal(ks[1], (BATCH, D), f32),
        "w_ada": dense(ks[2], (L, D, N_MOD * D), D, 0.5),
        "b_ada": 0.02 * jax.random.normal(ks[3], (L, N_MOD * D), f32),
        "g_ffn1": gain(ks[4], (L, D)),
        "w1_ffn1": dense(ks[5], (L, D, D_FF), D),
        "w3_ffn1": dense(ks[6], (L, D, D_FF), D),
        "w2_ffn1": dense(ks[7], (L, D_FF, D), D_FF),
        "g_mix": gain(ks[8], (L, D)),
        "w_in": dense(ks[9], (L, D, D_IN), D),
        "g_q_dsa": gain(ks[10], (L, HEAD_DIM)),
        "g_k_dsa": gain(ks[11], (L, HEAD_DIM)),
        "w_proj_sb": dense(ks[12], (L, W_SB, D), W_SB),
        "w_proj_dsa": dense(ks[13], (L, W_DSA, D), W_DSA),
        "w_gate": dense(ks[14], (L, D, 2 * D), D),
        "b_gate": 0.02 * jax.random.normal(ks[15], (L, 2 * D), f32),
        "w_out": dense(ks[16], (L, D, D), D),
        "g_ffn2": gain(ks[17], (L, D)),
        "w1_ffn2": dense(ks[18], (L, D, D_FF), D),
        "w3_ffn2": dense(ks[19], (L, D, D_FF), D),
        "w2_ffn2": dense(ks[20], (L, D_FF, D), D_FF),
    }


def reference(x, c, w_ada, b_ada, g_ffn1, w1_ffn1, w3_ffn1, w2_ffn1, g_mix, w_in,
              g_q_dsa, g_k_dsa, w_proj_sb, w_proj_dsa, w_gate, b_gate, w_out,
              g_ffn2, w1_ffn2, w3_ffn2, w2_ffn2):
    B, S, D = x.shape
    sc = jax.nn.silu(c)
    h = x
    for l in range(DEPTH):
        mod = sc @ w_ada[l] + b_ada[l]
        sh1, sc1, gt1, sh2, sc2, gt2, sh3, sc3, gt3 = jnp.split(mod, N_MOD, axis=-1)

        n1 = modulate(rmsnorm(h, g_ffn1[l]), sh1, sc1)
        h = h + 0.5 * gt1[:, None, :] * swiglu(n1, w1_ffn1[l], w3_ffn1[l], w2_ffn1[l])

        n = modulate(rmsnorm(h, g_mix[l]), sh2, sc2)
        proj = n @ w_in[l]
        q_sb, k_sb, v_sb, q_d, k_d, v_d, q_i, k_i, w_i = jnp.split(proj, SPLIT_POINTS, axis=-1)

        heads = lambda t, nh: t.reshape(B, S, nh, -1)
        y_sb = stick_breaking_attention(heads(q_sb, N_HEADS_SB), heads(k_sb, N_HEADS_SB),
                                        heads(v_sb, N_HEADS_SB))
        q_d = rmsnorm(heads(q_d, N_HEADS_DSA), g_q_dsa[l])
        k_d = rmsnorm(k_d, g_k_dsa[l])
        y_d = dsa_attention(q_d, k_d, v_d, heads(q_i, N_HEADS_IDX), k_i, w_i)

        y_sb = y_sb.reshape(B, S, W_SB) @ w_proj_sb[l]
        y_d = y_d.reshape(B, S, W_DSA) @ w_proj_dsa[l]
        gates = jax.nn.sigmoid(n @ w_gate[l] + b_gate[l])
        g_sb, g_d = jnp.split(gates, 2, axis=-1)
        mixed = (g_sb * y_sb + g_d * y_d) @ w_out[l]
        h = h + gt2[:, None, :] * mixed

        n3 = modulate(rmsnorm(h, g_ffn2[l]), sh3, sc3)
        h = h + 0.5 * gt3[:, None, :] * swiglu(n3, w1_ffn2[l], w3_ffn2[l], w2_ffn2[l])
    return h
```

```python
import functools

import numpy as np
import jax
import jax.numpy as jnp
from jax import lax
from jax.experimental import pallas as pl
from jax.experimental.pallas import tpu as pltpu

F32 = jnp.float32
BF16 = jnp.bfloat16

HEAD_DIM = 64
N_HEADS = 8
W_HEADS = N_HEADS * HEAD_DIM
N_PAIRS = N_HEADS // 2
LANES = 128
TOPK_MAX = 256
N_MOD = 9
RMS_EPS = 1e-6
D_IDX = 64

TQ = 128
TK = 128
NEG_BIG = -1e30
INT_MIN = np.int32(-2**31)
VMEM_LIMIT = 56 * 1024 * 1024


def _cparams(n_axes):
    return pltpu.CompilerParams(
        dimension_semantics=("arbitrary",) * n_axes, vmem_limit_bytes=VMEM_LIMIT)


def _const_spec(shape):
    nd = len(shape)
    return pl.BlockSpec(shape, lambda *_: (0,) * nd, pipeline_mode=pl.Buffered(1))


def _dot(a, b):
    return jnp.dot(a, b, preferred_element_type=F32)


def _dot_nt(a, b):
    return lax.dot_general(a, b, (((1,), (1,)), ((), ())), preferred_element_type=F32)


def _mod_kernel(c_ref, w_ref, b_ref, o_ref):
    nb, kdim, _ = c_ref.shape
    tn = w_ref.shape[1]
    nch = tn // LANES

    def body(kb, accs):
        k0 = pl.multiple_of(kb * 8, 8)
        wblk = w_ref[pl.ds(k0, 8), :]
        out = []
        for b in range(nb):
            cb = c_ref[b, pl.ds(k0, 8), :]
            sb = cb * jax.nn.sigmoid(cb)
            for ch in range(nch):
                out.append(accs[b * nch + ch] + wblk[:, ch * LANES:(ch + 1) * LANES] * sb)
        return tuple(out)

    accs = lax.fori_loop(0, kdim // 8, body,
                         tuple(jnp.zeros((8, LANES), F32) for _ in range(nb * nch)))
    for b in range(nb):
        row = jnp.concatenate(
            [jnp.sum(accs[b * nch + ch], axis=0, keepdims=True) for ch in range(nch)], axis=1)
        o_ref[0, b] = row + b_ref[...]


def _mod_call(c, w_ada, b_ada):
    nb, d = c.shape
    n_out = w_ada.shape[1]
    c_rep = jnp.broadcast_to(c[:, :, None], (nb, d, LANES))
    return pl.pallas_call(
        _mod_kernel,
        out_shape=jax.ShapeDtypeStruct((n_out // d, nb, 1, d), F32),
        grid=(n_out // d,),
        in_specs=[_const_spec((nb, d, LANES)),
                  pl.BlockSpec((d, d), lambda j: (0, j)),
                  pl.BlockSpec((1, d), lambda j: (0, j))],
        out_specs=pl.BlockSpec((1, nb, 1, d), lambda j: (j, 0, 0, 0)),
        compiler_params=_cparams(1),
        name="mod",
    )(c_rep, w_ada, b_ada.reshape(1, n_out))


def _norm_mod(h, g, shift, scale):
    y = h * lax.rsqrt(jnp.mean(h * h, axis=-1, keepdims=True) + RMS_EPS)
    return (y * g) * (1.0 + scale) + shift


def _ffn_kernel(h_ref, mod_ref, g_ref, w1_ref, w3_ref, w2_ref, gn_ref, *out_refs,
                mod_base, emit_next, n_chunks):
    o_ref = out_refs[0]
    h = h_ref[...]
    n = _norm_mod(h, g_ref[...], mod_ref[mod_base], mod_ref[mod_base + 1]).astype(BF16)
    fc = w1_ref.shape[1] // n_chunks
    acc = jnp.zeros(h.shape, F32)
    for c in range(n_chunks):
        a = _dot(n, w1_ref[:, c * fc:(c + 1) * fc])
        b = _dot(n, w3_ref[:, c * fc:(c + 1) * fc])
        act = (a * jax.nn.sigmoid(a) * b).astype(BF16)
        acc = acc + _dot(act, w2_ref[c * fc:(c + 1) * fc, :])
    hn = h + (0.5 * mod_ref[mod_base + 2]) * acc
    o_ref[...] = hn
    if emit_next:
        out_refs[1][...] = _norm_mod(hn, gn_ref[...], mod_ref[mod_base + 3],
                                     mod_ref[mod_base + 4]).astype(BF16)


def _mod_spec(tiles_per_batch, d):
    return pl.BlockSpec((N_MOD, None, 1, d), lambda i: (0, i // tiles_per_batch, 0, 0))


def _ffn_call(h, mod, g, w1, w3, w2, g_next, *, mod_base, emit_next, seq, tm=512):
    n_tok, d = h.shape
    f = w1.shape[1]
    tok = pl.BlockSpec((tm, d), lambda i: (i, 0))
    out_shape = [jax.ShapeDtypeStruct((n_tok, d), F32)]
    out_specs = [tok]
    if emit_next:
        out_shape.append(jax.ShapeDtypeStruct((n_tok, d), BF16))
        out_specs.append(tok)
    return pl.pallas_call(
        functools.partial(_ffn_kernel, mod_base=mod_base, emit_next=emit_next, n_chunks=2),
        out_shape=out_shape,
        grid=(n_tok // tm,),
        in_specs=[tok, _mod_spec(seq // tm, d), _const_spec((1, d)),
                  _const_spec((d, f)), _const_spec((d, f)), _const_spec((f, d)),
                  _const_spec((1, d))],
        out_specs=out_specs,
        compiler_params=_cparams(1),
        name="ffn",
    )(h, mod, g.reshape(1, d), w1, w3, w2, g_next.reshape(1, d))


def _inproj_kernel(n_ref, wsb_ref, wqd_ref, wkv_ref, wqi_ref, wki_ref, wwi_ref,
                   gq_ref, gk_ref, hm_ref,
                   qsb_ref, ksb_ref, vsb_ref, qd_ref, kk_ref, vv_ref, qi_ref, kki_ref, wi_ref):
    n = n_ref[...]
    qk_scale = HEAD_DIM ** -0.5
    sb = _dot(n, wsb_ref[...])
    qsb_ref[...] = (sb[:, :W_HEADS] * qk_scale).astype(BF16)
    ksb_ref[...] = sb[:, W_HEADS:2 * W_HEADS].astype(BF16)
    vsb_ref[...] = sb[:, 2 * W_HEADS:].astype(BF16)

    qd = _dot(n, wqd_ref[...])
    ms = _dot((qd * qd).astype(BF16), hm_ref[...])
    qd_ref[...] = (qd * lax.rsqrt(ms + RMS_EPS) * (gq_ref[...] * qk_scale)).astype(BF16)

    kv = _dot(n, wkv_ref[...])
    kd = kv[:, :LANES]
    kk_ref[...] = (kd * lax.rsqrt(jnp.mean(kd * kd, axis=-1, keepdims=True) + RMS_EPS)
                   * gk_ref[...]).astype(BF16)
    vv_ref[...] = kv[:, LANES:].astype(BF16)

    qi_ref[...] = _dot(n, wqi_ref[...]).astype(BF16)
    kki_ref[...] = _dot(n, wki_ref[...]).astype(BF16)
    wi_ref[...] = _dot(n, wwi_ref[...])


def _inproj_call(n2, w_in, g_q, g_k, tm=512):
    n_tok, d = n2.shape
    sizes = (W_HEADS, W_HEADS, W_HEADS, W_HEADS, HEAD_DIM, HEAD_DIM, N_HEADS * D_IDX, D_IDX, N_HEADS)
    offs = np.concatenate([[0], np.cumsum(sizes)])
    col = lambda k: w_in[:, offs[k]:offs[k + 1]]
    idx_scale = (D_IDX ** -0.5) * (N_HEADS ** -0.5)
    w_sb = jnp.concatenate([col(0), col(1), col(2)], axis=1).astype(BF16)
    w_qd = col(3).astype(BF16)
    w_kv = jnp.concatenate([col(4), col(4), col(5), col(5)], axis=1).astype(BF16)
    w_qi = col(6).astype(BF16)
    w_ki = jnp.concatenate([col(7), col(7)], axis=1).astype(BF16)
    w_wi = jnp.pad(col(8) * idx_scale, ((0, 0), (0, LANES - N_HEADS))).astype(BF16)
    gq = jnp.tile(g_q, N_HEADS).reshape(1, W_HEADS)
    gk = jnp.tile(g_k, 2).reshape(1, LANES)
    head_of = np.arange(W_HEADS) // HEAD_DIM
    head_mean = jnp.asarray((head_of[:, None] == head_of[None, :]) / HEAD_DIM, BF16)

    tok = lambda w: pl.BlockSpec((tm, w), lambda i: (i, 0))
    sd = lambda w, dt: jax.ShapeDtypeStruct((n_tok, w), dt)
    return pl.pallas_call(
        _inproj_kernel,
        out_shape=[sd(W_HEADS, BF16)] * 4 + [sd(LANES, BF16)] * 2 + [sd(W_HEADS, BF16),
                                                                   sd(LANES, BF16), sd(LANES, F32)],
        grid=(n_tok // tm,),
        in_specs=[tok(d)] + [_const_spec(w.shape) for w in
                             (w_sb, w_qd, w_kv, w_qi, w_ki, w_wi, gq, gk, head_mean)],
        out_specs=[tok(W_HEADS)] * 4 + [tok(LANES)] * 2 + [tok(W_HEADS), tok(LANES), tok(LANES)],
        compiler_params=_cparams(1),
        name="inproj",
    )(n2, w_sb, w_qd, w_kv, w_qi, w_ki, w_wi, gq, gk, head_mean)


def _split_heads(x, lane):
    xf = x.astype(F32)
    lo = jnp.where(lane < HEAD_DIM, xf, 0.0).astype(x.dtype)
    hi = jnp.where(lane >= HEAD_DIM, xf, 0.0).astype(x.dtype)
    return lo, hi


def _sb_kernel(q_ref, k_ref, v_ref, t_ref, o_ref):
    i = pl.program_id(2)
    lane = lax.broadcasted_iota(jnp.int32, (TQ, LANES), 1)
    q_lo, q_hi = _split_heads(q_ref[...], lane)
    qs = jnp.concatenate([q_lo, q_hi], axis=0)
    t_pos = i * TQ + lax.broadcasted_iota(jnp.int32, (2 * TQ, TK), 0) % TQ
    col = lax.broadcasted_iota(jnp.int32, (2 * TQ, TK), 1)
    lane_k = lax.broadcasted_iota(jnp.int32, (TK, LANES), 1)

    def body(jj, carry):
        run, acc = carry
        k0 = pl.multiple_of((i - jj) * TK, TK)
        kb = k_ref[pl.ds(k0, TK), :]
        vb = v_ref[pl.ds(k0, TK), :]
        z = _dot_nt(qs, kb)
        strict = (k0 + col) < t_pos
        sp = jnp.maximum(z, 0.0) + jnp.log(1.0 + jnp.exp(-jnp.abs(z)))
        l1m = jnp.where(strict, -sp, 0.0)
        l_hi = l1m.astype(BF16)
        l_lo = (l1m - l_hi.astype(F32)).astype(BF16)
        w = _dot(jnp.concatenate([l_hi, l_lo], axis=1), t_ref[...])
        a = jnp.where(strict, jnp.exp((z - sp) + w[:, :TK] + run), 0.0).astype(BF16)
        v_lo, v_hi = _split_heads(vb, lane_k)
        acc = acc + _dot(jnp.concatenate([a[:TQ], a[TQ:]], axis=1),
                         jnp.concatenate([v_lo, v_hi], axis=0))
        return run + w[:, TK:], acc

    _, acc = lax.fori_loop(0, i + 1, body,
                           (jnp.zeros((2 * TQ, LANES), F32), jnp.zeros((TQ, LANES), F32)))
    o_ref[...] = acc.astype(o_ref.dtype)


def _sb_call(q, k, v, batch, seq):
    q, k, v = (t.reshape(batch, seq, W_HEADS) for t in (q, k, v))
    j_idx = np.arange(TK)
    suffix = (j_idx[:, None] > j_idx[None, :]).astype(np.float32)
    tmat = np.concatenate([suffix, np.ones((TK, LANES), np.float32)], axis=1)
    tmat = jnp.asarray(np.concatenate([tmat, tmat], axis=0), BF16)
    blk = pl.BlockSpec((None, TQ, LANES), lambda b, p, i: (b, i, p))
    allk = pl.BlockSpec((None, seq, LANES), lambda b, p, i: (b, 0, p))
    return pl.pallas_call(
        _sb_kernel,
        out_shape=jax.ShapeDtypeStruct((batch, seq, W_HEADS), BF16),
        grid=(batch, N_PAIRS, seq // TQ),
        in_specs=[blk, allk, allk, _const_spec(tmat.shape)],
        out_specs=blk,
        compiler_params=_cparams(3),
        name="sb",
    )(q, k, v, tmat)


def _dsa_kernel(qd_ref, qi_ref, wi_ref, kk_ref, vv_ref, kki_ref, o_ref,
                qd_st, qi_st, keys_ref, m_ref, acc_ref, *, topk):
    i = pl.program_id(1)
    lane = lax.broadcasted_iota(jnp.int32, (TQ, LANES), 1)
    for p in range(N_PAIRS):
        for src, dst in ((qd_ref, qd_st), (qi_ref, qi_st)):
            lo, hi = _split_heads(src[:, p * LANES:(p + 1) * LANES], lane)
            dst[(2 * p) * TQ:(2 * p + 1) * TQ, :] = lo
            dst[(2 * p + 1) * TQ:(2 * p + 2) * TQ, :] = hi

    t_pos = i * TQ + lax.broadcasted_iota(jnp.int32, (TQ, TK), 0)
    col = lax.broadcasted_iota(jnp.int32, (TQ, TK), 1)
    n_blk = i + 1

    wi = wi_ref[...]
    w_b = [jnp.broadcast_to(wi[:, h:h + 1], (TQ, TK)) for h in range(N_HEADS)]

    def score_body(j, _):
        k0 = pl.multiple_of(j * TK, TK)
        x = _dot_nt(qi_st[...], kki_ref[pl.ds(k0, TK), :])
        score = jnp.zeros((TQ, TK), F32)
        for h in range(N_HEADS):
            score = score + jnp.maximum(x[h * TQ:(h + 1) * TQ], 0.0) * w_b[h]
        bits = pltpu.bitcast(score, jnp.int32)
        key = jnp.where(bits < 0, bits ^ np.int32(0x7FFFFFFF), bits)
        keys_ref[:, pl.ds(k0, TK)] = jnp.where((k0 + col) <= t_pos, key, INT_MIN)
        return 0

    lax.fori_loop(0, n_blk, score_body, 0)

    def count(pred):
        def cbody(j, c):
            k0 = pl.multiple_of(j * TK, TK)
            return c + jnp.where(pred(keys_ref[:, pl.ds(k0, TK)]), 1.0, 0.0)
        c = lax.fori_loop(0, n_blk, cbody, jnp.zeros((TQ, TK), F32))
        return jnp.broadcast_to(jnp.sum(c, axis=1, keepdims=True), (TQ, TK))

    def bit_body(b, prefix):
        cand = prefix | jnp.left_shift(jnp.int32(1), 31 - b)
        cand_s = cand ^ INT_MIN
        return jnp.where(count(lambda kblk: kblk >= cand_s) >= topk, cand, prefix)

    prefix = lax.fori_loop(0, 32, bit_body, jnp.zeros((TQ, TK), jnp.int32))
    thr = jnp.maximum(prefix ^ INT_MIN, INT_MIN + 1)

    n_ge = count(lambda kblk: kblk >= thr)

    @pl.when(jnp.max(n_ge) > topk)
    def _():
        need = topk - count(lambda kblk: kblk > thr)
        r_i = lax.broadcasted_iota(jnp.int32, (TK, TK), 0)
        c_i = lax.broadcasted_iota(jnp.int32, (TK, TK), 1)
        before = jnp.where(r_i < c_i, 1.0, 0.0).astype(BF16)

        def tie_body(j, seen):
            k0 = pl.multiple_of(j * TK, TK)
            kblk = keys_ref[:, pl.ds(k0, TK)]
            eq = jnp.where(kblk == thr, 1.0, 0.0)
            rank = _dot(eq.astype(BF16), before) + seen
            drop = jnp.logical_and(kblk == thr, rank >= need)
            keys_ref[:, pl.ds(k0, TK)] = jnp.where(drop, INT_MIN, kblk)
            return seen + jnp.broadcast_to(jnp.sum(eq, axis=1, keepdims=True), (TQ, TK))

        lax.fori_loop(0, n_blk, tie_body, jnp.zeros((TQ, TK), F32))

    m_ref[...] = jnp.full(m_ref.shape, NEG_BIG, F32)
    acc_ref[...] = jnp.zeros(acc_ref.shape, F32)
    lane_k = lax.broadcasted_iota(jnp.int32, (TK, LANES), 1)
    ones_lo = jnp.where(lane_k < HEAD_DIM, 1.0, 0.0).astype(BF16)
    ones_hi = jnp.where(lane_k >= HEAD_DIM, 1.0, 0.0).astype(BF16)
    slopes = [float(2.0 ** (-8.0 * (h + 1) / N_HEADS)) for h in range(N_HEADS)]

    def attn_body(j, _):
        k0 = pl.multiple_of(j * TK, TK)
        logits = _dot_nt(qd_st[...], kk_ref[pl.ds(k0, TK), :])
        sel = keys_ref[:, pl.ds(k0, TK)] >= thr
        dist = (t_pos - (k0 + col)).astype(F32)
        v_lo, v_hi = _split_heads(vv_ref[pl.ds(k0, TK), :], lane_k)
        rhs = jnp.concatenate([jnp.concatenate([v_lo, ones_lo], axis=1),
                               jnp.concatenate([v_hi, ones_hi], axis=1)], axis=0)
        for p in range(N_PAIRS):
            probs, alphas = [], []
            for h in (2 * p, 2 * p + 1):
                lg = jnp.where(sel, logits[h * TQ:(h + 1) * TQ] - slopes[h] * dist, NEG_BIG)
                m_old = m_ref[h]
                m_new = jnp.maximum(m_old, jnp.broadcast_to(
                    jnp.max(lg, axis=1, keepdims=True), (TQ, TK)))
                m_ref[h] = m_new
                alphas.append(jnp.exp(m_old - m_new))
                probs.append(jnp.exp(lg - m_new).astype(BF16))
            alpha = jnp.where(lane < HEAD_DIM, alphas[0], alphas[1])
            upd = _dot(jnp.concatenate(probs, axis=1), rhs)
            acc_ref[p] = acc_ref[p] * jnp.concatenate([alpha, alpha], axis=1) + upd
        return 0

    lax.fori_loop(0, n_blk, attn_body, 0)
    for p in range(N_PAIRS):
        a = acc_ref[p]
        o_ref[:, p * LANES:(p + 1) * LANES] = (a[:, :LANES] / a[:, LANES:]).astype(o_ref.dtype)


def _dsa_call(qd, qi, wi, kk, vv, kki, batch, seq):
    r3 = lambda t: t.reshape(batch, seq, t.shape[-1])
    qd, qi, wi, kk, vv, kki = map(r3, (qd, qi, wi, kk, vv, kki))
    topk = min(TOPK_MAX, seq // 4)
    blk = lambda w: pl.BlockSpec((None, TQ, w), lambda b, i: (b, i, 0))
    allk = pl.BlockSpec((None, seq, LANES), lambda b, i: (b, 0, 0))
    return pl.pallas_call(
        functools.partial(_dsa_kernel, topk=topk),
        out_shape=jax.ShapeDtypeStruct((batch, seq, W_HEADS), BF16),
        grid=(batch, seq // TQ),
        in_specs=[blk(W_HEADS), blk(W_HEADS), blk(LANES), allk, allk, allk],
        out_specs=blk(W_HEADS),
        scratch_shapes=[pltpu.VMEM((N_HEADS * TQ, LANES), BF16),
                        pltpu.VMEM((N_HEADS * TQ, LANES), BF16),
                        pltpu.VMEM((TQ, seq), jnp.int32),
                        pltpu.VMEM((N_HEADS, TQ, TK), F32),
                        pltpu.VMEM((N_PAIRS, TQ, 2 * LANES), F32)],
        compiler_params=_cparams(2),
        name="dsa",
    )(qd, qi, wi, kk, vv, kki)


def _mix_kernel(h_ref, n_ref, ysb_ref, yd_ref, mod_ref, wps_ref, wpd_ref, wg_ref, bg_ref,
                wo_ref, o_ref):
    d = h_ref.shape[1]
    y_sb = _dot(ysb_ref[...], wps_ref[...])
    y_d = _dot(yd_ref[...], wpd_ref[...])
    gates = jax.nn.sigmoid(_dot(n_ref[...], wg_ref[...]) + bg_ref[...])
    merged = (gates[:, :d] * y_sb + gates[:, d:] * y_d).astype(BF16)
    o_ref[...] = h_ref[...] + mod_ref[5] * _dot(merged, wo_ref[...])


def _mix_call(h, n2, y_sb, y_d, mod, w_proj_sb, w_proj_dsa, w_gate, b_gate, w_out, *, seq, tm=512):
    n_tok, d = h.shape
    tok = lambda w: pl.BlockSpec((tm, w), lambda i: (i, 0))
    ws = [w.astype(BF16) for w in (w_proj_sb, w_proj_dsa, w_gate)]
    bg = b_gate.reshape(1, -1)
    wo = w_out.astype(BF16)
    return pl.pallas_call(
        _mix_kernel,
        out_shape=jax.ShapeDtypeStruct((n_tok, d), F32),
        grid=(n_tok // tm,),
        in_specs=[tok(d), tok(d), tok(W_HEADS), tok(W_HEADS), _mod_spec(seq // tm, d)]
                 + [_const_spec(w.shape) for w in (*ws, bg, wo)],
        out_specs=tok(d),
        compiler_params=_cparams(1),
        name="mix",
    )(h, n2, y_sb, y_d, mod, *ws, bg, wo)


def kernel(x, c, w_ada, b_ada, g_ffn1, w1_ffn1, w3_ffn1, w2_ffn1, g_mix, w_in, g_q_dsa, g_k_dsa,
           w_proj_sb, w_proj_dsa, w_gate, b_gate, w_out, g_ffn2, w1_ffn2, w3_ffn2, w2_ffn2):
    batch, seq, d = x.shape
    h = x.reshape(batch * seq, d)
    for l in range(w_ada.shape[0]):
        bf = lambda w: w[l].astype(BF16)
        mod = _mod_call(c, w_ada[l], b_ada[l])
        h, n2 = _ffn_call(h, mod, g_ffn1[l], bf(w1_ffn1), bf(w3_ffn1), bf(w2_ffn1), g_mix[l],
                          mod_base=0, emit_next=True, seq=seq)
        q_sb, k_sb, v_sb, q_d, kk, vv, q_i, kki, w_i = _inproj_call(
            n2, w_in[l], g_q_dsa[l], g_k_dsa[l])
        y_sb = _sb_call(q_sb, k_sb, v_sb, batch, seq).reshape(batch * seq, W_HEADS)
        y_d = _dsa_call(q_d, q_i, w_i, kk, vv, kki, batch, seq).reshape(batch * seq, W_HEADS)
        h = _mix_call(h, n2, y_sb, y_d, mod, w_proj_sb[l], w_proj_dsa[l], w_gate[l], b_gate[l],
                      w_out[l], seq=seq)
        (h,) = _ffn_call(h, mod, g_ffn2[l], bf(w1_ffn2), bf(w3_ffn2), bf(w2_ffn2), g_ffn2[l],
                         mod_base=6, emit_next=False, seq=seq)
    return h.reshape(batch, seq, d)
```

```python
import functools

import numpy as np
import jax
import jax.numpy as jnp
from jax import lax
from jax.experimental import pallas as pl
from jax.experimental.pallas import tpu as pltpu

F32 = jnp.float32
BF16 = jnp.bfloat16

HEAD_DIM = 64
N_HEADS = 8
W_HEADS = N_HEADS * HEAD_DIM
N_PAIRS = N_HEADS // 2
LANES = 128
SUBLANES = 8
TOPK_MAX = 256
N_MOD = 9
RMS_EPS = 1e-6
D_IDX = 64

TQ = 128
TK = 128
CH_S = 512
CH_A = 256
VT_ROWS = 80
LANE_POS_HI, LANE_POS_LO, LANE_ONE = HEAD_DIM, HEAD_DIM + 1, HEAD_DIM + 2
NEG_BIG = -1e30
INT_MIN = np.int32(-2**31)
VMEM_LIMIT = 56 * 1024 * 1024


def _cparams(n_axes):
    return pltpu.CompilerParams(
        dimension_semantics=("arbitrary",) * n_axes, vmem_limit_bytes=VMEM_LIMIT)


def _const_spec(shape):
    nd = len(shape)
    return pl.BlockSpec(shape, lambda *_: (0,) * nd, pipeline_mode=pl.Buffered(1))


def _dot(a, b):
    return jnp.dot(a, b, preferred_element_type=F32)


def _dot_nt(a, b):
    return lax.dot_general(a, b, (((1,), (1,)), ((), ())), preferred_element_type=F32)


def _mod_kernel(c_ref, w_ref, b_ref, o_ref):
    nb, kdim, _ = c_ref.shape
    tn = w_ref.shape[1]
    nch = tn // LANES

    def body(kb, accs):
        k0 = pl.multiple_of(kb * SUBLANES, SUBLANES)
        wblk = w_ref[pl.ds(k0, SUBLANES), :]
        out = []
        for b in range(nb):
            cb = c_ref[b, pl.ds(k0, SUBLANES), :]
            sb = cb * jax.nn.sigmoid(cb)
            for ch in range(nch):
                out.append(accs[b * nch + ch] + wblk[:, ch * LANES:(ch + 1) * LANES] * sb)
        return tuple(out)

    accs = lax.fori_loop(0, kdim // SUBLANES, body,
                         tuple(jnp.zeros((SUBLANES, LANES), F32) for _ in range(nb * nch)))
    for b in range(nb):
        row = jnp.concatenate(
            [jnp.sum(accs[b * nch + ch], axis=0, keepdims=True) for ch in range(nch)], axis=1)
        o_ref[0, b] = row + b_ref[...]


def _mod_call(c, w_ada, b_ada):
    nb, d = c.shape
    n_out = w_ada.shape[1]
    c_rep = jnp.broadcast_to(c[:, :, None], (nb, d, LANES))
    return pl.pallas_call(
        _mod_kernel,
        out_shape=jax.ShapeDtypeStruct((n_out // d, nb, 1, d), F32),
        grid=(n_out // d,),
        in_specs=[_const_spec((nb, d, LANES)),
                  pl.BlockSpec((d, d), lambda j: (0, j)),
                  pl.BlockSpec((1, d), lambda j: (0, j))],
        out_specs=pl.BlockSpec((1, nb, 1, d), lambda j: (j, 0, 0, 0)),
        compiler_params=_cparams(1),
        name="mod",
    )(c_rep, w_ada, b_ada.reshape(1, n_out))


def _norm_mod(h, g, shift, scale):
    y = h * lax.rsqrt(jnp.mean(h * h, axis=-1, keepdims=True) + RMS_EPS)
    return (y * g) * (1.0 + scale) + shift


def _ffn_kernel(h_ref, mod_ref, g_ref, w1_ref, w3_ref, w2_ref, gn_ref, *out_refs,
                mod_base, emit_next, n_chunks):
    o_ref = out_refs[0]
    h = h_ref[...]
    n = _norm_mod(h, g_ref[...], mod_ref[mod_base], mod_ref[mod_base + 1]).astype(BF16)
    fc = w1_ref.shape[1] // n_chunks
    acc = jnp.zeros(h.shape, F32)
    for c in range(n_chunks):
        a = _dot(n, w1_ref[:, c * fc:(c + 1) * fc])
        b = _dot(n, w3_ref[:, c * fc:(c + 1) * fc])
        act = (a * jax.nn.sigmoid(a) * b).astype(BF16)
        acc = acc + _dot(act, w2_ref[c * fc:(c + 1) * fc, :])
    hn = h + (0.5 * mod_ref[mod_base + 2]) * acc
    o_ref[...] = hn
    if emit_next:
        out_refs[1][...] = _norm_mod(hn, gn_ref[...], mod_ref[mod_base + 3],
                                     mod_ref[mod_base + 4]).astype(BF16)


def _mod_spec(tiles_per_batch, d):
    return pl.BlockSpec((N_MOD, None, 1, d), lambda i: (0, i // tiles_per_batch, 0, 0))


def _ffn_call(h, mod, g, w1, w3, w2, g_next, *, mod_base, emit_next, seq, tm=512):
    n_tok, d = h.shape
    f = w1.shape[1]
    tok = pl.BlockSpec((tm, d), lambda i: (i, 0))
    out_shape = [jax.ShapeDtypeStruct((n_tok, d), F32)]
    out_specs = [tok]
    if emit_next:
        out_shape.append(jax.ShapeDtypeStruct((n_tok, d), BF16))
        out_specs.append(tok)
    return pl.pallas_call(
        functools.partial(_ffn_kernel, mod_base=mod_base, emit_next=emit_next, n_chunks=2),
        out_shape=out_shape,
        grid=(n_tok // tm,),
        in_specs=[tok, _mod_spec(seq // tm, d), _const_spec((1, d)),
                  _const_spec((d, f)), _const_spec((d, f)), _const_spec((f, d)),
                  _const_spec((1, d))],
        out_specs=out_specs,
        compiler_params=_cparams(1),
        name="ffn",
    )(h, mod, g.reshape(1, d), w1, w3, w2, g_next.reshape(1, d))


def _inproj_kernel(n_ref, wsb_ref, wqd_ref, wkd_ref, wqi_ref, wki_ref, wvt_ref, wwt_ref,
                   gq_ref, gk_ref, hm_ref,
                   qsb_ref, ksb_ref, vsb_ref, qd_ref, kk_ref, qi_ref, kki_ref, vt_ref, wit_ref,
                   *, tiles_per_batch):
    n = n_ref[...]
    tm = n.shape[0]
    qk_scale = HEAD_DIM ** -0.5
    sb = _dot(n, wsb_ref[...])
    qsb_ref[...] = (sb[:, :W_HEADS] * qk_scale).astype(BF16)
    ksb_ref[...] = sb[:, W_HEADS:2 * W_HEADS].astype(BF16)
    vsb_ref[...] = sb[:, 2 * W_HEADS:].astype(BF16)

    qd = _dot(n, wqd_ref[...])
    ms = _dot((qd * qd).astype(BF16), hm_ref[...])
    qd_ref[...] = (qd * lax.rsqrt(ms + RMS_EPS) * (gq_ref[...] * qk_scale)).astype(BF16)

    kd = _dot(n, wkd_ref[...])
    kn = kd * lax.rsqrt(jnp.mean(kd * kd, axis=-1, keepdims=True) + RMS_EPS) * gk_ref[...]
    lane = lax.broadcasted_iota(jnp.int32, (tm, LANES), 1)
    pos = ((pl.program_id(0) % tiles_per_batch) * tm
           + lax.broadcasted_iota(jnp.int32, (tm, LANES), 0))
    feat = jnp.where(lane == LANE_POS_HI, pos >> (HEAD_DIM.bit_length() - 1),
                     jnp.where(lane == LANE_POS_LO, pos & (HEAD_DIM - 1),
                               jnp.where(lane == LANE_ONE, 1, 0))).astype(F32)
    kk_ref[...] = jnp.where(lane < HEAD_DIM, kn, feat).astype(BF16)

    qi_ref[...] = _dot(n, wqi_ref[...]).astype(BF16)
    kki_ref[...] = _dot(n, wki_ref[...]).astype(BF16)
    vt = _dot_nt(wvt_ref[...], n)
    row = lax.broadcasted_iota(jnp.int32, vt.shape, 0)
    vt_ref[...] = jnp.where(row == HEAD_DIM, 1.0, vt).astype(BF16)
    wit_ref[...] = _dot_nt(wwt_ref[...], n)[:N_HEADS]


def _inproj_call(n2, w_in, g_q, g_k, batch, seq, tm=512):
    n_tok, d = n2.shape
    sizes = (W_HEADS, W_HEADS, W_HEADS, W_HEADS, HEAD_DIM, HEAD_DIM, N_HEADS * D_IDX, D_IDX, N_HEADS)
    offs = np.concatenate([[0], np.cumsum(sizes)])
    col = lambda k: w_in[:, offs[k]:offs[k + 1]]
    idx_scale = (D_IDX ** -0.5) * (N_HEADS ** -0.5)
    w_sb = jnp.concatenate([col(0), col(1), col(2)], axis=1).astype(BF16)
    w_qd = col(3).astype(BF16)
    w_kd = jnp.concatenate([col(4), col(4)], axis=1).astype(BF16)
    w_qi = col(6).astype(BF16)
    w_ki = jnp.pad(col(7), ((0, 0), (0, LANES - D_IDX))).astype(BF16)
    w_vt = jnp.pad(col(5).T, ((0, VT_ROWS - HEAD_DIM), (0, 0))).astype(BF16)
    w_wt = jnp.pad((col(8) * idx_scale).T, ((0, 16 - N_HEADS), (0, 0))).astype(BF16)
    gq = jnp.tile(g_q, N_HEADS).reshape(1, W_HEADS)
    gk = jnp.tile(g_k, 2).reshape(1, LANES)
    head_of = np.arange(W_HEADS) // HEAD_DIM
    head_mean = jnp.asarray((head_of[:, None] == head_of[None, :]) / HEAD_DIM, BF16)

    tpb = seq // tm
    tok = lambda w: pl.BlockSpec((tm, w), lambda i: (i, 0))
    tr = lambda r: pl.BlockSpec((None, r, tm), lambda i: (i // tpb, 0, i % tpb))
    sd = lambda w, dt: jax.ShapeDtypeStruct((n_tok, w), dt)
    consts = (w_sb, w_qd, w_kd, w_qi, w_ki, w_vt, w_wt, gq, gk, head_mean)
    return pl.pallas_call(
        functools.partial(_inproj_kernel, tiles_per_batch=tpb),
        out_shape=[sd(W_HEADS, BF16)] * 4 + [sd(LANES, BF16), sd(W_HEADS, BF16), sd(LANES, BF16),
                                             jax.ShapeDtypeStruct((batch, VT_ROWS, seq), BF16),
                                             jax.ShapeDtypeStruct((batch, N_HEADS, seq), F32)],
        grid=(n_tok // tm,),
        in_specs=[tok(d)] + [_const_spec(w.shape) for w in consts],
        out_specs=[tok(W_HEADS)] * 4 + [tok(LANES), tok(W_HEADS), tok(LANES),
                                        tr(VT_ROWS), tr(N_HEADS)],
        compiler_params=_cparams(1),
        name="inproj",
    )(n2, *consts)


def _split_heads(x, lane):
    xf = x.astype(F32)
    lo = jnp.where(lane < HEAD_DIM, xf, 0.0).astype(x.dtype)
    hi = jnp.where(lane >= HEAD_DIM, xf, 0.0).astype(x.dtype)
    return lo, hi


def _sb_kernel(q_ref, k_ref, v_ref, t_ref, o_ref, qs_ref, run_ref, acc_ref):
    i = pl.program_id(1)
    lane = lax.broadcasted_iota(jnp.int32, (TQ, LANES), 1)
    lane_k = lax.broadcasted_iota(jnp.int32, (TK, LANES), 1)
    for p in range(N_PAIRS):
        lo, hi = _split_heads(q_ref[:, p * LANES:(p + 1) * LANES], lane)
        qs_ref[p, :TQ, :] = lo
        qs_ref[p, TQ:, :] = hi

    def block(k0, diag):
        pairs = range(N_PAIRS)
        strict = (lax.broadcasted_iota(jnp.int32, (2 * TQ, TK), 1)
                  < lax.broadcasted_iota(jnp.int32, (2 * TQ, TK), 0) % TQ) if diag else None
        zs = [_dot_nt(qs_ref[p], k_ref[pl.ds(k0, TK), p * LANES:(p + 1) * LANES]) for p in pairs]
        lbs, ws = [], []
        for p in pairs:
            z = zs[p]
            neg_abs = pltpu.bitcast(pltpu.bitcast(z, jnp.int32) | INT_MIN, F32)
            sp = jnp.maximum(z, 0.0) + jnp.log(1.0 + jnp.exp(neg_abs))
            l1m = jnp.where(strict, sp, 0.0) if diag else sp
            l_hi = l1m.astype(BF16)
            l_lo = (l1m - l_hi.astype(F32)).astype(BF16)
            ws.append(_dot(jnp.concatenate([l_hi, l_lo], axis=1), t_ref[...]))
            lbs.append(z - sp)
        for p in pairs:
            if diag:
                a = jnp.where(strict, jnp.exp(lbs[p] + ws[p][:, :TK]), 0.0).astype(BF16)
            else:
                a = jnp.exp(lbs[p] + ws[p][:, :TK] + run_ref[p]).astype(BF16)
            v_lo, v_hi = _split_heads(v_ref[pl.ds(k0, TK), p * LANES:(p + 1) * LANES], lane_k)
            upd = _dot(jnp.concatenate([a[:TQ], a[TQ:]], axis=1),
                       jnp.concatenate([v_lo, v_hi], axis=0))
            if diag:
                acc_ref[p] = upd
                run_ref[p] = ws[p][:, TK:]
            else:
                acc_ref[p] += upd
                run_ref[p] += ws[p][:, TK:]

    block(pl.multiple_of(i * TK, TK), True)

    def body(jj, _):
        block(pl.multiple_of((i - jj) * TK, TK), False)
        return 0

    lax.fori_loop(1, i + 1, body, 0)
    for p in range(N_PAIRS):
        o_ref[:, p * LANES:(p + 1) * LANES] = acc_ref[p].astype(o_ref.dtype)


def _sb_call(q, k, v, batch, seq):
    q, k, v = (t.reshape(batch, seq, W_HEADS) for t in (q, k, v))
    j_idx = np.arange(TK)
    suffix = (j_idx[:, None] > j_idx[None, :]).astype(np.float32)
    tmat = -np.concatenate([suffix, np.ones((TK, LANES), np.float32)], axis=1)
    tmat = jnp.asarray(np.concatenate([tmat, tmat], axis=0), BF16)
    blk = pl.BlockSpec((None, TQ, W_HEADS), lambda b, i: (b, i, 0))
    allk = pl.BlockSpec((None, seq, W_HEADS), lambda b, i: (b, 0, 0))
    return pl.pallas_call(
        _sb_kernel,
        out_shape=jax.ShapeDtypeStruct((batch, seq, W_HEADS), BF16),
        grid=(batch, seq // TQ),
        in_specs=[blk, allk, allk, _const_spec(tmat.shape)],
        out_specs=blk,
        scratch_shapes=[pltpu.VMEM((N_PAIRS, 2 * TQ, LANES), BF16),
                        pltpu.VMEM((N_PAIRS, 2 * TQ, LANES), F32),
                        pltpu.VMEM((N_PAIRS, TQ, LANES), F32)],
        compiler_params=_cparams(2),
        name="sb",
    )(q, k, v, tmat)


def _dsa_kernel(qd_ref, qi_ref, wit_ref, kk_ref, vt_ref, kki_ref, o_ref,
                qd_st, qi_st, keys_ref, m_ref, acc_ref, *, topk):
    i = pl.program_id(1)
    t0 = i * TQ
    lane = lax.broadcasted_iota(jnp.int32, (TQ, LANES), 1)
    slopes = [float(2.0 ** (-8.0 * (h + 1) / N_HEADS)) for h in range(N_HEADS)]
    t0_f = t0.astype(F32)
    for h in range(N_HEADS):
        p = h // 2
        for src, dst, alibi in ((qd_ref, qd_st, True), (qi_ref, qi_st, False)):
            x = src[:, p * LANES:(p + 1) * LANES].astype(F32)
            if h % 2:
                x = pltpu.roll(x, HEAD_DIM, axis=1)
            x = jnp.where(lane < HEAD_DIM, x, 0.0)
            if alibi:
                x = jnp.where(lane == LANE_POS_HI, slopes[h] * HEAD_DIM, x)
                x = jnp.where(lane == LANE_POS_LO, slopes[h], x)
                x = jnp.where(lane == LANE_ONE, -slopes[h] * t0_f, x)
            dst[h * TQ:(h + 1) * TQ, :] = x.astype(BF16)

    n_s = (i + CH_S // TQ) // (CH_S // TQ)
    n_a = (i + CH_A // TQ) // (CH_A // TQ)

    wit = wit_ref[...]

    def score_body(c, _):
        k0 = pl.multiple_of(c * CH_S, CH_S)
        kch = kki_ref[pl.ds(k0, CH_S), :]
        xs = [_dot_nt(kch, qi_st[p * 2 * TQ:(p + 1) * 2 * TQ, :]) for p in range(N_PAIRS)]
        score = None
        for h in range(N_HEADS):
            x_h = xs[h // 2][:, (h % 2) * TQ:(h % 2 + 1) * TQ]
            term = jnp.maximum(x_h, 0.0) * wit[h:h + 1, :]
            score = term if score is None else score + term
        bits = pltpu.bitcast(score, jnp.int32)
        key = jnp.where(bits < 0, bits ^ np.int32(0x7FFFFFFF), bits)
        causal = (k0 + lax.broadcasted_iota(jnp.int32, (CH_S, TQ), 0)
                  <= t0 + lax.broadcasted_iota(jnp.int32, (CH_S, TQ), 1))
        keys_ref[pl.ds(k0, CH_S), :] = jnp.where(causal, key, INT_MIN)
        return 0

    lax.fori_loop(0, n_s, score_body, 0)

    def count(pred):
        def cbody(c, acc):
            k0 = pl.multiple_of(c * CH_S, CH_S)
            kch = keys_ref[pl.ds(k0, CH_S), :].reshape(CH_S // SUBLANES, SUBLANES, TQ)
            return acc + jnp.sum(jnp.where(pred(kch), 1, 0), axis=0)
        acc = lax.fori_loop(0, n_s, cbody, jnp.zeros((SUBLANES, TQ), jnp.int32))
        tot = jnp.sum(acc.astype(F32), axis=0, keepdims=True)
        return jnp.broadcast_to(tot, (SUBLANES, TQ))

    def bit_body(b, prefix):
        cand = prefix | jnp.left_shift(jnp.int32(1), 31 - b)
        cand_s = cand ^ INT_MIN
        return jnp.where(count(lambda kch: kch >= cand_s[None]) >= topk, cand, prefix)

    prefix = lax.fori_loop(0, 32, bit_body, jnp.zeros((SUBLANES, TQ), jnp.int32))
    thr = jnp.maximum(prefix ^ INT_MIN, INT_MIN + 1)
    thr_row = thr[0:1, :]

    n_ge = count(lambda kch: kch >= thr[None])

    @pl.when(jnp.max(n_ge) > topk)
    def _():
        need = topk - count(lambda kch: kch > thr[None])[0:1, :]
        before = jnp.where(lax.broadcasted_iota(jnp.int32, (CH_A, CH_A), 1)
                           < lax.broadcasted_iota(jnp.int32, (CH_A, CH_A), 0), 1.0, 0.0).astype(BF16)

        def tie_body(c, seen):
            k0 = pl.multiple_of(c * CH_A, CH_A)
            kch = keys_ref[pl.ds(k0, CH_A), :]
            eq = jnp.where(kch == thr_row, 1.0, 0.0)
            rank = _dot(before, eq.astype(BF16)) + seen
            tie_rank = jnp.where(kch == thr_row, rank, -1.0)
            keys_ref[pl.ds(k0, CH_A), :] = jnp.where(tie_rank >= need, INT_MIN, kch)
            return seen + jnp.sum(eq, axis=0, keepdims=True)

        lax.fori_loop(0, n_a, tie_body, jnp.zeros((1, TQ), F32))

    m_ref[...] = jnp.full(m_ref.shape, NEG_BIG, F32)
    acc_ref[...] = jnp.zeros(acc_ref.shape, F32)

    def attn_body(c, _):
        k0 = pl.multiple_of(c * CH_A, CH_A)
        kch = kk_ref[pl.ds(k0, CH_A), :]
        vch = vt_ref[:, pl.ds(k0, CH_A)]
        sel = keys_ref[pl.ds(k0, CH_A), :] >= thr_row
        logits = [_dot_nt(kch, qd_st[p * 2 * TQ:(p + 1) * 2 * TQ, :]) for p in range(N_PAIRS)]
        for p in range(N_PAIRS):
            cols = slice(p * 2 * TQ, (p + 1) * 2 * TQ)
            probs, alphas = [], []
            for e in range(2):
                lg = jnp.where(sel, logits[p][:, e * TQ:(e + 1) * TQ], NEG_BIG)
                m_old = m_ref[:, (2 * p + e) * TQ:(2 * p + e + 1) * TQ]
                m_new = jnp.maximum(m_old, jnp.max(lg, axis=0, keepdims=True))
                m_ref[:, (2 * p + e) * TQ:(2 * p + e + 1) * TQ] = m_new
                alphas.append(jnp.exp(m_old - m_new))
                probs.append(jnp.exp(lg - m_new).astype(BF16))
            upd = _dot(vch, jnp.concatenate(probs, axis=1))
            acc_ref[:, cols] = acc_ref[:, cols] * jnp.concatenate(alphas, axis=1) + upd
        return 0

    lax.fori_loop(0, n_a, attn_body, 0)
    acc = acc_ref[...]
    out_t = acc[:HEAD_DIM] / acc[HEAD_DIM:HEAD_DIM + 1]
    for p in range(N_PAIRS):
        pair = jnp.concatenate([out_t[:, (2 * p) * TQ:(2 * p + 1) * TQ],
                                out_t[:, (2 * p + 1) * TQ:(2 * p + 2) * TQ]], axis=0)
        o_ref[:, p * LANES:(p + 1) * LANES] = pair.T.astype(o_ref.dtype)


def _dsa_call(qd, qi, wit, kk, vt, kki, batch, seq):
    r3 = lambda t: t.reshape(batch, seq, t.shape[-1])
    qd, qi, kk, kki = map(r3, (qd, qi, kk, kki))
    topk = min(TOPK_MAX, seq // 4)
    blk = lambda w: pl.BlockSpec((None, TQ, w), lambda b, i: (b, i, 0))
    allk = pl.BlockSpec((None, seq, LANES), lambda b, i: (b, 0, 0))
    return pl.pallas_call(
        functools.partial(_dsa_kernel, topk=topk),
        out_shape=jax.ShapeDtypeStruct((batch, seq, W_HEADS), BF16),
        grid=(batch, seq // TQ),
        in_specs=[blk(W_HEADS), blk(W_HEADS),
                  pl.BlockSpec((None, N_HEADS, TQ), lambda b, i: (b, 0, i)),
                  allk, pl.BlockSpec((None, VT_ROWS, seq), lambda b, i: (b, 0, 0)), allk],
        out_specs=blk(W_HEADS),
        scratch_shapes=[pltpu.VMEM((N_HEADS * TQ, LANES), BF16),
                        pltpu.VMEM((N_HEADS * TQ, LANES), BF16),
                        pltpu.VMEM((seq, TQ), jnp.int32),
                        pltpu.VMEM((1, N_HEADS * TQ), F32),
                        pltpu.VMEM((VT_ROWS, N_HEADS * TQ), F32)],
        compiler_params=_cparams(2),
        name="dsa",
    )(qd, qi, wit, kk, vt, kki)


def _mix_kernel(h_ref, n_ref, ysb_ref, yd_ref, mod_ref, wps_ref, wpd_ref, wg_ref, bg_ref,
                wo_ref, o_ref):
    d = h_ref.shape[1]
    y_sb = _dot(ysb_ref[...], wps_ref[...])
    y_d = _dot(yd_ref[...], wpd_ref[...])
    gates = jax.nn.sigmoid(_dot(n_ref[...], wg_ref[...]) + bg_ref[...])
    merged = (gates[:, :d] * y_sb + gates[:, d:] * y_d).astype(BF16)
    o_ref[...] = h_ref[...] + mod_ref[5] * _dot(merged, wo_ref[...])


def _mix_call(h, n2, y_sb, y_d, mod, w_proj_sb, w_proj_dsa, w_gate, b_gate, w_out, *, seq, tm=512):
    n_tok, d = h.shape
    tok = lambda w: pl.BlockSpec((tm, w), lambda i: (i, 0))
    ws = [w.astype(BF16) for w in (w_proj_sb, w_proj_dsa, w_gate)]
    bg = b_gate.reshape(1, -1)
    wo = w_out.astype(BF16)
    return pl.pallas_call(
        _mix_kernel,
        out_shape=jax.ShapeDtypeStruct((n_tok, d), F32),
        grid=(n_tok // tm,),
        in_specs=[tok(d), tok(d), tok(W_HEADS), tok(W_HEADS), _mod_spec(seq // tm, d)]
                 + [_const_spec(w.shape) for w in (*ws, bg, wo)],
        out_specs=tok(d),
        compiler_params=_cparams(1),
        name="mix",
    )(h, n2, y_sb, y_d, mod, *ws, bg, wo)


def kernel(x, c, w_ada, b_ada, g_ffn1, w1_ffn1, w3_ffn1, w2_ffn1, g_mix, w_in, g_q_dsa, g_k_dsa,
           w_proj_sb, w_proj_dsa, w_gate, b_gate, w_out, g_ffn2, w1_ffn2, w3_ffn2, w2_ffn2):
    batch, seq, d = x.shape
    h = x.reshape(batch * seq, d)
    for l in range(w_ada.shape[0]):
        bf = lambda w: w[l].astype(BF16)
        mod = _mod_call(c, w_ada[l], b_ada[l])
        h, n2 = _ffn_call(h, mod, g_ffn1[l], bf(w1_ffn1), bf(w3_ffn1), bf(w2_ffn1), g_mix[l],
                          mod_base=0, emit_next=True, seq=seq)
        q_sb, k_sb, v_sb, q_d, kk, q_i, kki, vt, wit = _inproj_call(
            n2, w_in[l], g_q_dsa[l], g_k_dsa[l], batch, seq)
        y_sb = _sb_call(q_sb, k_sb, v_sb, batch, seq).reshape(batch * seq, W_HEADS)
        y_d = _dsa_call(q_d, q_i, wit, kk, vt, kki, batch, seq).reshape(batch * seq, W_HEADS)
        h = _mix_call(h, n2, y_sb, y_d, mod, w_proj_sb[l], w_proj_dsa[l], w_gate[l], b_gate[l],
                      w_out[l], seq=seq)
        (h,) = _ffn_call(h, mod, g_ffn2[l], bf(w1_ffn2), bf(w3_ffn2), bf(w2_ffn2), g_ffn2[l],
                         mod_base=6, emit_next=False, seq=seq)
    return h.reshape(batch, seq, d)
```

```python
import functools

import numpy as np
import jax
import jax.numpy as jnp
from jax import lax
from jax.experimental import pallas as pl
from jax.experimental.pallas import tpu as pltpu

F32 = jnp.float32
BF16 = jnp.bfloat16

HEAD_DIM = 64
N_HEADS = 8
W_HEADS = N_HEADS * HEAD_DIM
N_PAIRS = N_HEADS // 2
LANES = 128
SUBLANES = 8
TOPK_MAX = 256
N_MOD = 9
RMS_EPS = 1e-6
D_IDX = 64

TQ = 128
TK = 128
CH_S = 512
CH_A = 512
VT_ROWS = 80
LANE_POS_HI, LANE_POS_LO = HEAD_DIM, HEAD_DIM + 1
N_POS_LANES = 2
N_SLOPE_PARTS = 3
LOG2E = float(np.log2(np.e))
NEG_BIG = -1e30
F32_EXP_UNDERFLOW = 104.0
INT_MIN = np.int32(-2**31)
VMEM_LIMIT = 56 * 1024 * 1024


def _cparams(n_axes):
    return pltpu.CompilerParams(
        dimension_semantics=("arbitrary",) * n_axes, vmem_limit_bytes=VMEM_LIMIT)


def _const_spec(shape):
    nd = len(shape)
    return pl.BlockSpec(shape, lambda *_: (0,) * nd, pipeline_mode=pl.Buffered(1))


def _dot(a, b):
    return jnp.dot(a, b, preferred_element_type=F32)


def _dot_nt(a, b):
    return lax.dot_general(a, b, (((1,), (1,)), ((), ())), preferred_element_type=F32)


def _mod_kernel(c_ref, w_ref, b_ref, o_ref):
    nb, kdim, _ = c_ref.shape
    tn = w_ref.shape[1]
    nch = tn // LANES

    def body(kb, accs):
        k0 = pl.multiple_of(kb * SUBLANES, SUBLANES)
        wblk = w_ref[pl.ds(k0, SUBLANES), :]
        out = []
        for b in range(nb):
            cb = c_ref[b, pl.ds(k0, SUBLANES), :]
            sb = cb * jax.nn.sigmoid(cb)
            for ch in range(nch):
                out.append(accs[b * nch + ch] + wblk[:, ch * LANES:(ch + 1) * LANES] * sb)
        return tuple(out)

    accs = lax.fori_loop(0, kdim // SUBLANES, body,
                         tuple(jnp.zeros((SUBLANES, LANES), F32) for _ in range(nb * nch)))
    for b in range(nb):
        row = jnp.concatenate(
            [jnp.sum(accs[b * nch + ch], axis=0, keepdims=True) for ch in range(nch)], axis=1)
        o_ref[0, b] = row + b_ref[...]


def _mod_call(c, w_ada, b_ada):
    nb, d = c.shape
    n_out = w_ada.shape[1]
    c_rep = jnp.broadcast_to(c[:, :, None], (nb, d, LANES))
    return pl.pallas_call(
        _mod_kernel,
        out_shape=jax.ShapeDtypeStruct((n_out // d, nb, 1, d), F32),
        grid=(n_out // d,),
        in_specs=[_const_spec((nb, d, LANES)),
                  pl.BlockSpec((d, d), lambda j: (0, j)),
                  pl.BlockSpec((1, d), lambda j: (0, j))],
        out_specs=pl.BlockSpec((1, nb, 1, d), lambda j: (j, 0, 0, 0)),
        compiler_params=_cparams(1),
        name="mod",
    )(c_rep, w_ada, b_ada.reshape(1, n_out))


def _norm_mod(h, g, shift, scale):
    y = h * lax.rsqrt(jnp.mean(h * h, axis=-1, keepdims=True) + RMS_EPS)
    return (y * g) * (1.0 + scale) + shift


def _ffn_kernel(h_ref, mod_ref, g_ref, w1_ref, w3_ref, w2_ref, gn_ref, *out_refs,
                mod_base, emit_next, n_chunks):
    o_ref = out_refs[0]
    h = h_ref[...]
    n = _norm_mod(h, g_ref[...], mod_ref[mod_base], mod_ref[mod_base + 1]).astype(BF16)
    fc = w1_ref.shape[1] // n_chunks
    acc = jnp.zeros(h.shape, F32)
    for c in range(n_chunks):
        a = _dot(n, w1_ref[:, c * fc:(c + 1) * fc])
        b = _dot(n, w3_ref[:, c * fc:(c + 1) * fc])
        act = (a * jax.nn.sigmoid(a) * b).astype(BF16)
        acc = acc + _dot(act, w2_ref[c * fc:(c + 1) * fc, :])
    hn = h + (0.5 * mod_ref[mod_base + 2]) * acc
    o_ref[...] = hn
    if emit_next:
        out_refs[1][...] = _norm_mod(hn, gn_ref[...], mod_ref[mod_base + 3],
                                     mod_ref[mod_base + 4]).astype(BF16)


def _mod_spec(tiles_per_batch, d):
    return pl.BlockSpec((N_MOD, None, 1, d), lambda i: (0, i // tiles_per_batch, 0, 0))


def _ffn_call(h, mod, g, w1, w3, w2, g_next, *, mod_base, emit_next, seq, tm=512):
    n_tok, d = h.shape
    f = w1.shape[1]
    tok = pl.BlockSpec((tm, d), lambda i: (i, 0))
    out_shape = [jax.ShapeDtypeStruct((n_tok, d), F32)]
    out_specs = [tok]
    if emit_next:
        out_shape.append(jax.ShapeDtypeStruct((n_tok, d), BF16))
        out_specs.append(tok)
    return pl.pallas_call(
        functools.partial(_ffn_kernel, mod_base=mod_base, emit_next=emit_next, n_chunks=2),
        out_shape=out_shape,
        grid=(n_tok // tm,),
        in_specs=[tok, _mod_spec(seq // tm, d), _const_spec((1, d)),
                  _const_spec((d, f)), _const_spec((d, f)), _const_spec((f, d)),
                  _const_spec((1, d))],
        out_specs=out_specs,
        compiler_params=_cparams(1),
        name="ffn",
    )(h, mod, g.reshape(1, d), w1, w3, w2, g_next.reshape(1, d))


def _inproj_kernel(n_ref, wsb_ref, wqd_ref, wkd_ref, wqi_ref, wki_ref, wvt_ref, wwt_ref,
                   gq_ref, gk_ref, hm_ref,
                   qsb_ref, ksb_ref, vsb_ref, qd_ref, kk_ref, qi_ref, kki_ref, vt_ref, wit_ref,
                   *, tiles_per_batch):
    n = n_ref[...]
    tm = n.shape[0]
    qk_scale = HEAD_DIM ** -0.5
    sb = _dot(n, wsb_ref[...])
    qsb_ref[...] = (sb[:, :W_HEADS] * qk_scale).astype(BF16)
    ksb_ref[...] = sb[:, W_HEADS:2 * W_HEADS].astype(BF16)
    vsb_ref[...] = sb[:, 2 * W_HEADS:].astype(BF16)

    qd = _dot(n, wqd_ref[...])
    ms = _dot((qd * qd).astype(BF16), hm_ref[...])
    qd_ref[...] = (qd * lax.rsqrt(ms + RMS_EPS) * (gq_ref[...] * (qk_scale * LOG2E))).astype(BF16)

    kd = _dot(n, wkd_ref[...])
    kn = kd * lax.rsqrt(jnp.mean(kd * kd, axis=-1, keepdims=True) + RMS_EPS) * gk_ref[...]
    lane = lax.broadcasted_iota(jnp.int32, (tm, LANES), 1)
    pos = ((pl.program_id(0) % tiles_per_batch) * tm
           + lax.broadcasted_iota(jnp.int32, (tm, LANES), 0))
    feat = jnp.zeros((tm, LANES), jnp.int32)
    for rep in range(N_SLOPE_PARTS):
        feat = jnp.where(lane == LANE_POS_HI + rep * N_POS_LANES,
                         pos >> (HEAD_DIM.bit_length() - 1), feat)
        feat = jnp.where(lane == LANE_POS_LO + rep * N_POS_LANES, pos & (HEAD_DIM - 1), feat)
    feat = feat.astype(F32)
    kk_ref[...] = jnp.where(lane < HEAD_DIM, kn, feat).astype(BF16)

    qi_ref[...] = _dot(n, wqi_ref[...]).astype(BF16)
    kki_ref[...] = _dot(n, wki_ref[...]).astype(BF16)
    vt = _dot_nt(wvt_ref[...], n)
    row = lax.broadcasted_iota(jnp.int32, vt.shape, 0)
    vt_ref[...] = jnp.where(row == HEAD_DIM, 1.0, vt).astype(BF16)
    wit_ref[...] = _dot_nt(wwt_ref[...], n)[:N_HEADS]


def _inproj_call(n2, w_in, g_q, g_k, batch, seq, tm=512):
    n_tok, d = n2.shape
    sizes = (W_HEADS, W_HEADS, W_HEADS, W_HEADS, HEAD_DIM, HEAD_DIM, N_HEADS * D_IDX, D_IDX, N_HEADS)
    offs = np.concatenate([[0], np.cumsum(sizes)])
    col = lambda k: w_in[:, offs[k]:offs[k + 1]]
    idx_scale = (D_IDX ** -0.5) * (N_HEADS ** -0.5)
    w_sb = jnp.concatenate([col(0), col(1), col(2)], axis=1).astype(BF16)
    w_qd = col(3).astype(BF16)
    w_kd = jnp.concatenate([col(4), col(4)], axis=1).astype(BF16)
    w_qi = col(6).astype(BF16)
    w_ki = jnp.pad(col(7), ((0, 0), (0, LANES - D_IDX))).astype(BF16)
    w_vt = jnp.pad(col(5).T, ((0, VT_ROWS - HEAD_DIM), (0, 0))).astype(BF16)
    w_wt = jnp.pad((col(8) * idx_scale).T, ((0, 16 - N_HEADS), (0, 0))).astype(BF16)
    gq = jnp.tile(g_q, N_HEADS).reshape(1, W_HEADS)
    gk = jnp.tile(g_k, 2).reshape(1, LANES)
    head_of = np.arange(W_HEADS) // HEAD_DIM
    head_mean = jnp.asarray((head_of[:, None] == head_of[None, :]) / HEAD_DIM, BF16)

    tpb = seq // tm
    tok = lambda w: pl.BlockSpec((tm, w), lambda i: (i, 0))
    tr = lambda r: pl.BlockSpec((None, r, tm), lambda i: (i // tpb, 0, i % tpb))
    sd = lambda w, dt: jax.ShapeDtypeStruct((n_tok, w), dt)
    consts = (w_sb, w_qd, w_kd, w_qi, w_ki, w_vt, w_wt, gq, gk, head_mean)
    return pl.pallas_call(
        functools.partial(_inproj_kernel, tiles_per_batch=tpb),
        out_shape=[sd(W_HEADS, BF16)] * 4 + [sd(LANES, BF16), sd(W_HEADS, BF16), sd(LANES, BF16),
                                             jax.ShapeDtypeStruct((batch, VT_ROWS, seq), BF16),
                                             jax.ShapeDtypeStruct((batch, N_HEADS, seq), F32)],
        grid=(n_tok // tm,),
        in_specs=[tok(d)] + [_const_spec(w.shape) for w in consts],
        out_specs=[tok(W_HEADS)] * 4 + [tok(LANES), tok(W_HEADS), tok(LANES),
                                        tr(VT_ROWS), tr(N_HEADS)],
        compiler_params=_cparams(1),
        name="inproj",
    )(n2, *consts)


def _split_heads(x, lane):
    xf = x.astype(F32)
    lo = jnp.where(lane < HEAD_DIM, xf, 0.0).astype(x.dtype)
    hi = jnp.where(lane >= HEAD_DIM, xf, 0.0).astype(x.dtype)
    return lo, hi


def _sb_kernel(q_ref, k_ref, v_ref, t_ref, o_ref, qs_ref, run_ref, acc_ref):
    i = pl.program_id(1)
    lane = lax.broadcasted_iota(jnp.int32, (TQ, LANES), 1)
    lane_k = lax.broadcasted_iota(jnp.int32, (TK, LANES), 1)
    for p in range(N_PAIRS):
        lo, hi = _split_heads(q_ref[:, p * LANES:(p + 1) * LANES], lane)
        qs_ref[p, :TQ, :] = lo
        qs_ref[p, TQ:, :] = hi

    def block(k0, diag):
        pairs = range(N_PAIRS)
        strict = (lax.broadcasted_iota(jnp.int32, (2 * TQ, TK), 1)
                  < lax.broadcasted_iota(jnp.int32, (2 * TQ, TK), 0) % TQ) if diag else None
        zs = [_dot_nt(qs_ref[p], k_ref[pl.ds(k0, TK), p * LANES:(p + 1) * LANES]) for p in pairs]
        lbs, ws = [], []
        for p in pairs:
            z = zs[p]
            neg_abs = pltpu.bitcast(pltpu.bitcast(z, jnp.int32) | INT_MIN, F32)
            sp = jnp.maximum(z, 0.0) + jnp.log(1.0 + jnp.exp(neg_abs))
            l1m = jnp.where(strict, sp, 0.0) if diag else sp
            l_hi = l1m.astype(BF16)
            l_lo = (l1m - l_hi.astype(F32)).astype(BF16)
            ws.append(_dot(jnp.concatenate([l_hi, l_lo], axis=1), t_ref[...]))
            lbs.append(z - sp)
        for p in pairs:
            if diag:
                a = jnp.where(strict, jnp.exp(lbs[p] + ws[p][:, :TK]), 0.0).astype(BF16)
            else:
                a = jnp.exp(lbs[p] + ws[p][:, :TK] + run_ref[p]).astype(BF16)
            v_lo, v_hi = _split_heads(v_ref[pl.ds(k0, TK), p * LANES:(p + 1) * LANES], lane_k)
            upd = _dot(jnp.concatenate([a[:TQ], a[TQ:]], axis=1),
                       jnp.concatenate([v_lo, v_hi], axis=0))
            if diag:
                acc_ref[p] = upd
                run_ref[p] = ws[p][:, TK:]
            else:
                acc_ref[p] += upd
                run_ref[p] += ws[p][:, TK:]

    def more_blocks(jj):
        run_max = functools.reduce(jnp.maximum, [jnp.max(run_ref[p]) for p in range(N_PAIRS)])
        return jnp.logical_and(jj <= i, run_max > -F32_EXP_UNDERFLOW)

    block(pl.multiple_of(i * TK, TK), True)

    def body(state):
        jj, _ = state
        block(pl.multiple_of((i - jj) * TK, TK), False)
        return jj + 1, more_blocks(jj + 1)

    lax.while_loop(lambda state: state[1], body, (jnp.int32(1), more_blocks(jnp.int32(1))))
    for p in range(N_PAIRS):
        o_ref[:, p * LANES:(p + 1) * LANES] = acc_ref[p].astype(o_ref.dtype)


def _sb_call(q, k, v, batch, seq):
    q, k, v = (t.reshape(batch, seq, W_HEADS) for t in (q, k, v))
    j_idx = np.arange(TK)
    suffix = (j_idx[:, None] > j_idx[None, :]).astype(np.float32)
    tmat = -np.concatenate([suffix, np.ones((TK, LANES), np.float32)], axis=1)
    tmat = jnp.asarray(np.concatenate([tmat, tmat], axis=0), BF16)
    blk = pl.BlockSpec((None, TQ, W_HEADS), lambda b, i: (b, i, 0))
    allk = pl.BlockSpec((None, seq, W_HEADS), lambda b, i: (b, 0, 0))
    return pl.pallas_call(
        _sb_kernel,
        out_shape=jax.ShapeDtypeStruct((batch, seq, W_HEADS), BF16),
        grid=(batch, seq // TQ),
        in_specs=[blk, allk, allk, _const_spec(tmat.shape)],
        out_specs=blk,
        scratch_shapes=[pltpu.VMEM((N_PAIRS, 2 * TQ, LANES), BF16),
                        pltpu.VMEM((N_PAIRS, 2 * TQ, LANES), F32),
                        pltpu.VMEM((N_PAIRS, TQ, LANES), F32)],
        compiler_params=_cparams(2),
        name="sb",
    )(q, k, v, tmat)


def _dsa_kernel(qd_ref, qi_ref, wit_ref, kk_ref, vt_ref, kki_ref, o_ref,
                qd_st, qi_st, keys_ref, m_ref, acc_ref, *, topk):
    i = pl.program_id(1)
    t0 = i * TQ
    lane = lax.broadcasted_iota(jnp.int32, (TQ, LANES), 1)
    slope_parts = []
    for h in range(N_HEADS):
        rest, parts = np.float32(2.0 ** (-8.0 * (h + 1) / N_HEADS) * LOG2E), []
        for _ in range(N_SLOPE_PARTS):
            parts.append(np.float32(np.asarray(rest, dtype=BF16)))
            rest = np.float32(rest - parts[-1])
        slope_parts.append(parts)
    for h in range(N_HEADS):
        p = h // 2
        for src, dst, alibi in ((qd_ref, qd_st, True), (qi_ref, qi_st, False)):
            x = src[:, p * LANES:(p + 1) * LANES].astype(F32)
            if h % 2:
                x = pltpu.roll(x, HEAD_DIM, axis=1)
            x = jnp.where(lane < HEAD_DIM, x, 0.0)
            if alibi:
                for rep, part in enumerate(slope_parts[h]):
                    x = jnp.where(lane == LANE_POS_HI + rep * N_POS_LANES, part * HEAD_DIM, x)
                    x = jnp.where(lane == LANE_POS_LO + rep * N_POS_LANES, part, x)
            dst[h * TQ:(h + 1) * TQ, :] = x.astype(BF16)

    n_s = (i + CH_S // TQ) // (CH_S // TQ)
    n_a = (i + CH_A // TQ) // (CH_A // TQ)

    wit = wit_ref[...]

    def score_body(c, _):
        k0 = pl.multiple_of(c * CH_S, CH_S)
        kch = kki_ref[pl.ds(k0, CH_S), :]
        xs = [_dot_nt(kch, qi_st[p * 2 * TQ:(p + 1) * 2 * TQ, :]) for p in range(N_PAIRS)]
        score = None
        for h in range(N_HEADS):
            x_h = xs[h // 2][:, (h % 2) * TQ:(h % 2 + 1) * TQ]
            term = jnp.maximum(x_h, 0.0) * wit[h:h + 1, :]
            score = term if score is None else score + term
        bits = pltpu.bitcast(score, jnp.int32)
        key = jnp.where(bits < 0, bits ^ np.int32(0x7FFFFFFF), bits)
        causal = (k0 + lax.broadcasted_iota(jnp.int32, (CH_S, TQ), 0)
                  <= t0 + lax.broadcasted_iota(jnp.int32, (CH_S, TQ), 1))
        keys_ref[pl.ds(k0, CH_S), :] = jnp.where(causal, key, INT_MIN)
        return 0

    lax.fori_loop(0, n_s, score_body, 0)

    def count(pred):
        def cbody(c, acc):
            k0 = pl.multiple_of(c * CH_S, CH_S)
            kch = keys_ref[pl.ds(k0, CH_S), :].reshape(CH_S // SUBLANES, SUBLANES, TQ)
            return acc + jnp.sum(jnp.where(pred(kch), 1, 0), axis=0)
        acc = lax.fori_loop(0, n_s, cbody, jnp.zeros((SUBLANES, TQ), jnp.int32))
        tot = jnp.sum(acc.astype(F32), axis=0, keepdims=True)
        return jnp.broadcast_to(tot, (SUBLANES, TQ))

    def bit_body(b, prefix):
        cand = prefix | jnp.left_shift(jnp.int32(1), 31 - b)
        cand_s = cand ^ INT_MIN
        return jnp.where(count(lambda kch: kch >= cand_s[None]) >= topk, cand, prefix)

    prefix = lax.fori_loop(0, 32, bit_body, jnp.zeros((SUBLANES, TQ), jnp.int32))
    thr = jnp.maximum(prefix ^ INT_MIN, INT_MIN + 1)
    thr_row = thr[0:1, :]

    n_ge = count(lambda kch: kch >= thr[None])

    @pl.when(jnp.max(n_ge) > topk)
    def _():
        need = topk - count(lambda kch: kch > thr[None])[0:1, :]
        before = jnp.where(lax.broadcasted_iota(jnp.int32, (CH_A, CH_A), 1)
                           < lax.broadcasted_iota(jnp.int32, (CH_A, CH_A), 0), 1.0, 0.0).astype(BF16)

        def tie_body(c, seen):
            k0 = pl.multiple_of(c * CH_A, CH_A)
            kch = keys_ref[pl.ds(k0, CH_A), :]
            eq = jnp.where(kch == thr_row, 1.0, 0.0)
            rank = _dot(before, eq.astype(BF16)) + seen
            tie_rank = jnp.where(kch == thr_row, rank, -1.0)
            keys_ref[pl.ds(k0, CH_A), :] = jnp.where(tie_rank >= need, INT_MIN, kch)
            return seen + jnp.sum(eq, axis=0, keepdims=True)

        lax.fori_loop(0, n_a, tie_body, jnp.zeros((1, TQ), F32))

    m_ref[...] = jnp.full(m_ref.shape, NEG_BIG, F32)
    acc_ref[...] = jnp.zeros(acc_ref.shape, F32)

    def attn_body(c, _):
        k0 = pl.multiple_of(c * CH_A, CH_A)
        kch = kk_ref[pl.ds(k0, CH_A), :]
        vch = vt_ref[:, pl.ds(k0, CH_A)]
        sel = keys_ref[pl.ds(k0, CH_A), :] >= thr_row
        logits = [_dot_nt(kch, qd_st[p * 2 * TQ:(p + 1) * 2 * TQ, :]) for p in range(N_PAIRS)]
        for p in range(N_PAIRS):
            cols = slice(p * 2 * TQ, (p + 1) * 2 * TQ)
            probs, alphas = [], []
            for e in range(2):
                lg = jnp.where(sel, logits[p][:, e * TQ:(e + 1) * TQ], NEG_BIG)
                m_old = m_ref[:, (2 * p + e) * TQ:(2 * p + e + 1) * TQ]
                m_new = jnp.maximum(m_old, jnp.max(lg, axis=0, keepdims=True))
                m_ref[:, (2 * p + e) * TQ:(2 * p + e + 1) * TQ] = m_new
                alphas.append(jnp.exp2(m_old - m_new))
                probs.append(jnp.exp2(lg - m_new).astype(BF16))
            upd = _dot(vch, jnp.concatenate(probs, axis=1))
            acc_ref[:, cols] = acc_ref[:, cols] * jnp.concatenate(alphas, axis=1) + upd
        return 0

    lax.fori_loop(0, n_a, attn_body, 0)
    acc = acc_ref[...]
    out_t = acc[:HEAD_DIM] / acc[HEAD_DIM:HEAD_DIM + 1]
    for p in range(N_PAIRS):
        pair = jnp.concatenate([out_t[:, (2 * p) * TQ:(2 * p + 1) * TQ],
                                out_t[:, (2 * p + 1) * TQ:(2 * p + 2) * TQ]], axis=0)
        o_ref[:, p * LANES:(p + 1) * LANES] = pair.T.astype(o_ref.dtype)


def _dsa_call(qd, qi, wit, kk, vt, kki, batch, seq):
    r3 = lambda t: t.reshape(batch, seq, t.shape[-1])
    qd, qi, kk, kki = map(r3, (qd, qi, kk, kki))
    topk = min(TOPK_MAX, seq // 4)
    blk = lambda w: pl.BlockSpec((None, TQ, w), lambda b, i: (b, i, 0))
    allk = pl.BlockSpec((None, seq, LANES), lambda b, i: (b, 0, 0))
    return pl.pallas_call(
        functools.partial(_dsa_kernel, topk=topk),
        out_shape=jax.ShapeDtypeStruct((batch, seq, W_HEADS), BF16),
        grid=(batch, seq // TQ),
        in_specs=[blk(W_HEADS), blk(W_HEADS),
                  pl.BlockSpec((None, N_HEADS, TQ), lambda b, i: (b, 0, i)),
                  allk, pl.BlockSpec((None, VT_ROWS, seq), lambda b, i: (b, 0, 0)), allk],
        out_specs=blk(W_HEADS),
        scratch_shapes=[pltpu.VMEM((N_HEADS * TQ, LANES), BF16),
                        pltpu.VMEM((N_HEADS * TQ, LANES), BF16),
                        pltpu.VMEM((seq, TQ), jnp.int32),
                        pltpu.VMEM((1, N_HEADS * TQ), F32),
                        pltpu.VMEM((VT_ROWS, N_HEADS * TQ), F32)],
        compiler_params=_cparams(2),
        name="dsa",
    )(qd, qi, wit, kk, vt, kki)


def _mix_kernel(h_ref, n_ref, ysb_ref, yd_ref, mod_ref, wps_ref, wpd_ref, wg_ref, bg_ref,
                wo_ref, o_ref):
    d = h_ref.shape[1]
    y_sb = _dot(ysb_ref[...], wps_ref[...])
    y_d = _dot(yd_ref[...], wpd_ref[...])
    gates = jax.nn.sigmoid(_dot(n_ref[...], wg_ref[...]) + bg_ref[...])
    merged = (gates[:, :d] * y_sb + gates[:, d:] * y_d).astype(BF16)
    o_ref[...] = h_ref[...] + mod_ref[5] * _dot(merged, wo_ref[...])


def _mix_call(h, n2, y_sb, y_d, mod, w_proj_sb, w_proj_dsa, w_gate, b_gate, w_out, *, seq, tm=512):
    n_tok, d = h.shape
    tok = lambda w: pl.BlockSpec((tm, w), lambda i: (i, 0))
    ws = [w.astype(BF16) for w in (w_proj_sb, w_proj_dsa, w_gate)]
    bg = b_gate.reshape(1, -1)
    wo = w_out.astype(BF16)
    return pl.pallas_call(
        _mix_kernel,
        out_shape=jax.ShapeDtypeStruct((n_tok, d), F32),
        grid=(n_tok // tm,),
        in_specs=[tok(d), tok(d), tok(W_HEADS), tok(W_HEADS), _mod_spec(seq // tm, d)]
                 + [_const_spec(w.shape) for w in (*ws, bg, wo)],
        out_specs=tok(d),
        compiler_params=_cparams(1),
        name="mix",
    )(h, n2, y_sb, y_d, mod, *ws, bg, wo)


def kernel(x, c, w_ada, b_ada, g_ffn1, w1_ffn1, w3_ffn1, w2_ffn1, g_mix, w_in, g_q_dsa, g_k_dsa,
           w_proj_sb, w_proj_dsa, w_gate, b_gate, w_out, g_ffn2, w1_ffn2, w3_ffn2, w2_ffn2):
    batch, seq, d = x.shape
    h = x.reshape(batch * seq, d)
    for l in range(w_ada.shape[0]):
        bf = lambda w: w[l].astype(BF16)
        mod = _mod_call(c, w_ada[l], b_ada[l])
        h, n2 = _ffn_call(h, mod, g_ffn1[l], bf(w1_ffn1), bf(w3_ffn1), bf(w2_ffn1), g_mix[l],
                          mod_base=0, emit_next=True, seq=seq)
        q_sb, k_sb, v_sb, q_d, kk, q_i, kki, vt, wit = _inproj_call(
            n2, w_in[l], g_q_dsa[l], g_k_dsa[l], batch, seq)
        y_sb = _sb_call(q_sb, k_sb, v_sb, batch, seq).reshape(batch * seq, W_HEADS)
        y_d = _dsa_call(q_d, q_i, wit, kk, vt, kki, batch, seq).reshape(batch * seq, W_HEADS)
        h = _mix_call(h, n2, y_sb, y_d, mod, w_proj_sb[l], w_proj_dsa[l], w_gate[l], b_gate[l],
                      w_out[l], seq=seq)
        (h,) = _ffn_call(h, mod, g_ffn2[l], bf(w1_ffn2), bf(w3_ffn2), bf(w2_ffn2), g_ffn2[l],
                         mod_base=6, emit_next=False, seq=seq)
    return h.reshape(batch, seq, d)
```

```python
import functools

import numpy as np
import jax
import jax.numpy as jnp
from jax import lax
from jax.experimental import pallas as pl
from jax.experimental.pallas import tpu as pltpu

F32 = jnp.float32
BF16 = jnp.bfloat16

HEAD_DIM = 64
N_HEADS = 8
W_HEADS = N_HEADS * HEAD_DIM
N_PAIRS = N_HEADS // 2
LANES = 128
SUBLANES = 8
BF16_SUBLANES = 16
BF16_MAX_EXACT_COUNT = 256
TOPK_MAX = 256
N_MOD = 9
RMS_EPS = 1e-6
D_IDX = 64

TQ = 128
TK = 128
CH_S = 512
CH_H = 256
VT_ROWS = 80
LANE_POS_HI, LANE_POS_LO = HEAD_DIM, HEAD_DIM + 1
N_POS_LANES = 2
N_SLOPE_PARTS = 3
LOG2E = float(np.log2(np.e))
NEG_BIG = -1e30
F32_EXP_UNDERFLOW = 104.0
INT_MIN = np.int32(-2**31)
VMEM_LIMIT = 56 * 1024 * 1024


def _cparams(n_axes):
    return pltpu.CompilerParams(
        dimension_semantics=("arbitrary",) * n_axes, vmem_limit_bytes=VMEM_LIMIT)


def _const_spec(shape):
    nd = len(shape)
    return pl.BlockSpec(shape, lambda *_: (0,) * nd, pipeline_mode=pl.Buffered(1))


def _dot(a, b):
    return jnp.dot(a, b, preferred_element_type=F32)


def _dot_nt(a, b):
    return lax.dot_general(a, b, (((1,), (1,)), ((), ())), preferred_element_type=F32)


def _mod_kernel(c_ref, w_ref, b_ref, o_ref):
    nb, kdim, _ = c_ref.shape
    tn = w_ref.shape[1]
    nch = tn // LANES

    def body(kb, accs):
        k0 = pl.multiple_of(kb * SUBLANES, SUBLANES)
        wblk = w_ref[pl.ds(k0, SUBLANES), :]
        out = []
        for b in range(nb):
            cb = c_ref[b, pl.ds(k0, SUBLANES), :]
            sb = cb * jax.nn.sigmoid(cb)
            for ch in range(nch):
                out.append(accs[b * nch + ch] + wblk[:, ch * LANES:(ch + 1) * LANES] * sb)
        return tuple(out)

    accs = lax.fori_loop(0, kdim // SUBLANES, body,
                         tuple(jnp.zeros((SUBLANES, LANES), F32) for _ in range(nb * nch)))
    for b in range(nb):
        row = jnp.concatenate(
            [jnp.sum(accs[b * nch + ch], axis=0, keepdims=True) for ch in range(nch)], axis=1)
        o_ref[0, b] = row + b_ref[...]


def _mod_call(c, w_ada, b_ada):
    nb, d = c.shape
    n_out = w_ada.shape[1]
    c_rep = jnp.broadcast_to(c[:, :, None], (nb, d, LANES))
    return pl.pallas_call(
        _mod_kernel,
        out_shape=jax.ShapeDtypeStruct((n_out // d, nb, 1, d), F32),
        grid=(n_out // d,),
        in_specs=[_const_spec((nb, d, LANES)),
                  pl.BlockSpec((d, d), lambda j: (0, j)),
                  pl.BlockSpec((1, d), lambda j: (0, j))],
        out_specs=pl.BlockSpec((1, nb, 1, d), lambda j: (j, 0, 0, 0)),
        compiler_params=_cparams(1),
        name="mod",
    )(c_rep, w_ada, b_ada.reshape(1, n_out))


def _norm_mod(h, g, shift, scale):
    y = h * lax.rsqrt(jnp.mean(h * h, axis=-1, keepdims=True) + RMS_EPS)
    return (y * g) * (1.0 + scale) + shift


def _ffn_kernel(h_ref, mod_ref, g_ref, w1_ref, w3_ref, w2_ref, gn_ref, *out_refs,
                mod_base, emit_next, n_chunks):
    o_ref = out_refs[0]
    h = h_ref[...]
    n = _norm_mod(h, g_ref[...], mod_ref[mod_base], mod_ref[mod_base + 1]).astype(BF16)
    fc = w1_ref.shape[1] // n_chunks
    acc = jnp.zeros(h.shape, F32)
    for c in range(n_chunks):
        a = _dot(n, w1_ref[:, c * fc:(c + 1) * fc])
        b = _dot(n, w3_ref[:, c * fc:(c + 1) * fc])
        act = (a * jax.nn.sigmoid(a) * b).astype(BF16)
        acc = acc + _dot(act, w2_ref[c * fc:(c + 1) * fc, :])
    hn = h + (0.5 * mod_ref[mod_base + 2]) * acc
    o_ref[...] = hn
    if emit_next:
        out_refs[1][...] = _norm_mod(hn, gn_ref[...], mod_ref[mod_base + 3],
                                     mod_ref[mod_base + 4]).astype(BF16)


def _mod_spec(tiles_per_batch, d):
    return pl.BlockSpec((N_MOD, None, 1, d), lambda i: (0, i // tiles_per_batch, 0, 0))


def _ffn_call(h, mod, g, w1, w3, w2, g_next, *, mod_base, emit_next, seq, tm=512):
    n_tok, d = h.shape
    f = w1.shape[1]
    tok = pl.BlockSpec((tm, d), lambda i: (i, 0))
    out_shape = [jax.ShapeDtypeStruct((n_tok, d), F32)]
    out_specs = [tok]
    if emit_next:
        out_shape.append(jax.ShapeDtypeStruct((n_tok, d), BF16))
        out_specs.append(tok)
    return pl.pallas_call(
        functools.partial(_ffn_kernel, mod_base=mod_base, emit_next=emit_next, n_chunks=2),
        out_shape=out_shape,
        grid=(n_tok // tm,),
        in_specs=[tok, _mod_spec(seq // tm, d), _const_spec((1, d)),
                  _const_spec((d, f)), _const_spec((d, f)), _const_spec((f, d)),
                  _const_spec((1, d))],
        out_specs=out_specs,
        compiler_params=_cparams(1),
        name="ffn",
    )(h, mod, g.reshape(1, d), w1, w3, w2, g_next.reshape(1, d))


def _inproj_kernel(n_ref, wsb_ref, wqd_ref, wkd_ref, wqi_ref, wki_ref, wvt_ref, wwt_ref,
                   gq_ref, gk_ref, hm_ref,
                   qsb_ref, ksb_ref, vsb_ref, qd_ref, kk_ref, qi_ref, kki_ref, vt_ref, wit_ref,
                   *, tiles_per_batch):
    n = n_ref[...]
    tm = n.shape[0]
    qk_scale = HEAD_DIM ** -0.5
    sb = _dot(n, wsb_ref[...])
    qsb_ref[...] = (sb[:, :W_HEADS] * qk_scale).astype(BF16)
    ksb_ref[...] = sb[:, W_HEADS:2 * W_HEADS].astype(BF16)
    vsb_ref[...] = sb[:, 2 * W_HEADS:].astype(BF16)

    qd = _dot(n, wqd_ref[...])
    ms = _dot((qd * qd).astype(BF16), hm_ref[...])
    qd_ref[...] = (qd * lax.rsqrt(ms + RMS_EPS) * (gq_ref[...] * (qk_scale * LOG2E))).astype(BF16)

    kd = _dot(n, wkd_ref[...])
    kn = kd * lax.rsqrt(jnp.mean(kd * kd, axis=-1, keepdims=True) + RMS_EPS) * gk_ref[...]
    lane = lax.broadcasted_iota(jnp.int32, (tm, LANES), 1)
    pos = ((pl.program_id(0) % tiles_per_batch) * tm
           + lax.broadcasted_iota(jnp.int32, (tm, LANES), 0))
    feat = jnp.zeros((tm, LANES), jnp.int32)
    for rep in range(N_SLOPE_PARTS):
        feat = jnp.where(lane == LANE_POS_HI + rep * N_POS_LANES,
                         pos >> (HEAD_DIM.bit_length() - 1), feat)
        feat = jnp.where(lane == LANE_POS_LO + rep * N_POS_LANES, pos & (HEAD_DIM - 1), feat)
    feat = feat.astype(F32)
    kk_ref[...] = jnp.where(lane < HEAD_DIM, kn, feat).astype(BF16)

    qi_ref[...] = _dot(n, wqi_ref[...]).astype(BF16)
    kki_ref[...] = _dot(n, wki_ref[...]).astype(BF16)
    vt = _dot_nt(wvt_ref[...], n)
    row = lax.broadcasted_iota(jnp.int32, vt.shape, 0)
    vt_ref[...] = jnp.where(row == HEAD_DIM, 1.0, vt).astype(BF16)
    wit_ref[...] = _dot_nt(wwt_ref[...], n)[:N_HEADS]


def _inproj_call(n2, w_in, g_q, g_k, batch, seq, tm=512):
    n_tok, d = n2.shape
    sizes = (W_HEADS, W_HEADS, W_HEADS, W_HEADS, HEAD_DIM, HEAD_DIM, N_HEADS * D_IDX, D_IDX, N_HEADS)
    offs = np.concatenate([[0], np.cumsum(sizes)])
    col = lambda k: w_in[:, offs[k]:offs[k + 1]]
    idx_scale = (D_IDX ** -0.5) * (N_HEADS ** -0.5)
    w_sb = jnp.concatenate([col(0), col(1), col(2)], axis=1).astype(BF16)
    w_qd = col(3).astype(BF16)
    w_kd = jnp.concatenate([col(4), col(4)], axis=1).astype(BF16)
    w_qi = col(6).astype(BF16)
    w_ki = jnp.pad(col(7), ((0, 0), (0, LANES - D_IDX))).astype(BF16)
    w_vt = jnp.pad(col(5).T, ((0, VT_ROWS - HEAD_DIM), (0, 0))).astype(BF16)
    w_wt = jnp.pad((col(8) * idx_scale).T, ((0, 16 - N_HEADS), (0, 0))).astype(BF16)
    gq = jnp.tile(g_q, N_HEADS).reshape(1, W_HEADS)
    gk = jnp.tile(g_k, 2).reshape(1, LANES)
    head_of = np.arange(W_HEADS) // HEAD_DIM
    head_mean = jnp.asarray((head_of[:, None] == head_of[None, :]) / HEAD_DIM, BF16)

    tpb = seq // tm
    tok = lambda w: pl.BlockSpec((tm, w), lambda i: (i, 0))
    tr = lambda r: pl.BlockSpec((None, r, tm), lambda i: (i // tpb, 0, i % tpb))
    sd = lambda w, dt: jax.ShapeDtypeStruct((n_tok, w), dt)
    consts = (w_sb, w_qd, w_kd, w_qi, w_ki, w_vt, w_wt, gq, gk, head_mean)
    return pl.pallas_call(
        functools.partial(_inproj_kernel, tiles_per_batch=tpb),
        out_shape=[sd(W_HEADS, BF16)] * 4 + [sd(LANES, BF16), sd(W_HEADS, BF16), sd(LANES, BF16),
                                             jax.ShapeDtypeStruct((batch, VT_ROWS, seq), BF16),
                                             jax.ShapeDtypeStruct((batch, N_HEADS, seq), F32)],
        grid=(n_tok // tm,),
        in_specs=[tok(d)] + [_const_spec(w.shape) for w in consts],
        out_specs=[tok(W_HEADS)] * 4 + [tok(LANES), tok(W_HEADS), tok(LANES),
                                        tr(VT_ROWS), tr(N_HEADS)],
        compiler_params=_cparams(1),
        name="inproj",
    )(n2, *consts)


def _split_heads(x, lane):
    xf = x.astype(F32)
    lo = jnp.where(lane < HEAD_DIM, xf, 0.0).astype(x.dtype)
    hi = jnp.where(lane >= HEAD_DIM, xf, 0.0).astype(x.dtype)
    return lo, hi


def _sb_kernel(q_ref, k_ref, v_ref, t_ref, o_ref, qs_ref, run_ref, acc_ref):
    i = pl.program_id(1)
    lane = lax.broadcasted_iota(jnp.int32, (TQ, LANES), 1)
    lane_k = lax.broadcasted_iota(jnp.int32, (TK, LANES), 1)
    for p in range(N_PAIRS):
        lo, hi = _split_heads(q_ref[:, p * LANES:(p + 1) * LANES], lane)
        qs_ref[p, :TQ, :] = lo
        qs_ref[p, TQ:, :] = hi

    def block(k0, diag):
        pairs = range(N_PAIRS)
        strict = (lax.broadcasted_iota(jnp.int32, (2 * TQ, TK), 1)
                  < lax.broadcasted_iota(jnp.int32, (2 * TQ, TK), 0) % TQ) if diag else None
        zs = [_dot_nt(qs_ref[p], k_ref[pl.ds(k0, TK), p * LANES:(p + 1) * LANES]) for p in pairs]
        lbs, ws = [], []
        for p in pairs:
            z = zs[p]
            neg_abs = pltpu.bitcast(pltpu.bitcast(z, jnp.int32) | INT_MIN, F32)
            sp = jnp.maximum(z, 0.0) + jnp.log(1.0 + jnp.exp(neg_abs))
            l1m = jnp.where(strict, sp, 0.0) if diag else sp
            l_hi = l1m.astype(BF16)
            l_lo = (l1m - l_hi.astype(F32)).astype(BF16)
            ws.append(_dot(jnp.concatenate([l_hi, l_lo], axis=1), t_ref[...]))
            lbs.append(z - sp)
        for p in pairs:
            if diag:
                a = jnp.where(strict, jnp.exp(lbs[p] + ws[p][:, :TK]), 0.0).astype(BF16)
            else:
                a = jnp.exp(lbs[p] + ws[p][:, :TK] + run_ref[p]).astype(BF16)
            v_lo, v_hi = _split_heads(v_ref[pl.ds(k0, TK), p * LANES:(p + 1) * LANES], lane_k)
            upd = _dot(jnp.concatenate([a[:TQ], a[TQ:]], axis=1),
                       jnp.concatenate([v_lo, v_hi], axis=0))
            if diag:
                acc_ref[p] = upd
                run_ref[p] = ws[p][:, TK:]
            else:
                acc_ref[p] += upd
                run_ref[p] += ws[p][:, TK:]

    def more_blocks(jj):
        run_max = functools.reduce(jnp.maximum, [jnp.max(run_ref[p]) for p in range(N_PAIRS)])
        return jnp.logical_and(jj <= i, run_max > -F32_EXP_UNDERFLOW)

    block(pl.multiple_of(i * TK, TK), True)

    def body(state):
        jj, _ = state
        block(pl.multiple_of((i - jj) * TK, TK), False)
        return jj + 1, more_blocks(jj + 1)

    lax.while_loop(lambda state: state[1], body, (jnp.int32(1), more_blocks(jnp.int32(1))))
    for p in range(N_PAIRS):
        o_ref[:, p * LANES:(p + 1) * LANES] = acc_ref[p].astype(o_ref.dtype)


def _sb_call(q, k, v, batch, seq):
    q, k, v = (t.reshape(batch, seq, W_HEADS) for t in (q, k, v))
    j_idx = np.arange(TK)
    suffix = (j_idx[:, None] > j_idx[None, :]).astype(np.float32)
    tmat = -np.concatenate([suffix, np.ones((TK, LANES), np.float32)], axis=1)
    tmat = jnp.asarray(np.concatenate([tmat, tmat], axis=0), BF16)
    blk = pl.BlockSpec((None, TQ, W_HEADS), lambda b, i: (b, i, 0))
    allk = pl.BlockSpec((None, seq, W_HEADS), lambda b, i: (b, 0, 0))
    return pl.pallas_call(
        _sb_kernel,
        out_shape=jax.ShapeDtypeStruct((batch, seq, W_HEADS), BF16),
        grid=(batch, seq // TQ),
        in_specs=[blk, allk, allk, _const_spec(tmat.shape)],
        out_specs=blk,
        scratch_shapes=[pltpu.VMEM((N_PAIRS, 2 * TQ, LANES), BF16),
                        pltpu.VMEM((N_PAIRS, 2 * TQ, LANES), F32),
                        pltpu.VMEM((N_PAIRS, TQ, LANES), F32)],
        compiler_params=_cparams(2),
        name="sb",
    )(q, k, v, tmat)


def _dsa_kernel(qd_ref, qi_ref, wit_ref, kk_ref, vt_ref, kki_ref, o_ref,
                qd_st, qi_st, keys_ref, top_ref, m_ref, acc_ref, buf_ref, *, topk):
    i = pl.program_id(1)
    t0 = i * TQ
    lane = lax.broadcasted_iota(jnp.int32, (TQ, LANES), 1)
    slope_parts = []
    for h in range(N_HEADS):
        rest, parts = np.float32(2.0 ** (-8.0 * (h + 1) / N_HEADS) * LOG2E), []
        for _ in range(N_SLOPE_PARTS):
            parts.append(np.float32(np.asarray(rest, dtype=BF16)))
            rest = np.float32(rest - parts[-1])
        slope_parts.append(parts)
    for h in range(N_HEADS):
        p = h // 2
        for src, dst, alibi in ((qd_ref, qd_st, True), (qi_ref, qi_st, False)):
            x = src[:, p * LANES:(p + 1) * LANES].astype(F32)
            if h % 2:
                x = pltpu.roll(x, HEAD_DIM, axis=1)
            x = jnp.where(lane < HEAD_DIM, x, 0.0)
            if alibi:
                for rep, part in enumerate(slope_parts[h]):
                    x = jnp.where(lane == LANE_POS_HI + rep * N_POS_LANES, part * HEAD_DIM, x)
                    x = jnp.where(lane == LANE_POS_LO + rep * N_POS_LANES, part, x)
            dst[h * TQ:(h + 1) * TQ, :] = x.astype(BF16)

    n_s = (i + CH_S // TQ) // (CH_S // TQ)
    n_h = n_s * (CH_S // CH_H)

    def products(q_st, k_ref):
        def produce(k0):
            kch = k_ref[pl.ds(k0, CH_H), :]
            return [_dot_nt(kch, q_st[p * 2 * TQ:(p + 1) * 2 * TQ, :]) for p in range(N_PAIRS)]
        return produce

    def store_products(vals):
        for p in range(N_PAIRS):
            buf_ref[:, p * 2 * TQ:(p + 1) * 2 * TQ] = vals[p]

    def sweep(produce, consume):
        from_buf = lambda h: buf_ref[:, h * TQ:(h + 1) * TQ]

        def trip(c, prefetch):
            k_a = pl.multiple_of(c * CH_S, CH_S)
            k_b = pl.multiple_of(k_a + CH_H, CH_H)
            vals_b = produce(k_b)
            consume(from_buf, k_a)
            if prefetch:
                vals_next = produce(pl.multiple_of(k_a + CH_S, CH_S))
            consume(lambda h: vals_b[h // 2][:, (h % 2) * TQ:(h % 2 + 1) * TQ], k_b)
            if prefetch:
                store_products(vals_next)

        store_products(produce(0))

        def body(c, _):
            trip(c, True)
            return 0

        lax.fori_loop(0, n_s - 1, body, 0)
        trip(n_s - 1, False)

    wit = wit_ref[...]

    def score_consume(prod, k0):
        score = None
        for h in range(N_HEADS):
            term = jnp.maximum(prod(h), 0.0) * wit[h:h + 1, :]
            score = term if score is None else score + term
        bits = pltpu.bitcast(score, jnp.int32)
        bits = jnp.where(bits == INT_MIN, 0, bits)
        key = jnp.where(bits < 0, bits ^ np.int32(0x7FFFFFFF), bits)
        causal = (k0 + lax.broadcasted_iota(jnp.int32, (CH_H, TQ), 0)
                  <= t0 + lax.broadcasted_iota(jnp.int32, (CH_H, TQ), 1))
        keys_ref[pl.ds(k0, CH_H), :] = jnp.where(causal, key, INT_MIN)
        top = pltpu.bitcast(bits & np.int32(-(1 << 16)), F32)
        top_ref[pl.ds(k0, CH_H), :] = jnp.where(causal, top, jnp.nan).astype(BF16)

    sweep(products(qi_st, kki_ref), score_consume)

    def top_float(u16):
        pattern = jnp.where(u16 >= 0x8000, u16 & 0x7FFF, 0x8000 | (u16 ^ 0x7FFF))
        val = pltpu.bitcast(jnp.left_shift(pattern, 16), F32).astype(BF16)
        return jnp.broadcast_to(val, (BF16_SUBLANES, TQ))

    def count_top_ge(cand):
        one, zero = jnp.ones((BF16_SUBLANES, TQ), BF16), jnp.zeros((BF16_SUBLANES, TQ), BF16)

        def cbody(c, acc):
            k0 = pl.multiple_of(c * CH_S, CH_S)
            tops = top_ref[pl.ds(k0, CH_S), :]
            parts = [jnp.where(tops[j * BF16_SUBLANES:(j + 1) * BF16_SUBLANES] >= cand, one, zero)
                     for j in range(CH_S // BF16_SUBLANES)]
            while len(parts) > 1:
                parts = [a + b for a, b in zip(parts[::2], parts[1::2])]
            return acc + parts[0]

        acc = lax.fori_loop(0, n_s, cbody, zero)
        return jnp.sum(acc.astype(F32), axis=0, keepdims=True)

    def top_body(b, u16):
        cand = u16 | jnp.left_shift(jnp.int32(1), 15 - b)
        return jnp.where(count_top_ge(top_float(cand)) >= topk, cand, u16)

    u16 = lax.fori_loop(0, 16, top_body, jnp.zeros((1, TQ), jnp.int32))

    def count(pred):
        def cbody(c, acc):
            k0 = pl.multiple_of(c * CH_S, CH_S)
            kch = keys_ref[pl.ds(k0, CH_S), :].reshape(CH_S // SUBLANES, SUBLANES, TQ)
            return acc + jnp.sum(jnp.where(pred(kch), 1, 0), axis=0)
        acc = lax.fori_loop(0, n_s, cbody, jnp.zeros((SUBLANES, TQ), jnp.int32))
        tot = jnp.sum(acc.astype(F32), axis=0, keepdims=True)
        return jnp.broadcast_to(tot, (SUBLANES, TQ))

    def bit_body(b, prefix):
        cand = prefix | jnp.left_shift(jnp.int32(1), 31 - b)
        cand_s = cand ^ INT_MIN
        return jnp.where(count(lambda kch: kch >= cand_s[None]) >= topk, cand, prefix)

    prefix = lax.fori_loop(16, 32, bit_body,
                           jnp.broadcast_to(jnp.left_shift(u16, 16), (SUBLANES, TQ)))
    thr = jnp.maximum(prefix ^ INT_MIN, INT_MIN + 1)
    thr_row = thr[0:1, :]

    n_ge = count(lambda kch: kch >= thr[None])

    @pl.when(jnp.max(n_ge) > topk)
    def _():
        need = topk - count(lambda kch: kch > thr[None])[0:1, :]
        before = jnp.where(lax.broadcasted_iota(jnp.int32, (CH_H, CH_H), 1)
                           < lax.broadcasted_iota(jnp.int32, (CH_H, CH_H), 0), 1.0, 0.0).astype(BF16)

        def tie_body(c, seen):
            k0 = pl.multiple_of(c * CH_H, CH_H)
            kch = keys_ref[pl.ds(k0, CH_H), :]
            eq = jnp.where(kch == thr_row, 1.0, 0.0)
            rank = _dot(before, eq.astype(BF16)) + seen
            tie_rank = jnp.where(kch == thr_row, rank, -1.0)
            keys_ref[pl.ds(k0, CH_H), :] = jnp.where(tie_rank >= need, INT_MIN, kch)
            return seen + jnp.sum(eq, axis=0, keepdims=True)

        lax.fori_loop(0, n_h, tie_body, jnp.zeros((1, TQ), F32))

    m_ref[...] = jnp.full(m_ref.shape, NEG_BIG, F32)
    acc_ref[...] = jnp.zeros(acc_ref.shape, F32)

    def attn_consume(prod, k0):
        vch = vt_ref[:, pl.ds(k0, CH_H)]
        sel = keys_ref[pl.ds(k0, CH_H), :] >= thr_row
        for p in range(N_PAIRS):
            cols = slice(p * 2 * TQ, (p + 1) * 2 * TQ)
            probs, alphas = [], []
            for h in (2 * p, 2 * p + 1):
                lg = jnp.where(sel, prod(h), NEG_BIG)
                m_old = m_ref[:, h * TQ:(h + 1) * TQ]
                m_new = jnp.maximum(m_old, jnp.max(lg, axis=0, keepdims=True))
                m_ref[:, h * TQ:(h + 1) * TQ] = m_new
                alphas.append(jnp.exp2(m_old - m_new))
                probs.append(jnp.exp2(lg - m_new).astype(BF16))
            upd = _dot(vch, jnp.concatenate(probs, axis=1))
            acc_ref[:, cols] = acc_ref[:, cols] * jnp.concatenate(alphas, axis=1) + upd

    sweep(products(qd_st, kk_ref), attn_consume)
    acc = acc_ref[...]
    out_t = acc[:HEAD_DIM] / acc[HEAD_DIM:HEAD_DIM + 1]
    for p in range(N_PAIRS):
        pair = jnp.concatenate([out_t[:, (2 * p) * TQ:(2 * p + 1) * TQ],
                                out_t[:, (2 * p + 1) * TQ:(2 * p + 2) * TQ]], axis=0)
        o_ref[:, p * LANES:(p + 1) * LANES] = pair.T.astype(o_ref.dtype)


def _dsa_call(qd, qi, wit, kk, vt, kki, batch, seq):
    r3 = lambda t: t.reshape(batch, seq, t.shape[-1])
    qd, qi, kk, kki = map(r3, (qd, qi, kk, kki))
    topk = min(TOPK_MAX, seq // 4)
    assert seq // BF16_SUBLANES <= BF16_MAX_EXACT_COUNT
    blk = lambda w: pl.BlockSpec((None, TQ, w), lambda b, i: (b, i, 0))
    allk = pl.BlockSpec((None, seq, LANES), lambda b, i: (b, 0, 0))
    return pl.pallas_call(
        functools.partial(_dsa_kernel, topk=topk),
        out_shape=jax.ShapeDtypeStruct((batch, seq, W_HEADS), BF16),
        grid=(batch, seq // TQ),
        in_specs=[blk(W_HEADS), blk(W_HEADS),
                  pl.BlockSpec((None, N_HEADS, TQ), lambda b, i: (b, 0, i)),
                  allk, pl.BlockSpec((None, VT_ROWS, seq), lambda b, i: (b, 0, 0)), allk],
        out_specs=blk(W_HEADS),
        scratch_shapes=[pltpu.VMEM((N_HEADS * TQ, LANES), BF16),
                        pltpu.VMEM((N_HEADS * TQ, LANES), BF16),
                        pltpu.VMEM((seq, TQ), jnp.int32),
                        pltpu.VMEM((seq, TQ), BF16),
                        pltpu.VMEM((1, N_HEADS * TQ), F32),
                        pltpu.VMEM((VT_ROWS, N_HEADS * TQ), F32),
                        pltpu.VMEM((CH_H, N_HEADS * TQ), F32)],
        compiler_params=_cparams(2),
        name="dsa",
    )(qd, qi, wit, kk, vt, kki)


def _mix_kernel(h_ref, n_ref, ysb_ref, yd_ref, mod_ref, wps_ref, wpd_ref, wg_ref, bg_ref,
                wo_ref, o_ref):
    d = h_ref.shape[1]
    y_sb = _dot(ysb_ref[...], wps_ref[...])
    y_d = _dot(yd_ref[...], wpd_ref[...])
    gates = jax.nn.sigmoid(_dot(n_ref[...], wg_ref[...]) + bg_ref[...])
    merged = (gates[:, :d] * y_sb + gates[:, d:] * y_d).astype(BF16)
    o_ref[...] = h_ref[...] + mod_ref[5] * _dot(merged, wo_ref[...])


def _mix_call(h, n2, y_sb, y_d, mod, w_proj_sb, w_proj_dsa, w_gate, b_gate, w_out, *, seq, tm=512):
    n_tok, d = h.shape
    tok = lambda w: pl.BlockSpec((tm, w), lambda i: (i, 0))
    ws = [w.astype(BF16) for w in (w_proj_sb, w_proj_dsa, w_gate)]
    bg = b_gate.reshape(1, -1)
    wo = w_out.astype(BF16)
    return pl.pallas_call(
        _mix_kernel,
        out_shape=jax.ShapeDtypeStruct((n_tok, d), F32),
        grid=(n_tok // tm,),
        in_specs=[tok(d), tok(d), tok(W_HEADS), tok(W_HEADS), _mod_spec(seq // tm, d)]
                 + [_const_spec(w.shape) for w in (*ws, bg, wo)],
        out_specs=tok(d),
        compiler_params=_cparams(1),
        name="mix",
    )(h, n2, y_sb, y_d, mod, *ws, bg, wo)


def kernel(x, c, w_ada, b_ada, g_ffn1, w1_ffn1, w3_ffn1, w2_ffn1, g_mix, w_in, g_q_dsa, g_k_dsa,
           w_proj_sb, w_proj_dsa, w_gate, b_gate, w_out, g_ffn2, w1_ffn2, w3_ffn2, w2_ffn2):
    batch, seq, d = x.shape
    h = x.reshape(batch * seq, d)
    for l in range(w_ada.shape[0]):
        bf = lambda w: w[l].astype(BF16)
        mod = _mod_call(c, w_ada[l], b_ada[l])
        h, n2 = _ffn_call(h, mod, g_ffn1[l], bf(w1_ffn1), bf(w3_ffn1), bf(w2_ffn1), g_mix[l],
                          mod_base=0, emit_next=True, seq=seq)
        q_sb, k_sb, v_sb, q_d, kk, q_i, kki, vt, wit = _inproj_call(
            n2, w_in[l], g_q_dsa[l], g_k_dsa[l], batch, seq)
        y_sb = _sb_call(q_sb, k_sb, v_sb, batch, seq).reshape(batch * seq, W_HEADS)
        y_d = _dsa_call(q_d, q_i, wit, kk, vt, kki, batch, seq).reshape(batch * seq, W_HEADS)
        h = _mix_call(h, n2, y_sb, y_d, mod, w_proj_sb[l], w_proj_dsa[l], w_gate[l], b_gate[l],
                      w_out[l], seq=seq)
        (h,) = _ffn_call(h, mod, g_ffn2[l], bf(w1_ffn2), bf(w3_ffn2), bf(w2_ffn2), g_ffn2[l],
                         mod_base=6, emit_next=False, seq=seq)
    return h.reshape(batch, seq, d)
```

```python
import functools

import numpy as np
import jax
import jax.numpy as jnp
from jax import lax
from jax.experimental import pallas as pl
from jax.experimental.pallas import tpu as pltpu

F32 = jnp.float32
BF16 = jnp.bfloat16

HEAD_DIM = 64
N_HEADS = 8
W_HEADS = N_HEADS * HEAD_DIM
N_PAIRS = N_HEADS // 2
LANES = 128
SUBLANES = 8
BF16_SUBLANES = 16
BF16_MAX_EXACT_COUNT = 256
SELECT_WINDOW_BITS = 17
TOPK_MAX = 256
N_MOD = 9
RMS_EPS = 1e-6
D_IDX = 64

TQ = 128
TQ_DSA = 128
TK = 128
CH_S = 512
CH_H = 256
VT_ROWS = 80
LANE_POS_HI, LANE_POS_LO = HEAD_DIM, HEAD_DIM + 1
N_POS_LANES = 2
N_SLOPE_PARTS = 3
LOG2E = float(np.log2(np.e))
NEG_BIG = -1e30
F32_EXP_UNDERFLOW = 104.0
INT_MIN = np.int32(-2**31)
VMEM_LIMIT = 56 * 1024 * 1024


def _cparams(n_axes):
    return pltpu.CompilerParams(
        dimension_semantics=("arbitrary",) * n_axes, vmem_limit_bytes=VMEM_LIMIT)


def _const_spec(shape):
    nd = len(shape)
    return pl.BlockSpec(shape, lambda *_: (0,) * nd, pipeline_mode=pl.Buffered(1))


def _dot(a, b):
    return jnp.dot(a, b, preferred_element_type=F32)


def _dot_nt(a, b):
    return lax.dot_general(a, b, (((1,), (1,)), ((), ())), preferred_element_type=F32)


def _mod_kernel(c_ref, w_ref, b_ref, o_ref, sc_ref):
    nb, kdim, _ = c_ref.shape
    tn = w_ref.shape[1]
    nch = tn // LANES

    @pl.when(pl.program_id(0) == 0)
    def _():
        c = c_ref[...]
        sc_ref[...] = c * jax.nn.sigmoid(c)

    def body(kb, accs):
        k0 = pl.multiple_of(kb * SUBLANES, SUBLANES)
        wblk = w_ref[pl.ds(k0, SUBLANES), :]
        out = []
        for b in range(nb):
            sb = sc_ref[b, pl.ds(k0, SUBLANES), :]
            for ch in range(nch):
                out.append(accs[b * nch + ch] + wblk[:, ch * LANES:(ch + 1) * LANES] * sb)
        return tuple(out)

    accs = lax.fori_loop(0, kdim // SUBLANES, body,
                         tuple(jnp.zeros((SUBLANES, LANES), F32) for _ in range(nb * nch)))
    for b in range(nb):
        row = jnp.concatenate(
            [jnp.sum(accs[b * nch + ch], axis=0, keepdims=True) for ch in range(nch)], axis=1)
        o_ref[0, b] = row + b_ref[...]


def _mod_call(c, w_ada, b_ada):
    nb, d = c.shape
    n_out = w_ada.shape[1]
    c_rep = jnp.broadcast_to(c[:, :, None], (nb, d, LANES))
    return pl.pallas_call(
        _mod_kernel,
        out_shape=jax.ShapeDtypeStruct((n_out // d, nb, 1, d), F32),
        grid=(n_out // d,),
        in_specs=[_const_spec((nb, d, LANES)),
                  pl.BlockSpec((d, d), lambda j: (0, j)),
                  pl.BlockSpec((1, d), lambda j: (0, j))],
        out_specs=pl.BlockSpec((1, nb, 1, d), lambda j: (j, 0, 0, 0)),
        scratch_shapes=[pltpu.VMEM((nb, d, LANES), F32)],
        compiler_params=_cparams(1),
        name="mod",
    )(c_rep, w_ada, b_ada.reshape(1, n_out))


def _norm_mod(h, g, shift, scale):
    y = h * lax.rsqrt(jnp.mean(h * h, axis=-1, keepdims=True) + RMS_EPS)
    return (y * g) * (1.0 + scale) + shift


def _ffn_kernel(h_ref, mod_ref, g_ref, w1_ref, w3_ref, w2_ref, gn_ref, *out_refs,
                mod_base, emit_next, n_chunks):
    o_ref = out_refs[0]
    h = h_ref[...]
    n = _norm_mod(h, g_ref[...], mod_ref[mod_base], mod_ref[mod_base + 1]).astype(BF16)
    fc = w1_ref.shape[1] // n_chunks
    acc = jnp.zeros(h.shape, F32)
    for c in range(n_chunks):
        a = _dot(n, w1_ref[:, c * fc:(c + 1) * fc])
        b = _dot(n, w3_ref[:, c * fc:(c + 1) * fc])
        act = (a * jax.nn.sigmoid(a) * b).astype(BF16)
        acc = acc + _dot(act, w2_ref[c * fc:(c + 1) * fc, :])
    hn = h + (0.5 * mod_ref[mod_base + 2]) * acc
    o_ref[...] = hn
    if emit_next:
        out_refs[1][...] = _norm_mod(hn, gn_ref[...], mod_ref[mod_base + 3],
                                     mod_ref[mod_base + 4]).astype(BF16)


def _mod_spec(tiles_per_batch, d):
    return pl.BlockSpec((N_MOD, None, 1, d), lambda i: (0, i // tiles_per_batch, 0, 0))


def _ffn_call(h, mod, g, w1, w3, w2, g_next, *, mod_base, emit_next, seq, tm=512):
    n_tok, d = h.shape
    f = w1.shape[1]
    tok = pl.BlockSpec((tm, d), lambda i: (i, 0))
    out_shape = [jax.ShapeDtypeStruct((n_tok, d), F32)]
    out_specs = [tok]
    if emit_next:
        out_shape.append(jax.ShapeDtypeStruct((n_tok, d), BF16))
        out_specs.append(tok)
    return pl.pallas_call(
        functools.partial(_ffn_kernel, mod_base=mod_base, emit_next=emit_next, n_chunks=2),
        out_shape=out_shape,
        grid=(n_tok // tm,),
        in_specs=[tok, _mod_spec(seq // tm, d), _const_spec((1, d)),
                  _const_spec((d, f)), _const_spec((d, f)), _const_spec((f, d)),
                  _const_spec((1, d))],
        out_specs=out_specs,
        compiler_params=_cparams(1),
        name="ffn",
    )(h, mod, g.reshape(1, d), w1, w3, w2, g_next.reshape(1, d))


def _alibi_slope_parts():
    out = []
    for h in range(N_HEADS):
        rest, parts = np.float32(2.0 ** (-8.0 * (h + 1) / N_HEADS) * LOG2E), []
        for _ in range(N_SLOPE_PARTS):
            parts.append(np.float32(np.asarray(rest, dtype=BF16)))
            rest = np.float32(rest - parts[-1])
        out.append(parts)
    return out


def _store_head_rows(o_ref, x, slope_parts):
    tm = x.shape[0]
    lane = lax.broadcasted_iota(jnp.int32, (tm, LANES), 1)
    for h in range(N_HEADS):
        rows = x[:, (h // 2) * LANES:(h // 2 + 1) * LANES]
        if h % 2:
            rows = pltpu.roll(rows, HEAD_DIM, axis=1)
        rows = jnp.where(lane < HEAD_DIM, rows, 0.0)
        if slope_parts is not None:
            for rep, part in enumerate(slope_parts[h]):
                rows = jnp.where(lane == LANE_POS_HI + rep * N_POS_LANES, part * HEAD_DIM, rows)
                rows = jnp.where(lane == LANE_POS_LO + rep * N_POS_LANES, part, rows)
        o_ref[:, h] = rows.astype(BF16).reshape(tm // TQ_DSA, TQ_DSA, LANES)


def _inproj_kernel(n_ref, wsb_ref, wqd_ref, wkd_ref, wqi_ref, wki_ref, wvt_ref, wwt_ref,
                   gq_ref, gk_ref, hm_ref,
                   qsb_ref, ksb_ref, vsb_ref, qd_ref, kk_ref, qi_ref, kki_ref, vt_ref, wit_ref,
                   *, tiles_per_batch):
    n = n_ref[...]
    tm = n.shape[0]
    qk_scale = HEAD_DIM ** -0.5
    sb = _dot(n, wsb_ref[...])
    qsb_ref[...] = (sb[:, :W_HEADS] * qk_scale).astype(BF16)
    ksb_ref[...] = sb[:, W_HEADS:2 * W_HEADS].astype(BF16)
    vsb_ref[...] = sb[:, 2 * W_HEADS:].astype(BF16)

    qd = _dot(n, wqd_ref[...])
    ms = _dot((qd * qd).astype(BF16), hm_ref[...])
    _store_head_rows(qd_ref, qd * lax.rsqrt(ms + RMS_EPS) * (gq_ref[...] * (qk_scale * LOG2E)),
                     _alibi_slope_parts())

    kd = _dot(n, wkd_ref[...])
    kn = kd * lax.rsqrt(jnp.mean(kd * kd, axis=-1, keepdims=True) + RMS_EPS) * gk_ref[...]
    lane = lax.broadcasted_iota(jnp.int32, (tm, LANES), 1)
    pos = ((pl.program_id(0) % tiles_per_batch) * tm
           + lax.broadcasted_iota(jnp.int32, (tm, LANES), 0))
    feat = jnp.zeros((tm, LANES), jnp.int32)
    for rep in range(N_SLOPE_PARTS):
        feat = jnp.where(lane == LANE_POS_HI + rep * N_POS_LANES,
                         pos >> (HEAD_DIM.bit_length() - 1), feat)
        feat = jnp.where(lane == LANE_POS_LO + rep * N_POS_LANES, pos & (HEAD_DIM - 1), feat)
    feat = feat.astype(F32)
    kk_ref[...] = jnp.where(lane < HEAD_DIM, kn, feat).astype(BF16)

    _store_head_rows(qi_ref, _dot(n, wqi_ref[...]), None)
    kki_ref[...] = _dot(n, wki_ref[...]).astype(BF16)
    vt = _dot_nt(wvt_ref[...], n)
    row = lax.broadcasted_iota(jnp.int32, vt.shape, 0)
    vt_ref[...] = jnp.where(row == HEAD_DIM, 1.0, vt).astype(BF16)
    wit_ref[...] = _dot_nt(wwt_ref[...], n)[:N_HEADS]


def _inproj_call(n2, w_in, g_q, g_k, batch, seq, tm=512):
    n_tok, d = n2.shape
    sizes = (W_HEADS, W_HEADS, W_HEADS, W_HEADS, HEAD_DIM, HEAD_DIM, N_HEADS * D_IDX, D_IDX, N_HEADS)
    offs = np.concatenate([[0], np.cumsum(sizes)])
    col = lambda k: w_in[:, offs[k]:offs[k + 1]]
    idx_scale = (D_IDX ** -0.5) * (N_HEADS ** -0.5)
    w_sb = jnp.concatenate([col(0), col(1), col(2)], axis=1).astype(BF16)
    w_qd = col(3).astype(BF16)
    w_kd = jnp.concatenate([col(4), col(4)], axis=1).astype(BF16)
    w_qi = col(6).astype(BF16)
    w_ki = jnp.pad(col(7), ((0, 0), (0, LANES - D_IDX))).astype(BF16)
    w_vt = jnp.pad(col(5).T, ((0, VT_ROWS - HEAD_DIM), (0, 0))).astype(BF16)
    w_wt = jnp.pad((col(8) * idx_scale).T, ((0, 16 - N_HEADS), (0, 0))).astype(BF16)
    gq = jnp.tile(g_q, N_HEADS).reshape(1, W_HEADS)
    gk = jnp.tile(g_k, 2).reshape(1, LANES)
    head_of = np.arange(W_HEADS) // HEAD_DIM
    head_mean = jnp.asarray((head_of[:, None] == head_of[None, :]) / HEAD_DIM, BF16)

    tpb = seq // tm
    tok = lambda w: pl.BlockSpec((tm, w), lambda i: (i, 0))
    tr = lambda r: pl.BlockSpec((None, r, tm), lambda i: (i // tpb, 0, i % tpb))
    sd = lambda w, dt: jax.ShapeDtypeStruct((n_tok, w), dt)
    consts = (w_sb, w_qd, w_kd, w_qi, w_ki, w_vt, w_wt, gq, gk, head_mean)
    head_rows = jax.ShapeDtypeStruct((n_tok // TQ_DSA, N_HEADS, TQ_DSA, LANES), BF16)
    head_rows_spec = pl.BlockSpec((tm // TQ_DSA, N_HEADS, TQ_DSA, LANES), lambda i: (i, 0, 0, 0))
    return pl.pallas_call(
        functools.partial(_inproj_kernel, tiles_per_batch=tpb),
        out_shape=[sd(W_HEADS, BF16)] * 3 + [head_rows, sd(LANES, BF16), head_rows, sd(LANES, BF16),
                                             jax.ShapeDtypeStruct((batch, VT_ROWS, seq), BF16),
                                             jax.ShapeDtypeStruct((batch, N_HEADS, seq), F32)],
        grid=(n_tok // tm,),
        in_specs=[tok(d)] + [_const_spec(w.shape) for w in consts],
        out_specs=[tok(W_HEADS)] * 3 + [head_rows_spec, tok(LANES), head_rows_spec, tok(LANES),
                                        tr(VT_ROWS), tr(N_HEADS)],
        compiler_params=_cparams(1),
        name="inproj",
    )(n2, *consts)


def _split_heads(x, lane):
    xf = x.astype(F32)
    lo = jnp.where(lane < HEAD_DIM, xf, 0.0).astype(x.dtype)
    hi = jnp.where(lane >= HEAD_DIM, xf, 0.0).astype(x.dtype)
    return lo, hi


def _sb_kernel(q_ref, k_ref, v_ref, t_ref, o_ref, qs_ref, run_ref, acc_ref):
    i = pl.program_id(1)
    lane = lax.broadcasted_iota(jnp.int32, (TQ, LANES), 1)
    lane_k = lax.broadcasted_iota(jnp.int32, (TK, LANES), 1)
    for p in range(N_PAIRS):
        lo, hi = _split_heads(q_ref[:, p * LANES:(p + 1) * LANES], lane)
        qs_ref[p, :TQ, :] = lo
        qs_ref[p, TQ:, :] = hi

    def block(k0, diag):
        pairs = range(N_PAIRS)
        strict = (lax.broadcasted_iota(jnp.int32, (2 * TQ, TK), 1)
                  < lax.broadcasted_iota(jnp.int32, (2 * TQ, TK), 0) % TQ) if diag else None
        zs = [_dot_nt(qs_ref[p], k_ref[pl.ds(k0, TK), p * LANES:(p + 1) * LANES]) for p in pairs]
        lbs, ws = [], []
        for p in pairs:
            z = zs[p]
            sp = jnp.maximum(z, 0.0) + jnp.log(1.0 + jnp.exp(-jnp.abs(z)))
            l1m = jnp.where(strict, sp, 0.0) if diag else sp
            l_hi = l1m.astype(BF16)
            l_lo = (l1m - l_hi.astype(F32)).astype(BF16)
            ws.append(_dot(jnp.concatenate([l_hi, l_lo], axis=1), t_ref[...]))
            lbs.append(z - sp)
        for p in pairs:
            if diag:
                a = jnp.where(strict, jnp.exp(lbs[p] + ws[p][:, :TK]), 0.0).astype(BF16)
            else:
                a = jnp.exp(lbs[p] + ws[p][:, :TK] + run_ref[p]).astype(BF16)
            v_lo, v_hi = _split_heads(v_ref[pl.ds(k0, TK), p * LANES:(p + 1) * LANES], lane_k)
            upd = _dot(jnp.concatenate([a[:TQ], a[TQ:]], axis=1),
                       jnp.concatenate([v_lo, v_hi], axis=0))
            if diag:
                acc_ref[p] = upd
                run_ref[p] = ws[p][:, TK:]
            else:
                acc_ref[p] += upd
                run_ref[p] += ws[p][:, TK:]

    def more_blocks(jj):
        run_max = functools.reduce(jnp.maximum, [jnp.max(run_ref[p]) for p in range(N_PAIRS)])
        return jnp.logical_and(jj <= i, run_max > -F32_EXP_UNDERFLOW)

    block(pl.multiple_of(i * TK, TK), True)

    def body(state):
        jj, _ = state
        block(pl.multiple_of((i - jj) * TK, TK), False)
        return jj + 1, more_blocks(jj + 1)

    lax.while_loop(lambda state: state[1], body, (jnp.int32(1), more_blocks(jnp.int32(1))))
    for p in range(N_PAIRS):
        o_ref[:, p * LANES:(p + 1) * LANES] = acc_ref[p].astype(o_ref.dtype)


def _sb_call(q, k, v, batch, seq):
    q, k, v = (t.reshape(batch, seq, W_HEADS) for t in (q, k, v))
    j_idx = np.arange(TK)
    suffix = (j_idx[:, None] > j_idx[None, :]).astype(np.float32)
    tmat = -np.concatenate([suffix, np.ones((TK, LANES), np.float32)], axis=1)
    tmat = jnp.asarray(np.concatenate([tmat, tmat], axis=0), BF16)
    blk = pl.BlockSpec((None, TQ, W_HEADS), lambda b, i: (b, i, 0))
    allk = pl.BlockSpec((None, seq, W_HEADS), lambda b, i: (b, 0, 0))
    return pl.pallas_call(
        _sb_kernel,
        out_shape=jax.ShapeDtypeStruct((batch, seq, W_HEADS), BF16),
        grid=(batch, seq // TQ),
        in_specs=[blk, allk, allk, _const_spec(tmat.shape)],
        out_specs=blk,
        scratch_shapes=[pltpu.VMEM((N_PAIRS, 2 * TQ, LANES), BF16),
                        pltpu.VMEM((N_PAIRS, 2 * TQ, LANES), F32),
                        pltpu.VMEM((N_PAIRS, TQ, LANES), F32)],
        compiler_params=_cparams(2),
        name="sb",
    )(q, k, v, tmat)


def _dsa_kernel(qd_st, qi_st, wit_ref, kk_ref, vt_ref, kki_ref, o_ref,
                sc_ref, top_ref, m_ref, acc_ref, buf_ref, *, topk):
    TQ = qd_st.shape[1]
    i = pl.program_id(1)
    t0 = i * TQ

    n_s = (i + CH_S // TQ) // (CH_S // TQ)
    n_h = n_s * (CH_S // CH_H)

    def products(q_st, k_ref):
        def produce(k0):
            kch = k_ref[pl.ds(k0, CH_H), :]
            return [_dot_nt(kch, q_st[2 * p:2 * p + 2].reshape(2 * TQ, LANES)) for p in range(N_PAIRS)]
        return produce

    def store_products(vals):
        for p in range(N_PAIRS):
            buf_ref[:, p * 2 * TQ:(p + 1) * 2 * TQ] = vals[p]

    def sweep(produce, consume):
        from_buf = lambda h: buf_ref[:, h * TQ:(h + 1) * TQ]

        def trip(c, _):
            k_a = pl.multiple_of(c * CH_S, CH_S)
            k_b = pl.multiple_of(k_a + CH_H, CH_H)
            vals_b = produce(k_b)
            consume(from_buf, k_a)
            vals_next = produce(pl.multiple_of(jnp.minimum(c + 1, n_s - 1) * CH_S, CH_S))
            consume(lambda h: vals_b[h // 2][:, (h % 2) * TQ:(h % 2 + 1) * TQ], k_b)
            store_products(vals_next)
            return 0

        store_products(produce(0))
        lax.fori_loop(0, n_s, trip, 0)

    wit = wit_ref[...]

    def score_consume(prod, k0):
        score = None
        for h in range(N_HEADS):
            term = jnp.maximum(prod(h), 0.0) * wit[h:h + 1, :]
            score = term if score is None else score + term
        causal = (k0 + lax.broadcasted_iota(jnp.int32, (CH_H, TQ), 0)
                  <= t0 + lax.broadcasted_iota(jnp.int32, (CH_H, TQ), 1))
        score = jnp.where(causal, score, -jnp.inf)
        sc_ref[pl.ds(k0, CH_H), :] = score
        top_ref[pl.ds(k0, CH_H), :] = score.astype(BF16)

    sweep(products(qi_st, kki_ref), score_consume)

    def float_of_rank(u):
        key = u ^ INT_MIN
        return pltpu.bitcast(jnp.where(key < 0, key ^ np.int32(0x7FFFFFFF), key), F32)

    def count_top_ge(cand):
        one, zero = jnp.ones((BF16_SUBLANES, TQ), BF16), jnp.zeros((BF16_SUBLANES, TQ), BF16)

        def cbody(c, acc):
            k0 = pl.multiple_of(c * CH_S, CH_S)
            tops = top_ref[pl.ds(k0, CH_S), :]
            parts = [jnp.where(tops[j * BF16_SUBLANES:(j + 1) * BF16_SUBLANES] >= cand, one, zero)
                     for j in range(CH_S // BF16_SUBLANES)]
            while len(parts) > 1:
                parts = [a + b for a, b in zip(parts[::2], parts[1::2])]
            return acc + parts[0]

        acc = lax.fori_loop(0, n_s, cbody, zero)
        return jnp.sum(acc.astype(F32), axis=0, keepdims=True)

    def top_body(b, u16):
        cand = u16 | jnp.left_shift(jnp.int32(1), 15 - b)
        cand_f = float_of_rank(jnp.left_shift(cand, 16)).astype(BF16)
        return jnp.where(count_top_ge(jnp.broadcast_to(cand_f, (BF16_SUBLANES, TQ))) >= topk,
                         cand, u16)

    u16 = lax.fori_loop(0, 16, top_body, jnp.zeros((1, TQ), jnp.int32))

    def count(pred):
        def cbody(c, acc):
            k0 = pl.multiple_of(c * CH_S, CH_S)
            sch = sc_ref[pl.ds(k0, CH_S), :].reshape(CH_S // SUBLANES, SUBLANES, TQ)
            return acc + jnp.sum(jnp.where(pred(sch), 1, 0), axis=0)
        acc = lax.fori_loop(0, n_s, cbody, jnp.zeros((SUBLANES, TQ), jnp.int32))
        tot = jnp.sum(acc.astype(F32), axis=0, keepdims=True)
        return jnp.broadcast_to(tot, (SUBLANES, TQ))

    base = jnp.broadcast_to(jnp.left_shift(u16, 16) - (1 << 15), (SUBLANES, TQ))

    def window_body(b, off):
        cand = off | jnp.left_shift(jnp.int32(1), SELECT_WINDOW_BITS - 1 - b)
        cand_f = float_of_rank(base + cand)
        return jnp.where(count(lambda sch: sch >= cand_f[None]) >= topk, cand, off)

    offset = lax.fori_loop(0, SELECT_WINDOW_BITS, window_body,
                           jnp.zeros((SUBLANES, TQ), jnp.int32))
    few_keys = t0 + lax.broadcasted_iota(jnp.int32, (SUBLANES, TQ), 1) < topk
    thr = jnp.where(few_keys, jnp.finfo(F32).min, float_of_rank(base + offset))
    thr_row = thr[0:1, :]

    n_ge = count(lambda sch: sch >= thr[None])

    @pl.when(jnp.max(n_ge) > topk)
    def _():
        need = topk - count(lambda sch: sch > thr[None])[0:1, :]
        before = jnp.where(lax.broadcasted_iota(jnp.int32, (CH_H, CH_H), 1)
                           < lax.broadcasted_iota(jnp.int32, (CH_H, CH_H), 0), 1.0, 0.0).astype(BF16)

        def tie_body(c, seen):
            k0 = pl.multiple_of(c * CH_H, CH_H)
            sch = sc_ref[pl.ds(k0, CH_H), :]
            eq = jnp.where(sch == thr_row, 1.0, 0.0)
            rank = _dot(before, eq.astype(BF16)) + seen
            tie_rank = jnp.where(sch == thr_row, rank, -1.0)
            sc_ref[pl.ds(k0, CH_H), :] = jnp.where(tie_rank >= need, -jnp.inf, sch)
            return seen + jnp.sum(eq, axis=0, keepdims=True)

        lax.fori_loop(0, n_h, tie_body, jnp.zeros((1, TQ), F32))

    m_ref[...] = jnp.full(m_ref.shape, NEG_BIG, F32)
    acc_ref[...] = jnp.zeros(acc_ref.shape, F32)

    def attn_consume(prod, k0):
        vch = vt_ref[:, pl.ds(k0, CH_H)]
        sel = sc_ref[pl.ds(k0, CH_H), :] >= thr_row
        for p in range(N_PAIRS):
            cols = slice(p * 2 * TQ, (p + 1) * 2 * TQ)
            probs, alphas = [], []
            for h in (2 * p, 2 * p + 1):
                lg = jnp.where(sel, prod(h), NEG_BIG)
                m_old = m_ref[:, h * TQ:(h + 1) * TQ]
                m_new = jnp.maximum(m_old, jnp.max(lg, axis=0, keepdims=True))
                m_ref[:, h * TQ:(h + 1) * TQ] = m_new
                alphas.append(jnp.exp2(m_old - m_new))
                probs.append(jnp.exp2(lg - m_new).astype(BF16))
            upd = _dot(vch, jnp.concatenate(probs, axis=1))
            acc_ref[:, cols] = acc_ref[:, cols] * jnp.concatenate(alphas, axis=1) + upd

    sweep(products(qd_st, kk_ref), attn_consume)
    acc = acc_ref[...]
    out_t = acc[:HEAD_DIM] / acc[HEAD_DIM:HEAD_DIM + 1]
    for p in range(N_PAIRS):
        pair = jnp.concatenate([out_t[:, (2 * p) * TQ:(2 * p + 1) * TQ],
                                out_t[:, (2 * p + 1) * TQ:(2 * p + 2) * TQ]], axis=0)
        o_ref[:, p * LANES:(p + 1) * LANES] = pair.T.astype(o_ref.dtype)


def _dsa_call(qd_rows, qi_rows, wit, kk, vt, kki, batch, seq):
    r3 = lambda t: t.reshape(batch, seq, t.shape[-1])
    kk, kki = map(r3, (kk, kki))
    topk = min(TOPK_MAX, seq // 4)
    assert seq // BF16_SUBLANES <= BF16_MAX_EXACT_COUNT
    TQ = TQ_DSA
    nq = seq // TQ
    rows = pl.BlockSpec((None, N_HEADS, TQ, LANES), lambda b, i: (b * nq + i, 0, 0, 0))
    allk = pl.BlockSpec((None, seq, LANES), lambda b, i: (b, 0, 0))
    return pl.pallas_call(
        functools.partial(_dsa_kernel, topk=topk),
        out_shape=jax.ShapeDtypeStruct((batch, seq, W_HEADS), BF16),
        grid=(batch, nq),
        in_specs=[rows, rows,
                  pl.BlockSpec((None, N_HEADS, TQ), lambda b, i: (b, 0, i)),
                  allk, pl.BlockSpec((None, VT_ROWS, seq), lambda b, i: (b, 0, 0)), allk],
        out_specs=pl.BlockSpec((None, TQ, W_HEADS), lambda b, i: (b, i, 0)),
        scratch_shapes=[pltpu.VMEM((seq, TQ), F32),
                        pltpu.VMEM((seq, TQ), BF16),
                        pltpu.VMEM((1, N_HEADS * TQ), F32),
                        pltpu.VMEM((VT_ROWS, N_HEADS * TQ), F32),
                        pltpu.VMEM((CH_H, N_HEADS * TQ), F32)],
        compiler_params=_cparams(2),
        name="dsa",
    )(qd_rows, qi_rows, wit, kk, vt, kki)


def _mix_kernel(h_ref, n_ref, ysb_ref, yd_ref, mod_ref, wps_ref, wpd_ref, wg_ref, bg_ref,
                wo_ref, o_ref):
    d = h_ref.shape[1]
    y_sb = _dot(ysb_ref[...], wps_ref[...])
    y_d = _dot(yd_ref[...], wpd_ref[...])
    gates = jax.nn.sigmoid(_dot(n_ref[...], wg_ref[...]) + bg_ref[...])
    merged = (gates[:, :d] * y_sb + gates[:, d:] * y_d).astype(BF16)
    o_ref[...] = h_ref[...] + mod_ref[5] * _dot(merged, wo_ref[...])


def _mix_call(h, n2, y_sb, y_d, mod, w_proj_sb, w_proj_dsa, w_gate, b_gate, w_out, *, seq, tm=512):
    n_tok, d = h.shape
    tok = lambda w: pl.BlockSpec((tm, w), lambda i: (i, 0))
    ws = [w.astype(BF16) for w in (w_proj_sb, w_proj_dsa, w_gate)]
    bg = b_gate.reshape(1, -1)
    wo = w_out.astype(BF16)
    return pl.pallas_call(
        _mix_kernel,
        out_shape=jax.ShapeDtypeStruct((n_tok, d), F32),
        grid=(n_tok // tm,),
        in_specs=[tok(d), tok(d), tok(W_HEADS), tok(W_HEADS), _mod_spec(seq // tm, d)]
                 + [_const_spec(w.shape) for w in (*ws, bg, wo)],
        out_specs=tok(d),
        compiler_params=_cparams(1),
        name="mix",
    )(h, n2, y_sb, y_d, mod, *ws, bg, wo)


def kernel(x, c, w_ada, b_ada, g_ffn1, w1_ffn1, w3_ffn1, w2_ffn1, g_mix, w_in, g_q_dsa, g_k_dsa,
           w_proj_sb, w_proj_dsa, w_gate, b_gate, w_out, g_ffn2, w1_ffn2, w3_ffn2, w2_ffn2):
    batch, seq, d = x.shape
    h = x.reshape(batch * seq, d)
    for l in range(w_ada.shape[0]):
        bf = lambda w: w[l].astype(BF16)
        mod = _mod_call(c, w_ada[l], b_ada[l])
        h, n2 = _ffn_call(h, mod, g_ffn1[l], bf(w1_ffn1), bf(w3_ffn1), bf(w2_ffn1), g_mix[l],
                          mod_base=0, emit_next=True, seq=seq)
        q_sb, k_sb, v_sb, q_d, kk, q_i, kki, vt, wit = _inproj_call(
            n2, w_in[l], g_q_dsa[l], g_k_dsa[l], batch, seq)
        y_sb = _sb_call(q_sb, k_sb, v_sb, batch, seq).reshape(batch * seq, W_HEADS)
        y_d = _dsa_call(q_d, q_i, wit, kk, vt, kki, batch, seq).reshape(batch * seq, W_HEADS)
        h = _mix_call(h, n2, y_sb, y_d, mod, w_proj_sb[l], w_proj_dsa[l], w_gate[l], b_gate[l],
                      w_out[l], seq=seq)
        (h,) = _ffn_call(h, mod, g_ffn2[l], bf(w1_ffn2), bf(w3_ffn2), bf(w2_ffn2), g_ffn2[l],
                         mod_base=6, emit_next=False, seq=seq)
    return h.reshape(batch, seq, d)
```

```python
import functools

import numpy as np
import jax
import jax.numpy as jnp
from jax import lax
from jax.experimental import pallas as pl
from jax.experimental.pallas import tpu as pltpu

F32 = jnp.float32
BF16 = jnp.bfloat16

HEAD_DIM = 64
N_HEADS = 8
W_HEADS = N_HEADS * HEAD_DIM
N_PAIRS = N_HEADS // 2
LANES = 128
SUBLANES = 8
BF16_SUBLANES = 16
BF16_MAX_EXACT_COUNT = 256
SELECT_WINDOW_BITS = 17
TOPK_MAX = 256
N_MOD = 9
RMS_EPS = 1e-6
D_IDX = 64

TQ = 128
TQ_DSA = 128
TK = 128
CH_S = 512
CH_H = 256
VT_ROWS = 80
LANE_POS_HI, LANE_POS_LO = HEAD_DIM, HEAD_DIM + 1
N_POS_LANES = 2
N_SLOPE_PARTS = 3
LOG2E = float(np.log2(np.e))
NEG_BIG = -1e30
F32_EXP_UNDERFLOW = 104.0
INT_MIN = np.int32(-2**31)
VMEM_LIMIT = 56 * 1024 * 1024


def _cparams(n_axes):
    return pltpu.CompilerParams(
        dimension_semantics=("arbitrary",) * n_axes, vmem_limit_bytes=VMEM_LIMIT)


def _const_spec(shape):
    nd = len(shape)
    return pl.BlockSpec(shape, lambda *_: (0,) * nd, pipeline_mode=pl.Buffered(1))


def _dot(a, b):
    return jnp.dot(a, b, preferred_element_type=F32)


def _dot_nt(a, b):
    return lax.dot_general(a, b, (((1,), (1,)), ((), ())), preferred_element_type=F32)


def _mod_kernel(c_ref, w_ref, b_ref, o_ref, sc_ref):
    nb, kdim, _ = c_ref.shape
    tn = w_ref.shape[1]
    nch = tn // LANES

    @pl.when(pl.program_id(0) == 0)
    def _():
        c = c_ref[...]
        sc_ref[...] = c * jax.nn.sigmoid(c)

    def body(kb, accs):
        k0 = pl.multiple_of(kb * SUBLANES, SUBLANES)
        wblk = w_ref[pl.ds(k0, SUBLANES), :]
        out = []
        for b in range(nb):
            sb = sc_ref[b, pl.ds(k0, SUBLANES), :]
            for ch in range(nch):
                out.append(accs[b * nch + ch] + wblk[:, ch * LANES:(ch + 1) * LANES] * sb)
        return tuple(out)

    accs = lax.fori_loop(0, kdim // SUBLANES, body,
                         tuple(jnp.zeros((SUBLANES, LANES), F32) for _ in range(nb * nch)))
    for b in range(nb):
        row = jnp.concatenate(
            [jnp.sum(accs[b * nch + ch], axis=0, keepdims=True) for ch in range(nch)], axis=1)
        o_ref[0, b] = row + b_ref[...]


def _mod_call(c, w_ada, b_ada):
    nb, d = c.shape
    n_out = w_ada.shape[1]
    c_rep = jnp.broadcast_to(c[:, :, None], (nb, d, LANES))
    return pl.pallas_call(
        _mod_kernel,
        out_shape=jax.ShapeDtypeStruct((n_out // d, nb, 1, d), F32),
        grid=(n_out // d,),
        in_specs=[_const_spec((nb, d, LANES)),
                  pl.BlockSpec((d, d), lambda j: (0, j)),
                  pl.BlockSpec((1, d), lambda j: (0, j))],
        out_specs=pl.BlockSpec((1, nb, 1, d), lambda j: (j, 0, 0, 0)),
        scratch_shapes=[pltpu.VMEM((nb, d, LANES), F32)],
        compiler_params=_cparams(1),
        name="mod",
    )(c_rep, w_ada, b_ada.reshape(1, n_out))


def _norm_mod(h, g, shift, scale):
    y = h * lax.rsqrt(jnp.mean(h * h, axis=-1, keepdims=True) + RMS_EPS)
    return (y * g) * (1.0 + scale) + shift


def _ffn_kernel(h_ref, mod_ref, g_ref, w1_ref, w3_ref, w2_ref, gn_ref, *out_refs,
                mod_base, emit_next, n_chunks):
    o_ref = out_refs[0]
    h = h_ref[...]
    n = _norm_mod(h, g_ref[...], mod_ref[mod_base], mod_ref[mod_base + 1]).astype(BF16)
    fc = w1_ref.shape[1] // n_chunks
    acc = jnp.zeros(h.shape, F32)
    for c in range(n_chunks):
        a = _dot(n, w1_ref[:, c * fc:(c + 1) * fc])
        b = _dot(n, w3_ref[:, c * fc:(c + 1) * fc])
        act = (a * jax.nn.sigmoid(a) * b).astype(BF16)
        acc = acc + _dot(act, w2_ref[c * fc:(c + 1) * fc, :])
    hn = h + (0.5 * mod_ref[mod_base + 2]) * acc
    o_ref[...] = hn
    if emit_next:
        out_refs[1][...] = _norm_mod(hn, gn_ref[...], mod_ref[mod_base + 3],
                                     mod_ref[mod_base + 4]).astype(BF16)


def _mod_spec(tiles_per_batch, d):
    return pl.BlockSpec((N_MOD, None, 1, d), lambda i: (0, i // tiles_per_batch, 0, 0))


def _ffn_call(h, mod, g, w1, w3, w2, g_next, *, mod_base, emit_next, seq, tm=512):
    n_tok, d = h.shape
    f = w1.shape[1]
    tok = pl.BlockSpec((tm, d), lambda i: (i, 0))
    out_shape = [jax.ShapeDtypeStruct((n_tok, d), F32)]
    out_specs = [tok]
    if emit_next:
        out_shape.append(jax.ShapeDtypeStruct((n_tok, d), BF16))
        out_specs.append(tok)
    return pl.pallas_call(
        functools.partial(_ffn_kernel, mod_base=mod_base, emit_next=emit_next, n_chunks=2),
        out_shape=out_shape,
        grid=(n_tok // tm,),
        in_specs=[tok, _mod_spec(seq // tm, d), _const_spec((1, d)),
                  _const_spec((d, f)), _const_spec((d, f)), _const_spec((f, d)),
                  _const_spec((1, d))],
        out_specs=out_specs,
        compiler_params=_cparams(1),
        name="ffn",
    )(h, mod, g.reshape(1, d), w1, w3, w2, g_next.reshape(1, d))


def _alibi_slope_parts():
    out = []
    for h in range(N_HEADS):
        rest, parts = np.float32(2.0 ** (-8.0 * (h + 1) / N_HEADS) * LOG2E), []
        for _ in range(N_SLOPE_PARTS):
            parts.append(np.float32(np.asarray(rest, dtype=BF16)))
            rest = np.float32(rest - parts[-1])
        out.append(parts)
    return out


def _store_head_rows(o_ref, x, slope_parts):
    tm = x.shape[0]
    lane = lax.broadcasted_iota(jnp.int32, (tm, LANES), 1)
    for h in range(N_HEADS):
        rows = x[:, (h // 2) * LANES:(h // 2 + 1) * LANES]
        if h % 2:
            rows = pltpu.roll(rows, HEAD_DIM, axis=1)
        rows = jnp.where(lane < HEAD_DIM, rows, 0.0)
        if slope_parts is not None:
            for rep, part in enumerate(slope_parts[h]):
                rows = jnp.where(lane == LANE_POS_HI + rep * N_POS_LANES, part * HEAD_DIM, rows)
                rows = jnp.where(lane == LANE_POS_LO + rep * N_POS_LANES, part, rows)
        o_ref[:, h] = rows.astype(BF16).reshape(tm // TQ_DSA, TQ_DSA, LANES)


def _inproj_kernel(n_ref, wsb_ref, wqd_ref, wkd_ref, wqi_ref, wki_ref, wvt_ref, wwt_ref,
                   gq_ref, gk_ref, hm_ref,
                   qsb_ref, ksb_ref, vsb_ref, qd_ref, kk_ref, qi_ref, kki_ref, vt_ref, wit_ref,
                   *, tiles_per_batch):
    n = n_ref[...]
    tm = n.shape[0]
    qk_scale = HEAD_DIM ** -0.5
    sb = _dot(n, wsb_ref[...])
    qsb_ref[...] = (sb[:, :W_HEADS] * qk_scale).astype(BF16)
    ksb_ref[...] = sb[:, W_HEADS:2 * W_HEADS].astype(BF16)
    vsb_ref[...] = sb[:, 2 * W_HEADS:].astype(BF16)

    qd = _dot(n, wqd_ref[...])
    ms = _dot((qd * qd).astype(BF16), hm_ref[...])
    _store_head_rows(qd_ref, qd * lax.rsqrt(ms + RMS_EPS) * (gq_ref[...] * (qk_scale * LOG2E)),
                     _alibi_slope_parts())

    kd = _dot(n, wkd_ref[...])
    kn = kd * lax.rsqrt(jnp.mean(kd * kd, axis=-1, keepdims=True) + RMS_EPS) * gk_ref[...]
    lane = lax.broadcasted_iota(jnp.int32, (tm, LANES), 1)
    pos = ((pl.program_id(0) % tiles_per_batch) * tm
           + lax.broadcasted_iota(jnp.int32, (tm, LANES), 0))
    feat = jnp.zeros((tm, LANES), jnp.int32)
    for rep in range(N_SLOPE_PARTS):
        feat = jnp.where(lane == LANE_POS_HI + rep * N_POS_LANES,
                         pos >> (HEAD_DIM.bit_length() - 1), feat)
        feat = jnp.where(lane == LANE_POS_LO + rep * N_POS_LANES, pos & (HEAD_DIM - 1), feat)
    feat = feat.astype(F32)
    kk_ref[...] = jnp.where(lane < HEAD_DIM, kn, feat).astype(BF16)

    _store_head_rows(qi_ref, _dot(n, wqi_ref[...]), None)
    kki_ref[...] = _dot(n, wki_ref[...]).astype(BF16)
    vt = _dot_nt(wvt_ref[...], n)
    row = lax.broadcasted_iota(jnp.int32, vt.shape, 0)
    vt_ref[...] = jnp.where(row == HEAD_DIM, 1.0, vt).astype(BF16)
    wit_ref[...] = _dot_nt(wwt_ref[...], n)[:N_HEADS]


def _inproj_call(n2, w_in, g_q, g_k, batch, seq, tm=512):
    n_tok, d = n2.shape
    sizes = (W_HEADS, W_HEADS, W_HEADS, W_HEADS, HEAD_DIM, HEAD_DIM, N_HEADS * D_IDX, D_IDX, N_HEADS)
    offs = np.concatenate([[0], np.cumsum(sizes)])
    col = lambda k: w_in[:, offs[k]:offs[k + 1]]
    idx_scale = (D_IDX ** -0.5) * (N_HEADS ** -0.5)
    w_sb = jnp.concatenate([col(0), col(1), col(2)], axis=1).astype(BF16)
    w_qd = col(3).astype(BF16)
    w_kd = jnp.concatenate([col(4), col(4)], axis=1).astype(BF16)
    w_qi = col(6).astype(BF16)
    w_ki = jnp.pad(col(7), ((0, 0), (0, LANES - D_IDX))).astype(BF16)
    w_vt = jnp.pad(col(5).T, ((0, VT_ROWS - HEAD_DIM), (0, 0))).astype(BF16)
    w_wt = jnp.pad((col(8) * idx_scale).T, ((0, 16 - N_HEADS), (0, 0))).astype(BF16)
    gq = jnp.tile(g_q, N_HEADS).reshape(1, W_HEADS)
    gk = jnp.tile(g_k, 2).reshape(1, LANES)
    head_of = np.arange(W_HEADS) // HEAD_DIM
    head_mean = jnp.asarray((head_of[:, None] == head_of[None, :]) / HEAD_DIM, BF16)

    tpb = seq // tm
    tok = lambda w: pl.BlockSpec((tm, w), lambda i: (i, 0))
    tr = lambda r: pl.BlockSpec((None, r, tm), lambda i: (i // tpb, 0, i % tpb))
    sd = lambda w, dt: jax.ShapeDtypeStruct((n_tok, w), dt)
    consts = (w_sb, w_qd, w_kd, w_qi, w_ki, w_vt, w_wt, gq, gk, head_mean)
    head_rows = jax.ShapeDtypeStruct((n_tok // TQ_DSA, N_HEADS, TQ_DSA, LANES), BF16)
    head_rows_spec = pl.BlockSpec((tm // TQ_DSA, N_HEADS, TQ_DSA, LANES), lambda i: (i, 0, 0, 0))
    return pl.pallas_call(
        functools.partial(_inproj_kernel, tiles_per_batch=tpb),
        out_shape=[sd(W_HEADS, BF16)] * 3 + [head_rows, sd(LANES, BF16), head_rows, sd(LANES, BF16),
                                             jax.ShapeDtypeStruct((batch, VT_ROWS, seq), BF16),
                                             jax.ShapeDtypeStruct((batch, N_HEADS, seq), F32)],
        grid=(n_tok // tm,),
        in_specs=[tok(d)] + [_const_spec(w.shape) for w in consts],
        out_specs=[tok(W_HEADS)] * 3 + [head_rows_spec, tok(LANES), head_rows_spec, tok(LANES),
                                        tr(VT_ROWS), tr(N_HEADS)],
        compiler_params=_cparams(1),
        name="inproj",
    )(n2, *consts)


def _split_heads(x, lane):
    xf = x.astype(F32)
    lo = jnp.where(lane < HEAD_DIM, xf, 0.0).astype(x.dtype)
    hi = jnp.where(lane >= HEAD_DIM, xf, 0.0).astype(x.dtype)
    return lo, hi


def _sb_kernel(q_ref, k_ref, v_ref, t_ref, o_ref, qs_ref, run_ref, acc_ref):
    i = pl.program_id(1)
    lane = lax.broadcasted_iota(jnp.int32, (TQ, LANES), 1)
    lane_k = lax.broadcasted_iota(jnp.int32, (TK, LANES), 1)
    for p in range(N_PAIRS):
        lo, hi = _split_heads(q_ref[:, p * LANES:(p + 1) * LANES], lane)
        qs_ref[p, :TQ, :] = lo
        qs_ref[p, TQ:, :] = hi

    def block(k0, diag):
        pairs = range(N_PAIRS)
        strict = (lax.broadcasted_iota(jnp.int32, (2 * TQ, TK), 1)
                  < lax.broadcasted_iota(jnp.int32, (2 * TQ, TK), 0) % TQ) if diag else None
        zs = [_dot_nt(qs_ref[p], k_ref[pl.ds(k0, TK), p * LANES:(p + 1) * LANES]) for p in pairs]
        lbs, ws = [], []
        for p in pairs:
            z = zs[p]
            sp = jnp.maximum(z, 0.0) + jnp.log(1.0 + jnp.exp(-jnp.abs(z)))
            l1m = jnp.where(strict, sp, 0.0) if diag else sp
            l_hi = l1m.astype(BF16)
            l_lo = (l1m - l_hi.astype(F32)).astype(BF16)
            ws.append(_dot(jnp.concatenate([l_hi, l_lo], axis=1), t_ref[...]))
            lbs.append(z - sp)
        for p in pairs:
            if diag:
                a = jnp.where(strict, jnp.exp(lbs[p] + ws[p][:, :TK]), 0.0).astype(BF16)
            else:
                a = jnp.exp(lbs[p] + ws[p][:, :TK] + run_ref[p]).astype(BF16)
            v_lo, v_hi = _split_heads(v_ref[pl.ds(k0, TK), p * LANES:(p + 1) * LANES], lane_k)
            upd = _dot(jnp.concatenate([a[:TQ], a[TQ:]], axis=1),
                       jnp.concatenate([v_lo, v_hi], axis=0))
            if diag:
                acc_ref[p] = upd
                run_ref[p] = ws[p][:, TK:]
            else:
                acc_ref[p] += upd
                run_ref[p] += ws[p][:, TK:]

    def more_blocks(jj):
        run_max = functools.reduce(jnp.maximum, [jnp.max(run_ref[p]) for p in range(N_PAIRS)])
        return jnp.logical_and(jj <= i, run_max > -F32_EXP_UNDERFLOW)

    block(pl.multiple_of(i * TK, TK), True)

    def body(state):
        jj, _ = state
        block(pl.multiple_of((i - jj) * TK, TK), False)
        return jj + 1, more_blocks(jj + 1)

    lax.while_loop(lambda state: state[1], body, (jnp.int32(1), more_blocks(jnp.int32(1))))
    for p in range(N_PAIRS):
        o_ref[:, p * LANES:(p + 1) * LANES] = acc_ref[p].astype(o_ref.dtype)


def _sb_call(q, k, v, batch, seq):
    q, k, v = (t.reshape(batch, seq, W_HEADS) for t in (q, k, v))
    j_idx = np.arange(TK)
    suffix = (j_idx[:, None] > j_idx[None, :]).astype(np.float32)
    tmat = -np.concatenate([suffix, np.ones((TK, LANES), np.float32)], axis=1)
    tmat = jnp.asarray(np.concatenate([tmat, tmat], axis=0), BF16)
    blk = pl.BlockSpec((None, TQ, W_HEADS), lambda b, i: (b, i, 0))
    allk = pl.BlockSpec((None, seq, W_HEADS), lambda b, i: (b, 0, 0))
    return pl.pallas_call(
        _sb_kernel,
        out_shape=jax.ShapeDtypeStruct((batch, seq, W_HEADS), BF16),
        grid=(batch, seq // TQ),
        in_specs=[blk, allk, allk, _const_spec(tmat.shape)],
        out_specs=blk,
        scratch_shapes=[pltpu.VMEM((N_PAIRS, 2 * TQ, LANES), BF16),
                        pltpu.VMEM((N_PAIRS, 2 * TQ, LANES), F32),
                        pltpu.VMEM((N_PAIRS, TQ, LANES), F32)],
        compiler_params=_cparams(2),
        name="sb",
    )(q, k, v, tmat)


def _dsa_kernel(qd_st, qi_st, wit_ref, kk_ref, vt_ref, kki_ref, o_ref,
                sc_ref, top_ref, m_ref, acc_ref, buf_ref, *, topk):
    TQ = qd_st.shape[1]
    i = pl.program_id(1)
    t0 = i * TQ

    n_s = (i + CH_S // TQ) // (CH_S // TQ)
    n_h = n_s * (CH_S // CH_H)

    def products(q_st, k_ref):
        def produce(k0):
            kch = k_ref[pl.ds(k0, CH_H), :]
            return [_dot_nt(kch, q_st[2 * p:2 * p + 2].reshape(2 * TQ, LANES)) for p in range(N_PAIRS)]
        return produce

    def store_products(vals):
        for p in range(N_PAIRS):
            buf_ref[2 * p] = vals[p][:, :TQ]
            buf_ref[2 * p + 1] = vals[p][:, TQ:]

    def sweep(produce, consume):
        from_buf = lambda h: buf_ref[h]

        def trip(c, prefetch):
            k_a = pl.multiple_of(c * CH_S, CH_S)
            k_b = pl.multiple_of(k_a + CH_H, CH_H)
            vals_b = produce(k_b)
            consume(from_buf, k_a)
            if prefetch:
                vals_next = produce(pl.multiple_of(k_a + CH_S, CH_S))
            consume(lambda h: vals_b[h // 2][:, (h % 2) * TQ:(h % 2 + 1) * TQ], k_b)
            if prefetch:
                store_products(vals_next)

        store_products(produce(0))

        def body(c, _):
            trip(c, True)
            return 0

        lax.fori_loop(0, n_s - 1, body, 0)
        trip(n_s - 1, False)

    wit = wit_ref[...]

    def score_consume(prod, k0):
        score = None
        for h in range(N_HEADS):
            term = jnp.maximum(prod(h), 0.0) * wit[h:h + 1, :]
            score = term if score is None else score + term
        causal = (k0 + lax.broadcasted_iota(jnp.int32, (CH_H, TQ), 0)
                  <= t0 + lax.broadcasted_iota(jnp.int32, (CH_H, TQ), 1))
        score = jnp.where(causal, score, -jnp.inf)
        sc_ref[pl.ds(k0, CH_H), :] = score
        top_ref[pl.ds(k0, CH_H), :] = score.astype(BF16)

    sweep(products(qi_st, kki_ref), score_consume)

    def float_of_rank(u):
        key = u ^ INT_MIN
        return pltpu.bitcast(jnp.where(key < 0, key ^ np.int32(0x7FFFFFFF), key), F32)

    def count_top_ge(cand):
        one, zero = jnp.ones((BF16_SUBLANES, TQ), BF16), jnp.zeros((BF16_SUBLANES, TQ), BF16)

        def cbody(c, acc):
            k0 = pl.multiple_of(c * CH_S, CH_S)
            tops = top_ref[pl.ds(k0, CH_S), :]
            parts = [jnp.where(tops[j * BF16_SUBLANES:(j + 1) * BF16_SUBLANES] >= cand, one, zero)
                     for j in range(CH_S // BF16_SUBLANES)]
            while len(parts) > 1:
                parts = [a + b for a, b in zip(parts[::2], parts[1::2])]
            return acc + parts[0]

        acc = lax.fori_loop(0, n_s, cbody, zero)
        return jnp.sum(acc.astype(F32), axis=0, keepdims=True)

    def top_body(b, u16):
        cand = u16 | jnp.left_shift(jnp.int32(1), 15 - b)
        cand_f = float_of_rank(jnp.left_shift(cand, 16)).astype(BF16)
        return jnp.where(count_top_ge(jnp.broadcast_to(cand_f, (BF16_SUBLANES, TQ))) >= topk,
                         cand, u16)

    u16 = lax.fori_loop(0, 16, top_body, jnp.zeros((1, TQ), jnp.int32))

    def count(pred):
        def cbody(c, acc):
            k0 = pl.multiple_of(c * CH_S, CH_S)
            sch = sc_ref[pl.ds(k0, CH_S), :].reshape(CH_S // SUBLANES, SUBLANES, TQ)
            return acc + jnp.sum(jnp.where(pred(sch), 1, 0), axis=0)
        acc = lax.fori_loop(0, n_s, cbody, jnp.zeros((SUBLANES, TQ), jnp.int32))
        tot = jnp.sum(acc.astype(F32), axis=0, keepdims=True)
        return jnp.broadcast_to(tot, (SUBLANES, TQ))

    base = jnp.broadcast_to(jnp.left_shift(u16, 16) - (1 << 15), (SUBLANES, TQ))

    def window_body(b, off):
        cand = off | jnp.left_shift(jnp.int32(1), SELECT_WINDOW_BITS - 1 - b)
        cand_f = float_of_rank(base + cand)
        return jnp.where(count(lambda sch: sch >= cand_f[None]) >= topk, cand, off)

    offset = lax.fori_loop(0, SELECT_WINDOW_BITS, window_body,
                           jnp.zeros((SUBLANES, TQ), jnp.int32))
    few_keys = t0 + lax.broadcasted_iota(jnp.int32, (SUBLANES, TQ), 1) < topk
    thr = jnp.where(few_keys, jnp.finfo(F32).min, float_of_rank(base + offset))
    thr_row = thr[0:1, :]

    n_ge = count(lambda sch: sch >= thr[None])

    @pl.when(jnp.max(n_ge) > topk)
    def _():
        need = topk - count(lambda sch: sch > thr[None])[0:1, :]
        before = jnp.where(lax.broadcasted_iota(jnp.int32, (CH_H, CH_H), 1)
                           < lax.broadcasted_iota(jnp.int32, (CH_H, CH_H), 0), 1.0, 0.0).astype(BF16)

        def tie_body(c, seen):
            k0 = pl.multiple_of(c * CH_H, CH_H)
            sch = sc_ref[pl.ds(k0, CH_H), :]
            eq = jnp.where(sch == thr_row, 1.0, 0.0)
            rank = _dot(before, eq.astype(BF16)) + seen
            tie_rank = jnp.where(sch == thr_row, rank, -1.0)
            sc_ref[pl.ds(k0, CH_H), :] = jnp.where(tie_rank >= need, -jnp.inf, sch)
            return seen + jnp.sum(eq, axis=0, keepdims=True)

        lax.fori_loop(0, n_h, tie_body, jnp.zeros((1, TQ), F32))

    m_ref[...] = jnp.full(m_ref.shape, NEG_BIG, F32)
    acc_ref[...] = jnp.zeros(acc_ref.shape, F32)

    def attn_consume(prod, k0):
        vch = vt_ref[:, pl.ds(k0, CH_H)]
        sel = sc_ref[pl.ds(k0, CH_H), :] >= thr_row
        for p in range(N_PAIRS):
            probs, alphas = [], []
            for h in (2 * p, 2 * p + 1):
                lg = jnp.where(sel, prod(h), NEG_BIG)
                m_old = m_ref[h]
                m_new = jnp.maximum(m_old, jnp.max(lg, axis=0, keepdims=True))
                m_ref[h] = m_new
                alphas.append(jnp.exp2(m_old - m_new))
                probs.append(jnp.exp2(lg - m_new).astype(BF16))
            upd = _dot(vch, jnp.concatenate(probs, axis=1))
            acc_ref[p] = acc_ref[p] * jnp.concatenate(alphas, axis=1) + upd

    sweep(products(qd_st, kk_ref), attn_consume)
    for p in range(N_PAIRS):
        acc = acc_ref[p]
        out_t = acc[:HEAD_DIM] / acc[HEAD_DIM:HEAD_DIM + 1]
        pair = jnp.concatenate([out_t[:, :TQ], out_t[:, TQ:]], axis=0)
        o_ref[:, p * LANES:(p + 1) * LANES] = pair.T.astype(o_ref.dtype)


def _dsa_call(qd_rows, qi_rows, wit, kk, vt, kki, batch, seq):
    r3 = lambda t: t.reshape(batch, seq, t.shape[-1])
    kk, kki = map(r3, (kk, kki))
    topk = min(TOPK_MAX, seq // 4)
    assert seq // BF16_SUBLANES <= BF16_MAX_EXACT_COUNT
    TQ = TQ_DSA
    nq = seq // TQ
    rows = pl.BlockSpec((None, N_HEADS, TQ, LANES), lambda b, i: (b * nq + i, 0, 0, 0))
    allk = pl.BlockSpec((None, seq, LANES), lambda b, i: (b, 0, 0))
    return pl.pallas_call(
        functools.partial(_dsa_kernel, topk=topk),
        out_shape=jax.ShapeDtypeStruct((batch, seq, W_HEADS), BF16),
        grid=(batch, nq),
        in_specs=[rows, rows,
                  pl.BlockSpec((None, N_HEADS, TQ), lambda b, i: (b, 0, i)),
                  allk, pl.BlockSpec((None, VT_ROWS, seq), lambda b, i: (b, 0, 0)), allk],
        out_specs=pl.BlockSpec((None, TQ, W_HEADS), lambda b, i: (b, i, 0)),
        scratch_shapes=[pltpu.VMEM((seq, TQ), F32),
                        pltpu.VMEM((seq, TQ), BF16),
                        pltpu.VMEM((N_HEADS, 1, TQ), F32),
                        pltpu.VMEM((N_PAIRS, VT_ROWS, 2 * TQ), F32),
                        pltpu.VMEM((N_HEADS, CH_H, TQ), F32)],
        compiler_params=_cparams(2),
        name="dsa",
    )(qd_rows, qi_rows, wit, kk, vt, kki)


def _mix_kernel(h_ref, n_ref, ysb_ref, yd_ref, mod_ref, wps_ref, wpd_ref, wg_ref, bg_ref,
                wo_ref, o_ref):
    d = h_ref.shape[1]
    y_sb = _dot(ysb_ref[...], wps_ref[...])
    y_d = _dot(yd_ref[...], wpd_ref[...])
    gates = jax.nn.sigmoid(_dot(n_ref[...], wg_ref[...]) + bg_ref[...])
    merged = (gates[:, :d] * y_sb + gates[:, d:] * y_d).astype(BF16)
    o_ref[...] = h_ref[...] + mod_ref[5] * _dot(merged, wo_ref[...])


def _mix_call(h, n2, y_sb, y_d, mod, w_proj_sb, w_proj_dsa, w_gate, b_gate, w_out, *, seq, tm=512):
    n_tok, d = h.shape
    tok = lambda w: pl.BlockSpec((tm, w), lambda i: (i, 0))
    ws = [w.astype(BF16) for w in (w_proj_sb, w_proj_dsa, w_gate)]
    bg = b_gate.reshape(1, -1)
    wo = w_out.astype(BF16)
    return pl.pallas_call(
        _mix_kernel,
        out_shape=jax.ShapeDtypeStruct((n_tok, d), F32),
        grid=(n_tok // tm,),
        in_specs=[tok(d), tok(d), tok(W_HEADS), tok(W_HEADS), _mod_spec(seq // tm, d)]
                 + [_const_spec(w.shape) for w in (*ws, bg, wo)],
        out_specs=tok(d),
        compiler_params=_cparams(1),
        name="mix",
    )(h, n2, y_sb, y_d, mod, *ws, bg, wo)


def kernel(x, c, w_ada, b_ada, g_ffn1, w1_ffn1, w3_ffn1, w2_ffn1, g_mix, w_in, g_q_dsa, g_k_dsa,
           w_proj_sb, w_proj_dsa, w_gate, b_gate, w_out, g_ffn2, w1_ffn2, w3_ffn2, w2_ffn2):
    batch, seq, d = x.shape
    h = x.reshape(batch * seq, d)
    for l in range(w_ada.shape[0]):
        bf = lambda w: w[l].astype(BF16)
        mod = _mod_call(c, w_ada[l], b_ada[l])
        h, n2 = _ffn_call(h, mod, g_ffn1[l], bf(w1_ffn1), bf(w3_ffn1), bf(w2_ffn1), g_mix[l],
                          mod_base=0, emit_next=True, seq=seq)
        q_sb, k_sb, v_sb, q_d, kk, q_i, kki, vt, wit = _inproj_call(
            n2, w_in[l], g_q_dsa[l], g_k_dsa[l], batch, seq)
        y_sb = _sb_call(q_sb, k_sb, v_sb, batch, seq).reshape(batch * seq, W_HEADS)
        y_d = _dsa_call(q_d, q_i, wit, kk, vt, kki, batch, seq).reshape(batch * seq, W_HEADS)
        h = _mix_call(h, n2, y_sb, y_d, mod, w_proj_sb[l], w_proj_dsa[l], w_gate[l], b_gate[l],
                      w_out[l], seq=seq)
        (h,) = _ffn_call(h, mod, g_ffn2[l], bf(w1_ffn2), bf(w3_ffn2), bf(w2_ffn2), g_ffn2[l],
                         mod_base=6, emit_next=False, seq=seq)
    return h.reshape(batch, seq, d)
```

```python
import functools

import numpy as np
import jax
import jax.numpy as jnp
from jax import lax
from jax.experimental import pallas as pl
from jax.experimental.pallas import tpu as pltpu

F32 = jnp.float32
BF16 = jnp.bfloat16

HEAD_DIM = 64
N_HEADS = 8
W_HEADS = N_HEADS * HEAD_DIM
N_PAIRS = N_HEADS // 2
LANES = 128
SUBLANES = 8
BF16_SUBLANES = 16
BF16_MAX_EXACT_COUNT = 256
SELECT_WINDOW_BITS = 17
SELECT_EARLY_PASSES = 11
TOPK_MAX = 256
N_MOD = 9
RMS_EPS = 1e-6
D_IDX = 64

TQ = 128
TQ_DSA = 128
TK = 128
CH_S = 512
CH_H = 256
VT_ROWS = 80
LANE_POS_HI, LANE_POS_LO = HEAD_DIM, HEAD_DIM + 1
N_POS_LANES = 2
N_SLOPE_PARTS = 3
LOG2E = float(np.log2(np.e))
NEG_BIG = -1e30
F32_EXP_UNDERFLOW = 104.0
INT_MIN = np.int32(-2**31)
VMEM_LIMIT = 56 * 1024 * 1024


def _cparams(n_axes):
    return pltpu.CompilerParams(
        dimension_semantics=("arbitrary",) * n_axes, vmem_limit_bytes=VMEM_LIMIT)


def _const_spec(shape):
    nd = len(shape)
    return pl.BlockSpec(shape, lambda *_: (0,) * nd, pipeline_mode=pl.Buffered(1))


def _dot(a, b):
    return jnp.dot(a, b, preferred_element_type=F32)


def _dot_nt(a, b):
    return lax.dot_general(a, b, (((1,), (1,)), ((), ())), preferred_element_type=F32)


def _mod_kernel(c_ref, w_ref, b_ref, o_ref, sc_ref):
    nb, kdim, _ = c_ref.shape
    tn = w_ref.shape[1]
    nch = tn // LANES

    @pl.when(pl.program_id(0) == 0)
    def _():
        c = c_ref[...]
        sc_ref[...] = c * jax.nn.sigmoid(c)

    def body(kb, accs):
        k0 = pl.multiple_of(kb * SUBLANES, SUBLANES)
        wblk = w_ref[pl.ds(k0, SUBLANES), :]
        out = []
        for b in range(nb):
            sb = sc_ref[b, pl.ds(k0, SUBLANES), :]
            for ch in range(nch):
                out.append(accs[b * nch + ch] + wblk[:, ch * LANES:(ch + 1) * LANES] * sb)
        return tuple(out)

    accs = lax.fori_loop(0, kdim // SUBLANES, body,
                         tuple(jnp.zeros((SUBLANES, LANES), F32) for _ in range(nb * nch)))
    for b in range(nb):
        row = jnp.concatenate(
            [jnp.sum(accs[b * nch + ch], axis=0, keepdims=True) for ch in range(nch)], axis=1)
        o_ref[0, b] = row + b_ref[...]


def _mod_call(c, w_ada, b_ada):
    nb, d = c.shape
    n_out = w_ada.shape[1]
    c_rep = jnp.broadcast_to(c[:, :, None], (nb, d, LANES))
    return pl.pallas_call(
        _mod_kernel,
        out_shape=jax.ShapeDtypeStruct((n_out // d, nb, 1, d), F32),
        grid=(n_out // d,),
        in_specs=[_const_spec((nb, d, LANES)),
                  pl.BlockSpec((d, d), lambda j: (0, j)),
                  pl.BlockSpec((1, d), lambda j: (0, j))],
        out_specs=pl.BlockSpec((1, nb, 1, d), lambda j: (j, 0, 0, 0)),
        scratch_shapes=[pltpu.VMEM((nb, d, LANES), F32)],
        compiler_params=_cparams(1),
        name="mod",
    )(c_rep, w_ada, b_ada.reshape(1, n_out))


def _norm_mod(h, g, shift, scale):
    y = h * lax.rsqrt(jnp.mean(h * h, axis=-1, keepdims=True) + RMS_EPS)
    return (y * g) * (1.0 + scale) + shift


def _ffn_kernel(h_ref, mod_ref, g_ref, w1_ref, w3_ref, w2_ref, gn_ref, *out_refs,
                mod_base, emit_next, n_chunks):
    o_ref = out_refs[0]
    h = h_ref[...]
    n = _norm_mod(h, g_ref[...], mod_ref[mod_base], mod_ref[mod_base + 1]).astype(BF16)
    fc = w1_ref.shape[1] // n_chunks
    acc = jnp.zeros(h.shape, F32)
    for c in range(n_chunks):
        a = _dot(n, w1_ref[:, c * fc:(c + 1) * fc])
        b = _dot(n, w3_ref[:, c * fc:(c + 1) * fc])
        act = (a * jax.nn.sigmoid(a) * b).astype(BF16)
        acc = acc + _dot(act, w2_ref[c * fc:(c + 1) * fc, :])
    hn = h + (0.5 * mod_ref[mod_base + 2]) * acc
    o_ref[...] = hn
    if emit_next:
        out_refs[1][...] = _norm_mod(hn, gn_ref[...], mod_ref[mod_base + 3],
                                     mod_ref[mod_base + 4]).astype(BF16)


def _mod_spec(tiles_per_batch, d):
    return pl.BlockSpec((N_MOD, None, 1, d), lambda i: (0, i // tiles_per_batch, 0, 0))


def _ffn_call(h, mod, g, w1, w3, w2, g_next, *, mod_base, emit_next, seq, tm=512):
    n_tok, d = h.shape
    f = w1.shape[1]
    tok = pl.BlockSpec((tm, d), lambda i: (i, 0))
    out_shape = [jax.ShapeDtypeStruct((n_tok, d), F32)]
    out_specs = [tok]
    if emit_next:
        out_shape.append(jax.ShapeDtypeStruct((n_tok, d), BF16))
        out_specs.append(tok)
    return pl.pallas_call(
        functools.partial(_ffn_kernel, mod_base=mod_base, emit_next=emit_next, n_chunks=2),
        out_shape=out_shape,
        grid=(n_tok // tm,),
        in_specs=[tok, _mod_spec(seq // tm, d), _const_spec((1, d)),
                  _const_spec((d, f)), _const_spec((d, f)), _const_spec((f, d)),
                  _const_spec((1, d))],
        out_specs=out_specs,
        compiler_params=_cparams(1),
        name="ffn",
    )(h, mod, g.reshape(1, d), w1, w3, w2, g_next.reshape(1, d))


def _alibi_slope_parts():
    out = []
    for h in range(N_HEADS):
        rest, parts = np.float32(2.0 ** (-8.0 * (h + 1) / N_HEADS) * LOG2E), []
        for _ in range(N_SLOPE_PARTS):
            parts.append(np.float32(np.asarray(rest, dtype=BF16)))
            rest = np.float32(rest - parts[-1])
        out.append(parts)
    return out


def _store_head_rows(o_ref, x, slope_parts):
    tm = x.shape[0]
    lane = lax.broadcasted_iota(jnp.int32, (tm, LANES), 1)
    for h in range(N_HEADS):
        rows = x[:, (h // 2) * LANES:(h // 2 + 1) * LANES]
        if h % 2:
            rows = pltpu.roll(rows, HEAD_DIM, axis=1)
        rows = jnp.where(lane < HEAD_DIM, rows, 0.0)
        if slope_parts is not None:
            for rep, part in enumerate(slope_parts[h]):
                rows = jnp.where(lane == LANE_POS_HI + rep * N_POS_LANES, part * HEAD_DIM, rows)
                rows = jnp.where(lane == LANE_POS_LO + rep * N_POS_LANES, part, rows)
        o_ref[:, h] = rows.astype(BF16).reshape(tm // TQ_DSA, TQ_DSA, LANES)


def _inproj_kernel(n_ref, wsb_ref, wqd_ref, wkd_ref, wqi_ref, wki_ref, wvt_ref, wwt_ref,
                   gq_ref, gk_ref, hm_ref,
                   qsb_ref, ksb_ref, vsb_ref, qd_ref, kk_ref, qi_ref, kki_ref, vt_ref, wit_ref,
                   *, tiles_per_batch):
    n = n_ref[...]
    tm = n.shape[0]
    qk_scale = HEAD_DIM ** -0.5
    sb = _dot(n, wsb_ref[...])
    qsb_ref[...] = (sb[:, :W_HEADS] * qk_scale).astype(BF16)
    ksb_ref[...] = sb[:, W_HEADS:2 * W_HEADS].astype(BF16)
    vsb_ref[...] = sb[:, 2 * W_HEADS:].astype(BF16)

    qd = _dot(n, wqd_ref[...])
    ms = _dot((qd * qd).astype(BF16), hm_ref[...])
    _store_head_rows(qd_ref, qd * lax.rsqrt(ms + RMS_EPS) * (gq_ref[...] * (qk_scale * LOG2E)),
                     _alibi_slope_parts())

    kd = _dot(n, wkd_ref[...])
    kn = kd * lax.rsqrt(jnp.mean(kd * kd, axis=-1, keepdims=True) + RMS_EPS) * gk_ref[...]
    lane = lax.broadcasted_iota(jnp.int32, (tm, LANES), 1)
    pos = ((pl.program_id(0) % tiles_per_batch) * tm
           + lax.broadcasted_iota(jnp.int32, (tm, LANES), 0))
    feat = jnp.zeros((tm, LANES), jnp.int32)
    for rep in range(N_SLOPE_PARTS):
        feat = jnp.where(lane == LANE_POS_HI + rep * N_POS_LANES,
                         pos >> (HEAD_DIM.bit_length() - 1), feat)
        feat = jnp.where(lane == LANE_POS_LO + rep * N_POS_LANES, pos & (HEAD_DIM - 1), feat)
    feat = feat.astype(F32)
    kk_ref[...] = jnp.where(lane < HEAD_DIM, kn, feat).astype(BF16)

    _store_head_rows(qi_ref, _dot(n, wqi_ref[...]), None)
    kki_ref[...] = _dot(n, wki_ref[...]).astype(BF16)
    vt = _dot_nt(wvt_ref[...], n)
    row = lax.broadcasted_iota(jnp.int32, vt.shape, 0)
    vt_ref[...] = jnp.where(row == HEAD_DIM, 1.0, vt).astype(BF16)
    wit_ref[...] = _dot_nt(wwt_ref[...], n)[:N_HEADS]


def _inproj_call(n2, w_in, g_q, g_k, batch, seq, tm=512):
    n_tok, d = n2.shape
    sizes = (W_HEADS, W_HEADS, W_HEADS, W_HEADS, HEAD_DIM, HEAD_DIM, N_HEADS * D_IDX, D_IDX, N_HEADS)
    offs = np.concatenate([[0], np.cumsum(sizes)])
    col = lambda k: w_in[:, offs[k]:offs[k + 1]]
    idx_scale = (D_IDX ** -0.5) * (N_HEADS ** -0.5)
    w_sb = jnp.concatenate([col(0), col(1), col(2)], axis=1).astype(BF16)
    w_qd = col(3).astype(BF16)
    w_kd = jnp.concatenate([col(4), col(4)], axis=1).astype(BF16)
    w_qi = col(6).astype(BF16)
    w_ki = jnp.pad(col(7), ((0, 0), (0, LANES - D_IDX))).astype(BF16)
    w_vt = jnp.pad(col(5).T, ((0, VT_ROWS - HEAD_DIM), (0, 0))).astype(BF16)
    w_wt = jnp.pad((col(8) * idx_scale).T, ((0, 16 - N_HEADS), (0, 0))).astype(BF16)
    gq = jnp.tile(g_q, N_HEADS).reshape(1, W_HEADS)
    gk = jnp.tile(g_k, 2).reshape(1, LANES)
    head_of = np.arange(W_HEADS) // HEAD_DIM
    head_mean = jnp.asarray((head_of[:, None] == head_of[None, :]) / HEAD_DIM, BF16)

    tpb = seq // tm
    tok = lambda w: pl.BlockSpec((tm, w), lambda i: (i, 0))
    tr = lambda r: pl.BlockSpec((None, r, tm), lambda i: (i // tpb, 0, i % tpb))
    sd = lambda w, dt: jax.ShapeDtypeStruct((n_tok, w), dt)
    consts = (w_sb, w_qd, w_kd, w_qi, w_ki, w_vt, w_wt, gq, gk, head_mean)
    head_rows = jax.ShapeDtypeStruct((n_tok // TQ_DSA, N_HEADS, TQ_DSA, LANES), BF16)
    head_rows_spec = pl.BlockSpec((tm // TQ_DSA, N_HEADS, TQ_DSA, LANES), lambda i: (i, 0, 0, 0))
    return pl.pallas_call(
        functools.partial(_inproj_kernel, tiles_per_batch=tpb),
        out_shape=[sd(W_HEADS, BF16)] * 3 + [head_rows, sd(LANES, BF16), head_rows, sd(LANES, BF16),
                                             jax.ShapeDtypeStruct((batch, VT_ROWS, seq), BF16),
                                             jax.ShapeDtypeStruct((batch, N_HEADS, seq), F32)],
        grid=(n_tok // tm,),
        in_specs=[tok(d)] + [_const_spec(w.shape) for w in consts],
        out_specs=[tok(W_HEADS)] * 3 + [head_rows_spec, tok(LANES), head_rows_spec, tok(LANES),
                                        tr(VT_ROWS), tr(N_HEADS)],
        compiler_params=_cparams(1),
        name="inproj",
    )(n2, *consts)


def _split_heads(x, lane):
    xf = x.astype(F32)
    lo = jnp.where(lane < HEAD_DIM, xf, 0.0).astype(x.dtype)
    hi = jnp.where(lane >= HEAD_DIM, xf, 0.0).astype(x.dtype)
    return lo, hi


def _sb_kernel(q_ref, k_ref, v_ref, t_ref, o_ref, qs_ref, run_ref, acc_ref):
    i = pl.program_id(1)
    lane = lax.broadcasted_iota(jnp.int32, (TQ, LANES), 1)
    lane_k = lax.broadcasted_iota(jnp.int32, (TK, LANES), 1)
    for p in range(N_PAIRS):
        lo, hi = _split_heads(q_ref[:, p * LANES:(p + 1) * LANES], lane)
        qs_ref[p, :TQ, :] = lo
        qs_ref[p, TQ:, :] = hi

    def block(k0, diag):
        pairs = range(N_PAIRS)
        strict = (lax.broadcasted_iota(jnp.int32, (2 * TQ, TK), 1)
                  < lax.broadcasted_iota(jnp.int32, (2 * TQ, TK), 0) % TQ) if diag else None
        zs = [_dot_nt(qs_ref[p], k_ref[pl.ds(k0, TK), p * LANES:(p + 1) * LANES]) for p in pairs]
        lbs, ws = [], []
        for p in pairs:
            z = zs[p]
            sp = jnp.maximum(z, 0.0) + jnp.log(1.0 + jnp.exp(-jnp.abs(z)))
            l1m = jnp.where(strict, sp, 0.0) if diag else sp
            l_hi = l1m.astype(BF16)
            l_lo = (l1m - l_hi.astype(F32)).astype(BF16)
            ws.append(_dot(jnp.concatenate([l_hi, l_lo], axis=1), t_ref[...]))
            lbs.append(z - sp)
        for p in pairs:
            if diag:
                a = jnp.where(strict, jnp.exp(lbs[p] + ws[p][:, :TK]), 0.0).astype(BF16)
            else:
                a = jnp.exp(lbs[p] + ws[p][:, :TK] + run_ref[p]).astype(BF16)
            v_lo, v_hi = _split_heads(v_ref[pl.ds(k0, TK), p * LANES:(p + 1) * LANES], lane_k)
            upd = _dot(jnp.concatenate([a[:TQ], a[TQ:]], axis=1),
                       jnp.concatenate([v_lo, v_hi], axis=0))
            if diag:
                acc_ref[p] = upd
                run_ref[p] = ws[p][:, TK:]
            else:
                acc_ref[p] += upd
                run_ref[p] += ws[p][:, TK:]

    def more_blocks(jj):
        run_max = functools.reduce(jnp.maximum, [jnp.max(run_ref[p]) for p in range(N_PAIRS)])
        return jnp.logical_and(jj <= i, run_max > -F32_EXP_UNDERFLOW)

    block(pl.multiple_of(i * TK, TK), True)

    def body(state):
        jj, _ = state
        block(pl.multiple_of((i - jj) * TK, TK), False)
        return jj + 1, more_blocks(jj + 1)

    lax.while_loop(lambda state: state[1], body, (jnp.int32(1), more_blocks(jnp.int32(1))))
    for p in range(N_PAIRS):
        o_ref[:, p * LANES:(p + 1) * LANES] = acc_ref[p].astype(o_ref.dtype)


def _sb_call(q, k, v, batch, seq):
    q, k, v = (t.reshape(batch, seq, W_HEADS) for t in (q, k, v))
    j_idx = np.arange(TK)
    suffix = (j_idx[:, None] > j_idx[None, :]).astype(np.float32)
    tmat = -np.concatenate([suffix, np.ones((TK, LANES), np.float32)], axis=1)
    tmat = jnp.asarray(np.concatenate([tmat, tmat], axis=0), BF16)
    blk = pl.BlockSpec((None, TQ, W_HEADS), lambda b, i: (b, i, 0))
    allk = pl.BlockSpec((None, seq, W_HEADS), lambda b, i: (b, 0, 0))
    return pl.pallas_call(
        _sb_kernel,
        out_shape=jax.ShapeDtypeStruct((batch, seq, W_HEADS), BF16),
        grid=(batch, seq // TQ),
        in_specs=[blk, allk, allk, _const_spec(tmat.shape)],
        out_specs=blk,
        scratch_shapes=[pltpu.VMEM((N_PAIRS, 2 * TQ, LANES), BF16),
                        pltpu.VMEM((N_PAIRS, 2 * TQ, LANES), F32),
                        pltpu.VMEM((N_PAIRS, TQ, LANES), F32)],
        compiler_params=_cparams(2),
        name="sb",
    )(q, k, v, tmat)


def _dsa_kernel(qd_st, qi_st, wit_ref, kk_ref, vt_ref, kki_ref, o_ref,
                sc_ref, top_ref, m_ref, acc_ref, buf_ref, *, topk):
    TQ = qd_st.shape[1]
    i = pl.program_id(1)
    t0 = i * TQ

    n_s = (i + CH_S // TQ) // (CH_S // TQ)
    n_h = n_s * (CH_S // CH_H)

    def products(q_st, k_ref):
        def produce(k0):
            kch = k_ref[pl.ds(k0, CH_H), :]
            return [_dot_nt(kch, q_st[2 * p:2 * p + 2].reshape(2 * TQ, LANES)) for p in range(N_PAIRS)]
        return produce

    def store_products(vals):
        for p in range(N_PAIRS):
            buf_ref[2 * p] = vals[p][:, :TQ]
            buf_ref[2 * p + 1] = vals[p][:, TQ:]

    def sweep(produce, consume):
        from_buf = lambda h: buf_ref[h]

        def trip(c, prefetch):
            k_a = pl.multiple_of(c * CH_S, CH_S)
            k_b = pl.multiple_of(k_a + CH_H, CH_H)
            vals_b = produce(k_b)
            consume(from_buf, k_a)
            if prefetch:
                vals_next = produce(pl.multiple_of(k_a + CH_S, CH_S))
            consume(lambda h: vals_b[h // 2][:, (h % 2) * TQ:(h % 2 + 1) * TQ], k_b)
            if prefetch:
                store_products(vals_next)

        store_products(produce(0))

        def body(c, _):
            trip(c, True)
            return 0

        lax.fori_loop(0, n_s - 1, body, 0)
        trip(n_s - 1, False)

    wit = wit_ref[...]

    def score_consume(prod, k0):
        score = None
        for h in range(N_HEADS):
            term = jnp.maximum(prod(h), 0.0) * wit[h:h + 1, :]
            score = term if score is None else score + term
        causal = (k0 + lax.broadcasted_iota(jnp.int32, (CH_H, TQ), 0)
                  <= t0 + lax.broadcasted_iota(jnp.int32, (CH_H, TQ), 1))
        score = jnp.where(causal, score, -jnp.inf)
        sc_ref[pl.ds(k0, CH_H), :] = score
        top_ref[pl.ds(k0, CH_H), :] = score.astype(BF16)

    sweep(products(qi_st, kki_ref), score_consume)

    def float_of_rank(u):
        key = u ^ INT_MIN
        return pltpu.bitcast(jnp.where(key < 0, key ^ np.int32(0x7FFFFFFF), key), F32)

    def count_top_ge(cand):
        one, zero = jnp.ones((BF16_SUBLANES, TQ), BF16), jnp.zeros((BF16_SUBLANES, TQ), BF16)

        def cbody(c, acc):
            k0 = pl.multiple_of(c * CH_S, CH_S)
            tops = top_ref[pl.ds(k0, CH_S), :]
            parts = [jnp.where(tops[j * BF16_SUBLANES:(j + 1) * BF16_SUBLANES] >= cand, one, zero)
                     for j in range(CH_S // BF16_SUBLANES)]
            while len(parts) > 1:
                parts = [a + b for a, b in zip(parts[::2], parts[1::2])]
            return acc + parts[0]

        acc = lax.fori_loop(0, n_s, cbody, zero)
        return jnp.sum(acc.astype(F32), axis=0, keepdims=True)

    def top_body(b, u16):
        cand = u16 | jnp.left_shift(jnp.int32(1), 15 - b)
        cand_f = float_of_rank(jnp.left_shift(cand, 16)).astype(BF16)
        return jnp.where(count_top_ge(jnp.broadcast_to(cand_f, (BF16_SUBLANES, TQ))) >= topk,
                         cand, u16)

    u16 = lax.fori_loop(0, 16, top_body, jnp.zeros((1, TQ), jnp.int32))

    def count(pred):
        def cbody(c, acc):
            k0 = pl.multiple_of(c * CH_S, CH_S)
            sch = sc_ref[pl.ds(k0, CH_S), :].reshape(CH_S // SUBLANES, SUBLANES, TQ)
            return acc + jnp.sum(jnp.where(pred(sch), 1, 0), axis=0)
        acc = lax.fori_loop(0, n_s, cbody, jnp.zeros((SUBLANES, TQ), jnp.int32))
        tot = jnp.sum(acc.astype(F32), axis=0, keepdims=True)
        return jnp.broadcast_to(tot, (SUBLANES, TQ))

    base = jnp.broadcast_to(jnp.left_shift(u16, 16) - (1 << 15), (SUBLANES, TQ))

    def window_body(b, state):
        off, n_off = state
        cand = off | jnp.left_shift(jnp.int32(1), SELECT_WINDOW_BITS - 1 - b)
        cand_f = float_of_rank(base + cand)
        n_cand = count(lambda sch: sch >= cand_f[None])
        take = n_cand >= topk
        return jnp.where(take, cand, off), jnp.where(take, n_cand, n_off)

    few_keys = t0 + lax.broadcasted_iota(jnp.int32, (SUBLANES, TQ), 1) < topk
    unknown = jnp.full((SUBLANES, TQ), -1.0, F32)
    state = lax.fori_loop(0, SELECT_EARLY_PASSES, window_body,
                          (jnp.zeros((SUBLANES, TQ), jnp.int32), unknown))
    still_open = jnp.max(jnp.where(jnp.logical_or(state[1] == topk, few_keys), 0.0, 1.0))
    offset, n_off = lax.cond(
        still_open > 0.0,
        lambda st: lax.fori_loop(SELECT_EARLY_PASSES, SELECT_WINDOW_BITS, window_body, st),
        lambda st: st, state)
    thr = jnp.where(few_keys, jnp.finfo(F32).min, float_of_rank(base + offset))
    thr_row = thr[0:1, :]

    n_ge = jnp.where(few_keys, 0.0, jnp.where(n_off < 0.0, topk + 1.0, n_off))

    @pl.when(jnp.max(n_ge) > topk)
    def _():
        need = topk - count(lambda sch: sch > thr[None])[0:1, :]
        before = jnp.where(lax.broadcasted_iota(jnp.int32, (CH_H, CH_H), 1)
                           < lax.broadcasted_iota(jnp.int32, (CH_H, CH_H), 0), 1.0, 0.0).astype(BF16)

        def tie_body(c, seen):
            k0 = pl.multiple_of(c * CH_H, CH_H)
            sch = sc_ref[pl.ds(k0, CH_H), :]
            eq = jnp.where(sch == thr_row, 1.0, 0.0)
            rank = _dot(before, eq.astype(BF16)) + seen
            tie_rank = jnp.where(sch == thr_row, rank, -1.0)
            sc_ref[pl.ds(k0, CH_H), :] = jnp.where(tie_rank >= need, -jnp.inf, sch)
            return seen + jnp.sum(eq, axis=0, keepdims=True)

        lax.fori_loop(0, n_h, tie_body, jnp.zeros((1, TQ), F32))

    m_ref[...] = jnp.full(m_ref.shape, NEG_BIG, F32)
    acc_ref[...] = jnp.zeros(acc_ref.shape, F32)

    def attn_consume(prod, k0):
        vch = vt_ref[:, pl.ds(k0, CH_H)]
        sel = sc_ref[pl.ds(k0, CH_H), :] >= thr_row
        for p in range(N_PAIRS):
            probs, alphas = [], []
            for h in (2 * p, 2 * p + 1):
                lg = jnp.where(sel, prod(h), NEG_BIG)
                m_old = m_ref[h]
                m_new = jnp.maximum(m_old, jnp.max(lg, axis=0, keepdims=True))
                m_ref[h] = m_new
                alphas.append(jnp.exp2(m_old - m_new))
                probs.append(jnp.exp2(lg - m_new).astype(BF16))
            upd = _dot(vch, jnp.concatenate(probs, axis=1))
            acc_ref[p] = acc_ref[p] * jnp.concatenate(alphas, axis=1) + upd

    sweep(products(qd_st, kk_ref), attn_consume)
    for p in range(N_PAIRS):
        acc = acc_ref[p]
        out_t = acc[:HEAD_DIM] / acc[HEAD_DIM:HEAD_DIM + 1]
        pair = jnp.concatenate([out_t[:, :TQ], out_t[:, TQ:]], axis=0)
        o_ref[:, p * LANES:(p + 1) * LANES] = pair.T.astype(o_ref.dtype)


def _dsa_call(qd_rows, qi_rows, wit, kk, vt, kki, batch, seq):
    r3 = lambda t: t.reshape(batch, seq, t.shape[-1])
    kk, kki = map(r3, (kk, kki))
    topk = min(TOPK_MAX, seq // 4)
    assert seq // BF16_SUBLANES <= BF16_MAX_EXACT_COUNT
    TQ = TQ_DSA
    nq = seq // TQ
    rows = pl.BlockSpec((None, N_HEADS, TQ, LANES), lambda b, i: (b * nq + i, 0, 0, 0))
    allk = pl.BlockSpec((None, seq, LANES), lambda b, i: (b, 0, 0))
    return pl.pallas_call(
        functools.partial(_dsa_kernel, topk=topk),
        out_shape=jax.ShapeDtypeStruct((batch, seq, W_HEADS), BF16),
        grid=(batch, nq),
        in_specs=[rows, rows,
                  pl.BlockSpec((None, N_HEADS, TQ), lambda b, i: (b, 0, i)),
                  allk, pl.BlockSpec((None, VT_ROWS, seq), lambda b, i: (b, 0, 0)), allk],
        out_specs=pl.BlockSpec((None, TQ, W_HEADS), lambda b, i: (b, i, 0)),
        scratch_shapes=[pltpu.VMEM((seq, TQ), F32),
                        pltpu.VMEM((seq, TQ), BF16),
                        pltpu.VMEM((N_HEADS, 1, TQ), F32),
                        pltpu.VMEM((N_PAIRS, VT_ROWS, 2 * TQ), F32),
                        pltpu.VMEM((N_HEADS, CH_H, TQ), F32)],
        compiler_params=_cparams(2),
        name="dsa",
    )(qd_rows, qi_rows, wit, kk, vt, kki)


def _mix_kernel(h_ref, n_ref, ysb_ref, yd_ref, mod_ref, wps_ref, wpd_ref, wg_ref, bg_ref,
                wo_ref, o_ref):
    d = h_ref.shape[1]
    y_sb = _dot(ysb_ref[...], wps_ref[...])
    y_d = _dot(yd_ref[...], wpd_ref[...])
    gates = jax.nn.sigmoid(_dot(n_ref[...], wg_ref[...]) + bg_ref[...])
    merged = (gates[:, :d] * y_sb + gates[:, d:] * y_d).astype(BF16)
    o_ref[...] = h_ref[...] + mod_ref[5] * _dot(merged, wo_ref[...])


def _mix_call(h, n2, y_sb, y_d, mod, w_proj_sb, w_proj_dsa, w_gate, b_gate, w_out, *, seq, tm=512):
    n_tok, d = h.shape
    tok = lambda w: pl.BlockSpec((tm, w), lambda i: (i, 0))
    ws = [w.astype(BF16) for w in (w_proj_sb, w_proj_dsa, w_gate)]
    bg = b_gate.reshape(1, -1)
    wo = w_out.astype(BF16)
    return pl.pallas_call(
        _mix_kernel,
        out_shape=jax.ShapeDtypeStruct((n_tok, d), F32),
        grid=(n_tok // tm,),
        in_specs=[tok(d), tok(d), tok(W_HEADS), tok(W_HEADS), _mod_spec(seq // tm, d)]
                 + [_const_spec(w.shape) for w in (*ws, bg, wo)],
        out_specs=tok(d),
        compiler_params=_cparams(1),
        name="mix",
    )(h, n2, y_sb, y_d, mod, *ws, bg, wo)


def kernel(x, c, w_ada, b_ada, g_ffn1, w1_ffn1, w3_ffn1, w2_ffn1, g_mix, w_in, g_q_dsa, g_k_dsa,
           w_proj_sb, w_proj_dsa, w_gate, b_gate, w_out, g_ffn2, w1_ffn2, w3_ffn2, w2_ffn2):
    batch, seq, d = x.shape
    h = x.reshape(batch * seq, d)
    for l in range(w_ada.shape[0]):
        bf = lambda w: w[l].astype(BF16)
        mod = _mod_call(c, w_ada[l], b_ada[l])
        h, n2 = _ffn_call(h, mod, g_ffn1[l], bf(w1_ffn1), bf(w3_ffn1), bf(w2_ffn1), g_mix[l],
                          mod_base=0, emit_next=True, seq=seq)
        q_sb, k_sb, v_sb, q_d, kk, q_i, kki, vt, wit = _inproj_call(
            n2, w_in[l], g_q_dsa[l], g_k_dsa[l], batch, seq)
        y_sb = _sb_call(q_sb, k_sb, v_sb, batch, seq).reshape(batch * seq, W_HEADS)
        y_d = _dsa_call(q_d, q_i, wit, kk, vt, kki, batch, seq).reshape(batch * seq, W_HEADS)
        h = _mix_call(h, n2, y_sb, y_d, mod, w_proj_sb[l], w_proj_dsa[l], w_gate[l], b_gate[l],
                      w_out[l], seq=seq)
        (h,) = _ffn_call(h, mod, g_ffn2[l], bf(w1_ffn2), bf(w3_ffn2), bf(w2_ffn2), g_ffn2[l],
                         mod_base=6, emit_next=False, seq=seq)
    return h.reshape(batch, seq, d)
```

```python
import functools

import numpy as np
import jax
import jax.numpy as jnp
from jax import lax
from jax.experimental import pallas as pl
from jax.experimental.pallas import tpu as pltpu

F32 = jnp.float32
BF16 = jnp.bfloat16

HEAD_DIM = 64
N_HEADS = 8
W_HEADS = N_HEADS * HEAD_DIM
N_PAIRS = N_HEADS // 2
LANES = 128
SUBLANES = 8
BF16_SUBLANES = 16
BF16_MAX_EXACT_COUNT = 256
SELECT_WINDOW_BITS = 17
SELECT_EARLY_PASSES = 11
TOPK_MAX = 256
N_MOD = 9
RMS_EPS = 1e-6
D_IDX = 64

TQ = 128
TQ_DSA = 128
TK = 128
CH_S = 512
CH_H = 256
VT_ROWS = 80
LANE_POS_HI, LANE_POS_LO = HEAD_DIM, HEAD_DIM + 1
N_POS_LANES = 2
N_SLOPE_PARTS = 3
LOG2E = float(np.log2(np.e))
NEG_BIG = -1e30
F32_EXP_UNDERFLOW = 104.0
INT_MIN = np.int32(-2**31)
VMEM_LIMIT = 56 * 1024 * 1024


def _cparams(n_axes):
    return pltpu.CompilerParams(
        dimension_semantics=("arbitrary",) * n_axes, vmem_limit_bytes=VMEM_LIMIT)


def _const_spec(shape):
    nd = len(shape)
    return pl.BlockSpec(shape, lambda *_: (0,) * nd, pipeline_mode=pl.Buffered(1))


def _dot(a, b):
    return jnp.dot(a, b, preferred_element_type=F32)


def _dot_nt(a, b):
    return lax.dot_general(a, b, (((1,), (1,)), ((), ())), preferred_element_type=F32)


def _mod_kernel(c_ref, w_ref, b_ref, o_ref, sc_ref):
    nb, kdim, _ = c_ref.shape
    tn = w_ref.shape[1]
    nch = tn // LANES

    @pl.when(pl.program_id(0) == 0)
    def _():
        c = c_ref[...]
        sc_ref[...] = c * jax.nn.sigmoid(c)

    def body(kb, accs):
        k0 = pl.multiple_of(kb * SUBLANES, SUBLANES)
        wblk = w_ref[pl.ds(k0, SUBLANES), :]
        out = []
        for b in range(nb):
            sb = sc_ref[b, pl.ds(k0, SUBLANES), :]
            for ch in range(nch):
                out.append(accs[b * nch + ch] + wblk[:, ch * LANES:(ch + 1) * LANES] * sb)
        return tuple(out)

    accs = lax.fori_loop(0, kdim // SUBLANES, body,
                         tuple(jnp.zeros((SUBLANES, LANES), F32) for _ in range(nb * nch)))
    for b in range(nb):
        row = jnp.concatenate(
            [jnp.sum(accs[b * nch + ch], axis=0, keepdims=True) for ch in range(nch)], axis=1)
        o_ref[0, b] = row + b_ref[...]


def _mod_call(c, w_ada, b_ada):
    nb, d = c.shape
    n_out = w_ada.shape[1]
    c_rep = jnp.broadcast_to(c[:, :, None], (nb, d, LANES))
    return pl.pallas_call(
        _mod_kernel,
        out_shape=jax.ShapeDtypeStruct((n_out // d, nb, 1, d), F32),
        grid=(n_out // d,),
        in_specs=[_const_spec((nb, d, LANES)),
                  pl.BlockSpec((d, d), lambda j: (0, j)),
                  pl.BlockSpec((1, d), lambda j: (0, j))],
        out_specs=pl.BlockSpec((1, nb, 1, d), lambda j: (j, 0, 0, 0)),
        scratch_shapes=[pltpu.VMEM((nb, d, LANES), F32)],
        compiler_params=_cparams(1),
        name="mod",
    )(c_rep, w_ada, b_ada.reshape(1, n_out))


def _norm_mod(h, g, shift, scale):
    y = h * lax.rsqrt(jnp.mean(h * h, axis=-1, keepdims=True) + RMS_EPS)
    return (y * g) * (1.0 + scale) + shift


def _ffn_kernel(h_ref, mod_ref, g_ref, w1_ref, w3_ref, w2_ref, gn_ref, *out_refs,
                mod_base, emit_next, n_chunks):
    o_ref = out_refs[0]
    h = h_ref[...]
    n = _norm_mod(h, g_ref[...], mod_ref[mod_base], mod_ref[mod_base + 1]).astype(BF16)
    fc = w1_ref.shape[1] // n_chunks
    acc = jnp.zeros(h.shape, F32)
    for c in range(n_chunks):
        a = _dot(n, w1_ref[:, c * fc:(c + 1) * fc])
        b = _dot(n, w3_ref[:, c * fc:(c + 1) * fc])
        act = (a * jax.nn.sigmoid(a) * b).astype(BF16)
        acc = acc + _dot(act, w2_ref[c * fc:(c + 1) * fc, :])
    hn = h + (0.5 * mod_ref[mod_base + 2]) * acc
    o_ref[...] = hn
    if emit_next:
        out_refs[1][...] = _norm_mod(hn, gn_ref[...], mod_ref[mod_base + 3],
                                     mod_ref[mod_base + 4]).astype(BF16)


def _mod_spec(tiles_per_batch, d):
    return pl.BlockSpec((N_MOD, None, 1, d), lambda i: (0, i // tiles_per_batch, 0, 0))


def _ffn_call(h, mod, g, w1, w3, w2, g_next, *, mod_base, emit_next, seq, tm=512):
    n_tok, d = h.shape
    f = w1.shape[1]
    tok = pl.BlockSpec((tm, d), lambda i: (i, 0))
    out_shape = [jax.ShapeDtypeStruct((n_tok, d), F32)]
    out_specs = [tok]
    if emit_next:
        out_shape.append(jax.ShapeDtypeStruct((n_tok, d), BF16))
        out_specs.append(tok)
    return pl.pallas_call(
        functools.partial(_ffn_kernel, mod_base=mod_base, emit_next=emit_next, n_chunks=2),
        out_shape=out_shape,
        grid=(n_tok // tm,),
        in_specs=[tok, _mod_spec(seq // tm, d), _const_spec((1, d)),
                  _const_spec((d, f)), _const_spec((d, f)), _const_spec((f, d)),
                  _const_spec((1, d))],
        out_specs=out_specs,
        compiler_params=_cparams(1),
        name="ffn",
    )(h, mod, g.reshape(1, d), w1, w3, w2, g_next.reshape(1, d))


def _alibi_slope_parts():
    out = []
    for h in range(N_HEADS):
        rest, parts = np.float32(2.0 ** (-8.0 * (h + 1) / N_HEADS) * LOG2E), []
        for _ in range(N_SLOPE_PARTS):
            parts.append(np.float32(np.asarray(rest, dtype=BF16)))
            rest = np.float32(rest - parts[-1])
        out.append(parts)
    return out


def _store_head_rows(o_ref, x, slope_parts):
    tm = x.shape[0]
    lane = lax.broadcasted_iota(jnp.int32, (tm, LANES), 1)
    for h in range(N_HEADS):
        rows = x[:, (h // 2) * LANES:(h // 2 + 1) * LANES]
        if h % 2:
            rows = pltpu.roll(rows, HEAD_DIM, axis=1)
        rows = jnp.where(lane < HEAD_DIM, rows, 0.0)
        if slope_parts is not None:
            for rep, part in enumerate(slope_parts[h]):
                rows = jnp.where(lane == LANE_POS_HI + rep * N_POS_LANES, part * HEAD_DIM, rows)
                rows = jnp.where(lane == LANE_POS_LO + rep * N_POS_LANES, part, rows)
        o_ref[:, h] = rows.astype(BF16).reshape(tm // TQ_DSA, TQ_DSA, LANES)


def _inproj_kernel(n_ref, wsb_ref, wqd_ref, wkd_ref, wqi_ref, wki_ref, wvt_ref, wwt_ref,
                   gq_ref, gk_ref, hm_ref,
                   qsb_ref, ksb_ref, vsb_ref, qd_ref, kk_ref, qi_ref, kki_ref, vt_ref, wit_ref,
                   *, tiles_per_batch):
    n = n_ref[...]
    tm = n.shape[0]
    qk_scale = HEAD_DIM ** -0.5
    sb = _dot(n, wsb_ref[...])
    qsb_ref[...] = (sb[:, :W_HEADS] * qk_scale).astype(BF16)
    ksb_ref[...] = sb[:, W_HEADS:2 * W_HEADS].astype(BF16)
    vsb_ref[...] = sb[:, 2 * W_HEADS:].astype(BF16)

    qd = _dot(n, wqd_ref[...])
    ms = _dot((qd * qd).astype(BF16), hm_ref[...])
    _store_head_rows(qd_ref, qd * lax.rsqrt(ms + RMS_EPS) * (gq_ref[...] * (qk_scale * LOG2E)),
                     _alibi_slope_parts())

    kd = _dot(n, wkd_ref[...])
    kn = kd * lax.rsqrt(jnp.mean(kd * kd, axis=-1, keepdims=True) + RMS_EPS) * gk_ref[...]
    lane = lax.broadcasted_iota(jnp.int32, (tm, LANES), 1)
    pos = ((pl.program_id(0) % tiles_per_batch) * tm
           + lax.broadcasted_iota(jnp.int32, (tm, LANES), 0))
    feat = jnp.zeros((tm, LANES), jnp.int32)
    for rep in range(N_SLOPE_PARTS):
        feat = jnp.where(lane == LANE_POS_HI + rep * N_POS_LANES,
                         pos >> (HEAD_DIM.bit_length() - 1), feat)
        feat = jnp.where(lane == LANE_POS_LO + rep * N_POS_LANES, pos & (HEAD_DIM - 1), feat)
    feat = feat.astype(F32)
    kk_ref[...] = jnp.where(lane < HEAD_DIM, kn, feat).astype(BF16)

    _store_head_rows(qi_ref, _dot(n, wqi_ref[...]), None)
    kki_ref[...] = _dot(n, wki_ref[...]).astype(BF16)
    vt = _dot_nt(wvt_ref[...], n)
    row = lax.broadcasted_iota(jnp.int32, vt.shape, 0)
    vt_ref[...] = jnp.where(row == HEAD_DIM, 1.0, vt).astype(BF16)
    wit_ref[...] = _dot_nt(wwt_ref[...], n)[:N_HEADS]


def _inproj_call(n2, w_in, g_q, g_k, batch, seq, tm=512):
    n_tok, d = n2.shape
    sizes = (W_HEADS, W_HEADS, W_HEADS, W_HEADS, HEAD_DIM, HEAD_DIM, N_HEADS * D_IDX, D_IDX, N_HEADS)
    offs = np.concatenate([[0], np.cumsum(sizes)])
    col = lambda k: w_in[:, offs[k]:offs[k + 1]]
    idx_scale = (D_IDX ** -0.5) * (N_HEADS ** -0.5)
    w_sb = jnp.concatenate([col(0), col(1), col(2)], axis=1).astype(BF16)
    w_qd = col(3).astype(BF16)
    w_kd = jnp.concatenate([col(4), col(4)], axis=1).astype(BF16)
    w_qi = col(6).astype(BF16)
    w_ki = jnp.pad(col(7), ((0, 0), (0, LANES - D_IDX))).astype(BF16)
    w_vt = jnp.pad(col(5).T, ((0, VT_ROWS - HEAD_DIM), (0, 0))).astype(BF16)
    w_wt = jnp.pad((col(8) * idx_scale).T, ((0, 16 - N_HEADS), (0, 0))).astype(BF16)
    gq = jnp.tile(g_q, N_HEADS).reshape(1, W_HEADS)
    gk = jnp.tile(g_k, 2).reshape(1, LANES)
    head_of = np.arange(W_HEADS) // HEAD_DIM
    head_mean = jnp.asarray((head_of[:, None] == head_of[None, :]) / HEAD_DIM, BF16)

    tpb = seq // tm
    tok = lambda w: pl.BlockSpec((tm, w), lambda i: (i, 0))
    tr = lambda r: pl.BlockSpec((None, r, tm), lambda i: (i // tpb, 0, i % tpb))
    sd = lambda w, dt: jax.ShapeDtypeStruct((n_tok, w), dt)
    consts = (w_sb, w_qd, w_kd, w_qi, w_ki, w_vt, w_wt, gq, gk, head_mean)
    head_rows = jax.ShapeDtypeStruct((n_tok // TQ_DSA, N_HEADS, TQ_DSA, LANES), BF16)
    head_rows_spec = pl.BlockSpec((tm // TQ_DSA, N_HEADS, TQ_DSA, LANES), lambda i: (i, 0, 0, 0))
    return pl.pallas_call(
        functools.partial(_inproj_kernel, tiles_per_batch=tpb),
        out_shape=[sd(W_HEADS, BF16)] * 3 + [head_rows, sd(LANES, BF16), head_rows, sd(LANES, BF16),
                                             jax.ShapeDtypeStruct((batch, VT_ROWS, seq), BF16),
                                             jax.ShapeDtypeStruct((batch, N_HEADS, seq), F32)],
        grid=(n_tok // tm,),
        in_specs=[tok(d)] + [_const_spec(w.shape) for w in consts],
        out_specs=[tok(W_HEADS)] * 3 + [head_rows_spec, tok(LANES), head_rows_spec, tok(LANES),
                                        tr(VT_ROWS), tr(N_HEADS)],
        compiler_params=_cparams(1),
        name="inproj",
    )(n2, *consts)


def _split_heads(x, lane):
    xf = x.astype(F32)
    lo = jnp.where(lane < HEAD_DIM, xf, 0.0).astype(x.dtype)
    hi = jnp.where(lane >= HEAD_DIM, xf, 0.0).astype(x.dtype)
    return lo, hi


def _sb_kernel(q_ref, k_ref, v_ref, t_ref, o_ref, qs_ref, run_ref, acc_ref):
    i = pl.program_id(1)
    lane = lax.broadcasted_iota(jnp.int32, (TQ, LANES), 1)
    lane_k = lax.broadcasted_iota(jnp.int32, (TK, LANES), 1)
    for p in range(N_PAIRS):
        lo, hi = _split_heads(q_ref[:, p * LANES:(p + 1) * LANES], lane)
        qs_ref[p, :TQ, :] = lo
        qs_ref[p, TQ:, :] = hi

    def block(k0, diag):
        pairs = range(N_PAIRS)
        strict = (lax.broadcasted_iota(jnp.int32, (2 * TQ, TK), 1)
                  < lax.broadcasted_iota(jnp.int32, (2 * TQ, TK), 0) % TQ) if diag else None
        zs = [_dot_nt(qs_ref[p], k_ref[pl.ds(k0, TK), p * LANES:(p + 1) * LANES]) for p in pairs]
        lbs, ws = [], []
        for p in pairs:
            z = zs[p]
            sp = jnp.maximum(z, 0.0) + jnp.log(1.0 + jnp.exp(-jnp.abs(z)))
            l1m = jnp.where(strict, sp, 0.0) if diag else sp
            l_hi = l1m.astype(BF16)
            l_lo = (l1m - l_hi.astype(F32)).astype(BF16)
            ws.append(_dot(jnp.concatenate([l_hi, l_lo], axis=1), t_ref[...]))
            lbs.append(z - sp)
        for p in pairs:
            if diag:
                a = jnp.where(strict, jnp.exp(lbs[p] + ws[p][:, :TK]), 0.0).astype(BF16)
            else:
                a = jnp.exp(lbs[p] + ws[p][:, :TK] + run_ref[p]).astype(BF16)
            v_lo, v_hi = _split_heads(v_ref[pl.ds(k0, TK), p * LANES:(p + 1) * LANES], lane_k)
            upd = _dot(jnp.concatenate([a[:TQ], a[TQ:]], axis=1),
                       jnp.concatenate([v_lo, v_hi], axis=0))
            if diag:
                acc_ref[p] = upd
                run_ref[p] = ws[p][:, TK:]
            else:
                acc_ref[p] += upd
                run_ref[p] += ws[p][:, TK:]

    def more_blocks(jj):
        run_max = functools.reduce(jnp.maximum, [jnp.max(run_ref[p]) for p in range(N_PAIRS)])
        return jnp.logical_and(jj <= i, run_max > -F32_EXP_UNDERFLOW)

    block(pl.multiple_of(i * TK, TK), True)

    def body(state):
        jj, _ = state
        block(pl.multiple_of((i - jj) * TK, TK), False)
        return jj + 1, more_blocks(jj + 1)

    lax.while_loop(lambda state: state[1], body, (jnp.int32(1), more_blocks(jnp.int32(1))))
    for p in range(N_PAIRS):
        o_ref[:, p * LANES:(p + 1) * LANES] = acc_ref[p].astype(o_ref.dtype)


def _sb_call(q, k, v, batch, seq):
    q, k, v = (t.reshape(batch, seq, W_HEADS) for t in (q, k, v))
    j_idx = np.arange(TK)
    suffix = (j_idx[:, None] > j_idx[None, :]).astype(np.float32)
    tmat = -np.concatenate([suffix, np.ones((TK, LANES), np.float32)], axis=1)
    tmat = jnp.asarray(np.concatenate([tmat, tmat], axis=0), BF16)
    blk = pl.BlockSpec((None, TQ, W_HEADS), lambda b, i: (b, i, 0))
    allk = pl.BlockSpec((None, seq, W_HEADS), lambda b, i: (b, 0, 0))
    return pl.pallas_call(
        _sb_kernel,
        out_shape=jax.ShapeDtypeStruct((batch, seq, W_HEADS), BF16),
        grid=(batch, seq // TQ),
        in_specs=[blk, allk, allk, _const_spec(tmat.shape)],
        out_specs=blk,
        scratch_shapes=[pltpu.VMEM((N_PAIRS, 2 * TQ, LANES), BF16),
                        pltpu.VMEM((N_PAIRS, 2 * TQ, LANES), F32),
                        pltpu.VMEM((N_PAIRS, TQ, LANES), F32)],
        compiler_params=_cparams(2),
        name="sb",
    )(q, k, v, tmat)


def _dsa_kernel(qd_st, qi_st, wit_ref, kk_ref, vt_ref, kki_ref, o_ref,
                sc_ref, top_ref, m_ref, acc_ref, buf_ref, qit_ref, qdt_ref, *, topk):
    TQ = qd_st.shape[1]
    i = pl.program_id(1)
    t0 = i * TQ

    n_s = (i + CH_S // TQ) // (CH_S // TQ)
    n_h = n_s * (CH_S // CH_H)

    for q_st, qt_ref in ((qi_st, qit_ref), (qd_st, qdt_ref)):
        for p in range(N_PAIRS):
            qt_ref[p] = jnp.concatenate(
                [q_st[h].astype(F32).T for h in (2 * p, 2 * p + 1)], axis=1).astype(BF16)

    def products(qt_ref, k_ref):
        def produce(k0):
            kch = k_ref[pl.ds(k0, CH_H), :]
            return [_dot(kch, qt_ref[p]) for p in range(N_PAIRS)]
        return produce

    def store_products(vals):
        for p in range(N_PAIRS):
            buf_ref[2 * p] = vals[p][:, :TQ]
            buf_ref[2 * p + 1] = vals[p][:, TQ:]

    def sweep(produce, consume):
        from_buf = lambda h: buf_ref[h]

        def trip(c, prefetch):
            k_a = pl.multiple_of(c * CH_S, CH_S)
            k_b = pl.multiple_of(k_a + CH_H, CH_H)
            vals_b = produce(k_b)
            consume(from_buf, k_a)
            if prefetch:
                vals_next = produce(pl.multiple_of(k_a + CH_S, CH_S))
            consume(lambda h: vals_b[h // 2][:, (h % 2) * TQ:(h % 2 + 1) * TQ], k_b)
            if prefetch:
                store_products(vals_next)

        store_products(produce(0))

        def body(c, _):
            trip(c, True)
            return 0

        lax.fori_loop(0, n_s - 1, body, 0)
        trip(n_s - 1, False)

    wit = wit_ref[...]

    def score_consume(prod, k0):
        score = None
        for h in range(N_HEADS):
            term = jnp.maximum(prod(h), 0.0) * wit[h:h + 1, :]
            score = term if score is None else score + term
        causal = (k0 + lax.broadcasted_iota(jnp.int32, (CH_H, TQ), 0)
                  <= t0 + lax.broadcasted_iota(jnp.int32, (CH_H, TQ), 1))
        score = jnp.where(causal, score, -jnp.inf)
        sc_ref[pl.ds(k0, CH_H), :] = score
        top_ref[pl.ds(k0, CH_H), :] = score.astype(BF16)

    sweep(products(qit_ref, kki_ref), score_consume)

    def float_of_rank(u):
        key = u ^ INT_MIN
        return pltpu.bitcast(jnp.where(key < 0, key ^ np.int32(0x7FFFFFFF), key), F32)

    def count_top_ge(cand):
        one, zero = jnp.ones((BF16_SUBLANES, TQ), BF16), jnp.zeros((BF16_SUBLANES, TQ), BF16)

        def cbody(c, acc):
            k0 = pl.multiple_of(c * CH_S, CH_S)
            tops = top_ref[pl.ds(k0, CH_S), :]
            parts = [jnp.where(tops[j * BF16_SUBLANES:(j + 1) * BF16_SUBLANES] >= cand, one, zero)
                     for j in range(CH_S // BF16_SUBLANES)]
            while len(parts) > 1:
                parts = [a + b for a, b in zip(parts[::2], parts[1::2])]
            return acc + parts[0]

        acc = lax.fori_loop(0, n_s, cbody, zero)
        return jnp.sum(acc.astype(F32), axis=0, keepdims=True)

    def top_body(b, u16):
        cand = u16 | jnp.left_shift(jnp.int32(1), 15 - b)
        cand_f = float_of_rank(jnp.left_shift(cand, 16)).astype(BF16)
        return jnp.where(count_top_ge(jnp.broadcast_to(cand_f, (BF16_SUBLANES, TQ))) >= topk,
                         cand, u16)

    u16 = lax.fori_loop(0, 16, top_body, jnp.zeros((1, TQ), jnp.int32))

    def count(pred):
        def cbody(c, acc):
            k0 = pl.multiple_of(c * CH_S, CH_S)
            sch = sc_ref[pl.ds(k0, CH_S), :].reshape(CH_S // SUBLANES, SUBLANES, TQ)
            return acc + jnp.sum(jnp.where(pred(sch), 1, 0), axis=0)
        acc = lax.fori_loop(0, n_s, cbody, jnp.zeros((SUBLANES, TQ), jnp.int32))
        tot = jnp.sum(acc.astype(F32), axis=0, keepdims=True)
        return jnp.broadcast_to(tot, (SUBLANES, TQ))

    base = jnp.broadcast_to(jnp.left_shift(u16, 16) - (1 << 15), (SUBLANES, TQ))

    def window_body(b, state):
        off, n_off = state
        cand = off | jnp.left_shift(jnp.int32(1), SELECT_WINDOW_BITS - 1 - b)
        cand_f = float_of_rank(base + cand)
        n_cand = count(lambda sch: sch >= cand_f[None])
        take = n_cand >= topk
        return jnp.where(take, cand, off), jnp.where(take, n_cand, n_off)

    few_keys = t0 + lax.broadcasted_iota(jnp.int32, (SUBLANES, TQ), 1) < topk
    unknown = jnp.full((SUBLANES, TQ), -1.0, F32)
    state = lax.fori_loop(0, SELECT_EARLY_PASSES, window_body,
                          (jnp.zeros((SUBLANES, TQ), jnp.int32), unknown))
    still_open = jnp.max(jnp.where(jnp.logical_or(state[1] == topk, few_keys), 0.0, 1.0))
    offset, n_off = lax.cond(
        still_open > 0.0,
        lambda st: lax.fori_loop(SELECT_EARLY_PASSES, SELECT_WINDOW_BITS, window_body, st),
        lambda st: st, state)
    thr = jnp.where(few_keys, jnp.finfo(F32).min, float_of_rank(base + offset))
    thr_row = thr[0:1, :]

    n_ge = jnp.where(few_keys, 0.0, jnp.where(n_off < 0.0, topk + 1.0, n_off))

    @pl.when(jnp.max(n_ge) > topk)
    def _():
        need = topk - count(lambda sch: sch > thr[None])[0:1, :]
        before = jnp.where(lax.broadcasted_iota(jnp.int32, (CH_H, CH_H), 1)
                           < lax.broadcasted_iota(jnp.int32, (CH_H, CH_H), 0), 1.0, 0.0).astype(BF16)

        def tie_body(c, seen):
            k0 = pl.multiple_of(c * CH_H, CH_H)
            sch = sc_ref[pl.ds(k0, CH_H), :]
            eq = jnp.where(sch == thr_row, 1.0, 0.0)
            rank = _dot(before, eq.astype(BF16)) + seen
            tie_rank = jnp.where(sch == thr_row, rank, -1.0)
            sc_ref[pl.ds(k0, CH_H), :] = jnp.where(tie_rank >= need, -jnp.inf, sch)
            return seen + jnp.sum(eq, axis=0, keepdims=True)

        lax.fori_loop(0, n_h, tie_body, jnp.zeros((1, TQ), F32))

    m_ref[...] = jnp.full(m_ref.shape, NEG_BIG, F32)
    acc_ref[...] = jnp.zeros(acc_ref.shape, F32)

    def attn_consume(prod, k0):
        vch = vt_ref[:, pl.ds(k0, CH_H)]
        sel = sc_ref[pl.ds(k0, CH_H), :] >= thr_row
        for p in range(N_PAIRS):
            probs, alphas = [], []
            for h in (2 * p, 2 * p + 1):
                lg = jnp.where(sel, prod(h), NEG_BIG)
                m_old = m_ref[h]
                m_new = jnp.maximum(m_old, jnp.max(lg, axis=0, keepdims=True))
                m_ref[h] = m_new
                alphas.append(jnp.exp2(m_old - m_new))
                probs.append(jnp.exp2(lg - m_new).astype(BF16))
            upd = _dot(vch, jnp.concatenate(probs, axis=1))
            acc_ref[p] = acc_ref[p] * jnp.concatenate(alphas, axis=1) + upd

    sweep(products(qdt_ref, kk_ref), attn_consume)
    for p in range(N_PAIRS):
        acc = acc_ref[p]
        out_t = acc[:HEAD_DIM] / acc[HEAD_DIM:HEAD_DIM + 1]
        pair = jnp.concatenate([out_t[:, :TQ], out_t[:, TQ:]], axis=0)
        o_ref[:, p * LANES:(p + 1) * LANES] = pair.T.astype(o_ref.dtype)


def _dsa_call(qd_rows, qi_rows, wit, kk, vt, kki, batch, seq):
    r3 = lambda t: t.reshape(batch, seq, t.shape[-1])
    kk, kki = map(r3, (kk, kki))
    topk = min(TOPK_MAX, seq // 4)
    assert seq // BF16_SUBLANES <= BF16_MAX_EXACT_COUNT
    TQ = TQ_DSA
    nq = seq // TQ
    rows = pl.BlockSpec((None, N_HEADS, TQ, LANES), lambda b, i: (b * nq + i, 0, 0, 0))
    allk = pl.BlockSpec((None, seq, LANES), lambda b, i: (b, 0, 0))
    return pl.pallas_call(
        functools.partial(_dsa_kernel, topk=topk),
        out_shape=jax.ShapeDtypeStruct((batch, seq, W_HEADS), BF16),
        grid=(batch, nq),
        in_specs=[rows, rows,
                  pl.BlockSpec((None, N_HEADS, TQ), lambda b, i: (b, 0, i)),
                  allk, pl.BlockSpec((None, VT_ROWS, seq), lambda b, i: (b, 0, 0)), allk],
        out_specs=pl.BlockSpec((None, TQ, W_HEADS), lambda b, i: (b, i, 0)),
        scratch_shapes=[pltpu.VMEM((seq, TQ), F32),
                        pltpu.VMEM((seq, TQ), BF16),
                        pltpu.VMEM((N_HEADS, 1, TQ), F32),
                        pltpu.VMEM((N_PAIRS, VT_ROWS, 2 * TQ), F32),
                        pltpu.VMEM((N_HEADS, CH_H, TQ), F32),
                        pltpu.VMEM((N_PAIRS, LANES, 2 * TQ), BF16),
                        pltpu.VMEM((N_PAIRS, LANES, 2 * TQ), BF16)],
        compiler_params=_cparams(2),
        name="dsa",
    )(qd_rows, qi_rows, wit, kk, vt, kki)


def _mix_kernel(h_ref, n_ref, ysb_ref, yd_ref, mod_ref, wps_ref, wpd_ref, wg_ref, bg_ref,
                wo_ref, o_ref):
    d = h_ref.shape[1]
    y_sb = _dot(ysb_ref[...], wps_ref[...])
    y_d = _dot(yd_ref[...], wpd_ref[...])
    gates = jax.nn.sigmoid(_dot(n_ref[...], wg_ref[...]) + bg_ref[...])
    merged = (gates[:, :d] * y_sb + gates[:, d:] * y_d).astype(BF16)
    o_ref[...] = h_ref[...] + mod_ref[5] * _dot(merged, wo_ref[...])


def _mix_call(h, n2, y_sb, y_d, mod, w_proj_sb, w_proj_dsa, w_gate, b_gate, w_out, *, seq, tm=512):
    n_tok, d = h.shape
    tok = lambda w: pl.BlockSpec((tm, w), lambda i: (i, 0))
    ws = [w.astype(BF16) for w in (w_proj_sb, w_proj_dsa, w_gate)]
    bg = b_gate.reshape(1, -1)
    wo = w_out.astype(BF16)
    return pl.pallas_call(
        _mix_kernel,
        out_shape=jax.ShapeDtypeStruct((n_tok, d), F32),
        grid=(n_tok // tm,),
        in_specs=[tok(d), tok(d), tok(W_HEADS), tok(W_HEADS), _mod_spec(seq // tm, d)]
                 + [_const_spec(w.shape) for w in (*ws, bg, wo)],
        out_specs=tok(d),
        compiler_params=_cparams(1),
        name="mix",
    )(h, n2, y_sb, y_d, mod, *ws, bg, wo)


def kernel(x, c, w_ada, b_ada, g_ffn1, w1_ffn1, w3_ffn1, w2_ffn1, g_mix, w_in, g_q_dsa, g_k_dsa,
           w_proj_sb, w_proj_dsa, w_gate, b_gate, w_out, g_ffn2, w1_ffn2, w3_ffn2, w2_ffn2):
    batch, seq, d = x.shape
    h = x.reshape(batch * seq, d)
    for l in range(w_ada.shape[0]):
        bf = lambda w: w[l].astype(BF16)
        mod = _mod_call(c, w_ada[l], b_ada[l])
        h, n2 = _ffn_call(h, mod, g_ffn1[l], bf(w1_ffn1), bf(w3_ffn1), bf(w2_ffn1), g_mix[l],
                          mod_base=0, emit_next=True, seq=seq)
        q_sb, k_sb, v_sb, q_d, kk, q_i, kki, vt, wit = _inproj_call(
            n2, w_in[l], g_q_dsa[l], g_k_dsa[l], batch, seq)
        y_sb = _sb_call(q_sb, k_sb, v_sb, batch, seq).reshape(batch * seq, W_HEADS)
        y_d = _dsa_call(q_d, q_i, wit, kk, vt, kki, batch, seq).reshape(batch * seq, W_HEADS)
        h = _mix_call(h, n2, y_sb, y_d, mod, w_proj_sb[l], w_proj_dsa[l], w_gate[l], b_gate[l],
                      w_out[l], seq=seq)
        (h,) = _ffn_call(h, mod, g_ffn2[l], bf(w1_ffn2), bf(w3_ffn2), bf(w2_ffn2), g_ffn2[l],
                         mod_base=6, emit_next=False, seq=seq)
    return h.reshape(batch, seq, d)
```

```python
import functools

import numpy as np
import jax
import jax.numpy as jnp
from jax import lax
from jax.experimental import pallas as pl
from jax.experimental.pallas import tpu as pltpu

F32 = jnp.float32
BF16 = jnp.bfloat16

HEAD_DIM = 64
N_HEADS = 8
W_HEADS = N_HEADS * HEAD_DIM
N_PAIRS = N_HEADS // 2
LANES = 128
SUBLANES = 8
BF16_SUBLANES = 16
BF16_MAX_EXACT_COUNT = 256
SELECT_WINDOW_BITS = 17
SELECT_EARLY_PASSES = 11
TOPK_MAX = 256
N_MOD = 9
RMS_EPS = 1e-6
D_IDX = 64

TQ = 128
TQ_DSA = 128
TK = 128
CH_S = 512
CH_H = 256
VT_ROWS = 80
LANE_POS_HI, LANE_POS_LO = HEAD_DIM, HEAD_DIM + 1
N_POS_LANES = 2
N_SLOPE_PARTS = 3
LOG2E = float(np.log2(np.e))
NEG_BIG = -1e30
F32_EXP_UNDERFLOW = 104.0
INT_MIN = np.int32(-2**31)
VMEM_LIMIT = 56 * 1024 * 1024


def _cparams(n_axes):
    return pltpu.CompilerParams(
        dimension_semantics=("arbitrary",) * n_axes, vmem_limit_bytes=VMEM_LIMIT)


def _const_spec(shape):
    nd = len(shape)
    return pl.BlockSpec(shape, lambda *_: (0,) * nd, pipeline_mode=pl.Buffered(1))


def _dot(a, b):
    return jnp.dot(a, b, preferred_element_type=F32)


def _dot_nt(a, b):
    return lax.dot_general(a, b, (((1,), (1,)), ((), ())), preferred_element_type=F32)


def _mod_kernel(c_ref, w_ref, b_ref, o_ref, sc_ref):
    nb, kdim, _ = c_ref.shape
    tn = w_ref.shape[1]
    nch = tn // LANES

    @pl.when(pl.program_id(0) == 0)
    def _():
        c = c_ref[...]
        sc_ref[...] = c * jax.nn.sigmoid(c)

    def body(kb, accs):
        k0 = pl.multiple_of(kb * SUBLANES, SUBLANES)
        wblk = w_ref[pl.ds(k0, SUBLANES), :]
        out = []
        for b in range(nb):
            sb = sc_ref[b, pl.ds(k0, SUBLANES), :]
            for ch in range(nch):
                out.append(accs[b * nch + ch] + wblk[:, ch * LANES:(ch + 1) * LANES] * sb)
        return tuple(out)

    accs = lax.fori_loop(0, kdim // SUBLANES, body,
                         tuple(jnp.zeros((SUBLANES, LANES), F32) for _ in range(nb * nch)))
    for b in range(nb):
        row = jnp.concatenate(
            [jnp.sum(accs[b * nch + ch], axis=0, keepdims=True) for ch in range(nch)], axis=1)
        o_ref[0, b] = row + b_ref[...]


def _mod_call(c, w_ada, b_ada):
    nb, d = c.shape
    n_out = w_ada.shape[1]
    c_rep = jnp.broadcast_to(c[:, :, None], (nb, d, LANES))
    return pl.pallas_call(
        _mod_kernel,
        out_shape=jax.ShapeDtypeStruct((n_out // d, nb, 1, d), F32),
        grid=(n_out // d,),
        in_specs=[_const_spec((nb, d, LANES)),
                  pl.BlockSpec((d, d), lambda j: (0, j)),
                  pl.BlockSpec((1, d), lambda j: (0, j))],
        out_specs=pl.BlockSpec((1, nb, 1, d), lambda j: (j, 0, 0, 0)),
        scratch_shapes=[pltpu.VMEM((nb, d, LANES), F32)],
        compiler_params=_cparams(1),
        name="mod",
    )(c_rep, w_ada, b_ada.reshape(1, n_out))


def _norm_mod(h, g, shift, scale):
    y = h * lax.rsqrt(jnp.mean(h * h, axis=-1, keepdims=True) + RMS_EPS)
    return (y * g) * (1.0 + scale) + shift


def _ffn_kernel(h_ref, mod_ref, g_ref, w1_ref, w3_ref, w2_ref, gn_ref, *out_refs,
                mod_base, emit_next, n_chunks):
    o_ref = out_refs[0]
    h = h_ref[...]
    n = _norm_mod(h, g_ref[...], mod_ref[mod_base], mod_ref[mod_base + 1]).astype(BF16)
    fc = w1_ref.shape[1] // n_chunks
    acc = jnp.zeros(h.shape, F32)
    for c in range(n_chunks):
        a = _dot(n, w1_ref[:, c * fc:(c + 1) * fc])
        b = _dot(n, w3_ref[:, c * fc:(c + 1) * fc])
        act = (a * jax.nn.sigmoid(a) * b).astype(BF16)
        acc = acc + _dot(act, w2_ref[c * fc:(c + 1) * fc, :])
    hn = h + (0.5 * mod_ref[mod_base + 2]) * acc
    o_ref[...] = hn
    if emit_next:
        out_refs[1][...] = _norm_mod(hn, gn_ref[...], mod_ref[mod_base + 3],
                                     mod_ref[mod_base + 4]).astype(BF16)


def _mod_spec(tiles_per_batch, d):
    return pl.BlockSpec((N_MOD, None, 1, d), lambda i: (0, i // tiles_per_batch, 0, 0))


def _ffn_call(h, mod, g, w1, w3, w2, g_next, *, mod_base, emit_next, seq, tm=1024):
    n_tok, d = h.shape
    f = w1.shape[1]
    tok = pl.BlockSpec((tm, d), lambda i: (i, 0))
    out_shape = [jax.ShapeDtypeStruct((n_tok, d), F32)]
    out_specs = [tok]
    if emit_next:
        out_shape.append(jax.ShapeDtypeStruct((n_tok, d), BF16))
        out_specs.append(tok)
    return pl.pallas_call(
        functools.partial(_ffn_kernel, mod_base=mod_base, emit_next=emit_next, n_chunks=4),
        out_shape=out_shape,
        grid=(n_tok // tm,),
        in_specs=[tok, _mod_spec(seq // tm, d), _const_spec((1, d)),
                  _const_spec((d, f)), _const_spec((d, f)), _const_spec((f, d)),
                  _const_spec((1, d))],
        out_specs=out_specs,
        compiler_params=_cparams(1),
        name="ffn",
    )(h, mod, g.reshape(1, d), w1, w3, w2, g_next.reshape(1, d))


def _alibi_slope_parts():
    out = []
    for h in range(N_HEADS):
        rest, parts = np.float32(2.0 ** (-8.0 * (h + 1) / N_HEADS) * LOG2E), []
        for _ in range(N_SLOPE_PARTS):
            parts.append(np.float32(np.asarray(rest, dtype=BF16)))
            rest = np.float32(rest - parts[-1])
        out.append(parts)
    return out


def _store_head_rows(o_ref, x, slope_parts):
    tm = x.shape[0]
    lane = lax.broadcasted_iota(jnp.int32, (tm, LANES), 1)
    for h in range(N_HEADS):
        rows = x[:, (h // 2) * LANES:(h // 2 + 1) * LANES]
        if h % 2:
            rows = pltpu.roll(rows, HEAD_DIM, axis=1)
        rows = jnp.where(lane < HEAD_DIM, rows, 0.0)
        if slope_parts is not None:
            for rep, part in enumerate(slope_parts[h]):
                rows = jnp.where(lane == LANE_POS_HI + rep * N_POS_LANES, part * HEAD_DIM, rows)
                rows = jnp.where(lane == LANE_POS_LO + rep * N_POS_LANES, part, rows)
        o_ref[:, h] = rows.astype(BF16).reshape(tm // TQ_DSA, TQ_DSA, LANES)


def _inproj_kernel(n_ref, wsb_ref, wqd_ref, wkd_ref, wqi_ref, wki_ref, wvt_ref, wwt_ref,
                   gq_ref, gk_ref, hm_ref,
                   qsb_ref, ksb_ref, vsb_ref, qd_ref, kk_ref, qi_ref, kki_ref, vt_ref, wit_ref,
                   *, tiles_per_batch):
    n = n_ref[...]
    tm = n.shape[0]
    qk_scale = HEAD_DIM ** -0.5
    sb = _dot(n, wsb_ref[...])
    qsb_ref[...] = (sb[:, :W_HEADS] * qk_scale).astype(BF16)
    ksb_ref[...] = sb[:, W_HEADS:2 * W_HEADS].astype(BF16)
    vsb_ref[...] = sb[:, 2 * W_HEADS:].astype(BF16)

    qd = _dot(n, wqd_ref[...])
    ms = _dot((qd * qd).astype(BF16), hm_ref[...])
    _store_head_rows(qd_ref, qd * lax.rsqrt(ms + RMS_EPS) * (gq_ref[...] * (qk_scale * LOG2E)),
                     _alibi_slope_parts())

    kd = _dot(n, wkd_ref[...])
    kn = kd * lax.rsqrt(jnp.mean(kd * kd, axis=-1, keepdims=True) + RMS_EPS) * gk_ref[...]
    lane = lax.broadcasted_iota(jnp.int32, (tm, LANES), 1)
    pos = ((pl.program_id(0) % tiles_per_batch) * tm
           + lax.broadcasted_iota(jnp.int32, (tm, LANES), 0))
    feat = jnp.zeros((tm, LANES), jnp.int32)
    for rep in range(N_SLOPE_PARTS):
        feat = jnp.where(lane == LANE_POS_HI + rep * N_POS_LANES,
                         pos >> (HEAD_DIM.bit_length() - 1), feat)
        feat = jnp.where(lane == LANE_POS_LO + rep * N_POS_LANES, pos & (HEAD_DIM - 1), feat)
    feat = feat.astype(F32)
    kk_ref[...] = jnp.where(lane < HEAD_DIM, kn, feat).astype(BF16)

    _store_head_rows(qi_ref, _dot(n, wqi_ref[...]), None)
    kki_ref[...] = _dot(n, wki_ref[...]).astype(BF16)
    vt = _dot_nt(wvt_ref[...], n)
    row = lax.broadcasted_iota(jnp.int32, vt.shape, 0)
    vt_ref[...] = jnp.where(row == HEAD_DIM, 1.0, vt).astype(BF16)
    wit_ref[...] = _dot_nt(wwt_ref[...], n)[:N_HEADS]


def _inproj_call(n2, w_in, g_q, g_k, batch, seq, tm=512):
    n_tok, d = n2.shape
    sizes = (W_HEADS, W_HEADS, W_HEADS, W_HEADS, HEAD_DIM, HEAD_DIM, N_HEADS * D_IDX, D_IDX, N_HEADS)
    offs = np.concatenate([[0], np.cumsum(sizes)])
    col = lambda k: w_in[:, offs[k]:offs[k + 1]]
    idx_scale = (D_IDX ** -0.5) * (N_HEADS ** -0.5)
    w_sb = jnp.concatenate([col(0), col(1), col(2)], axis=1).astype(BF16)
    w_qd = col(3).astype(BF16)
    w_kd = jnp.concatenate([col(4), col(4)], axis=1).astype(BF16)
    w_qi = col(6).astype(BF16)
    w_ki = jnp.pad(col(7), ((0, 0), (0, LANES - D_IDX))).astype(BF16)
    w_vt = jnp.pad(col(5).T, ((0, VT_ROWS - HEAD_DIM), (0, 0))).astype(BF16)
    w_wt = jnp.pad((col(8) * idx_scale).T, ((0, 16 - N_HEADS), (0, 0))).astype(BF16)
    gq = jnp.tile(g_q, N_HEADS).reshape(1, W_HEADS)
    gk = jnp.tile(g_k, 2).reshape(1, LANES)
    head_of = np.arange(W_HEADS) // HEAD_DIM
    head_mean = jnp.asarray((head_of[:, None] == head_of[None, :]) / HEAD_DIM, BF16)

    tpb = seq // tm
    tok = lambda w: pl.BlockSpec((tm, w), lambda i: (i, 0))
    tr = lambda r: pl.BlockSpec((None, r, tm), lambda i: (i // tpb, 0, i % tpb))
    sd = lambda w, dt: jax.ShapeDtypeStruct((n_tok, w), dt)
    consts = (w_sb, w_qd, w_kd, w_qi, w_ki, w_vt, w_wt, gq, gk, head_mean)
    head_rows = jax.ShapeDtypeStruct((n_tok // TQ_DSA, N_HEADS, TQ_DSA, LANES), BF16)
    head_rows_spec = pl.BlockSpec((tm // TQ_DSA, N_HEADS, TQ_DSA, LANES), lambda i: (i, 0, 0, 0))
    return pl.pallas_call(
        functools.partial(_inproj_kernel, tiles_per_batch=tpb),
        out_shape=[sd(W_HEADS, BF16)] * 3 + [head_rows, sd(LANES, BF16), head_rows, sd(LANES, BF16),
                                             jax.ShapeDtypeStruct((batch, VT_ROWS, seq), BF16),
                                             jax.ShapeDtypeStruct((batch, N_HEADS, seq), F32)],
        grid=(n_tok // tm,),
        in_specs=[tok(d)] + [_const_spec(w.shape) for w in consts],
        out_specs=[tok(W_HEADS)] * 3 + [head_rows_spec, tok(LANES), head_rows_spec, tok(LANES),
                                        tr(VT_ROWS), tr(N_HEADS)],
        compiler_params=_cparams(1),
        name="inproj",
    )(n2, *consts)


def _split_heads(x, lane):
    xf = x.astype(F32)
    lo = jnp.where(lane < HEAD_DIM, xf, 0.0).astype(x.dtype)
    hi = jnp.where(lane >= HEAD_DIM, xf, 0.0).astype(x.dtype)
    return lo, hi


def _sb_kernel(q_ref, k_ref, v_ref, t_ref, o_ref, qs_ref, run_ref, acc_ref):
    i = pl.program_id(1)
    lane = lax.broadcasted_iota(jnp.int32, (TQ, LANES), 1)
    lane_k = lax.broadcasted_iota(jnp.int32, (TK, LANES), 1)
    for p in range(N_PAIRS):
        lo, hi = _split_heads(q_ref[:, p * LANES:(p + 1) * LANES], lane)
        qs_ref[p, :TQ, :] = lo
        qs_ref[p, TQ:, :] = hi

    def block(k0, diag):
        pairs = range(N_PAIRS)
        strict = (lax.broadcasted_iota(jnp.int32, (2 * TQ, TK), 1)
                  < lax.broadcasted_iota(jnp.int32, (2 * TQ, TK), 0) % TQ) if diag else None
        zs = [_dot_nt(qs_ref[p], k_ref[pl.ds(k0, TK), p * LANES:(p + 1) * LANES]) for p in pairs]
        lbs, ws = [], []
        for p in pairs:
            z = zs[p]
            sp = jnp.maximum(z, 0.0) + jnp.log(1.0 + jnp.exp(-jnp.abs(z)))
            l1m = jnp.where(strict, sp, 0.0) if diag else sp
            l_hi = l1m.astype(BF16)
            l_lo = (l1m - l_hi.astype(F32)).astype(BF16)
            ws.append(_dot(jnp.concatenate([l_hi, l_lo], axis=1), t_ref[...]))
            lbs.append(z - sp)
        for p in pairs:
            if diag:
                a = jnp.where(strict, jnp.exp(lbs[p] + ws[p][:, :TK]), 0.0).astype(BF16)
            else:
                a = jnp.exp(lbs[p] + ws[p][:, :TK] + run_ref[p]).astype(BF16)
            v_lo, v_hi = _split_heads(v_ref[pl.ds(k0, TK), p * LANES:(p + 1) * LANES], lane_k)
            upd = _dot(jnp.concatenate([a[:TQ], a[TQ:]], axis=1),
                       jnp.concatenate([v_lo, v_hi], axis=0))
            if diag:
                acc_ref[p] = upd
                run_ref[p] = ws[p][:, TK:]
            else:
                acc_ref[p] += upd
                run_ref[p] += ws[p][:, TK:]

    def more_blocks(jj):
        run_max = functools.reduce(jnp.maximum, [jnp.max(run_ref[p]) for p in range(N_PAIRS)])
        return jnp.logical_and(jj <= i, run_max > -F32_EXP_UNDERFLOW)

    block(pl.multiple_of(i * TK, TK), True)

    def body(state):
        jj, _ = state
        block(pl.multiple_of((i - jj) * TK, TK), False)
        return jj + 1, more_blocks(jj + 1)

    lax.while_loop(lambda state: state[1], body, (jnp.int32(1), more_blocks(jnp.int32(1))))
    for p in range(N_PAIRS):
        o_ref[:, p * LANES:(p + 1) * LANES] = acc_ref[p].astype(o_ref.dtype)


def _sb_call(q, k, v, batch, seq):
    q, k, v = (t.reshape(batch, seq, W_HEADS) for t in (q, k, v))
    j_idx = np.arange(TK)
    suffix = (j_idx[:, None] > j_idx[None, :]).astype(np.float32)
    tmat = -np.concatenate([suffix, np.ones((TK, LANES), np.float32)], axis=1)
    tmat = jnp.asarray(np.concatenate([tmat, tmat], axis=0), BF16)
    blk = pl.BlockSpec((None, TQ, W_HEADS), lambda b, i: (b, i, 0))
    allk = pl.BlockSpec((None, seq, W_HEADS), lambda b, i: (b, 0, 0))
    return pl.pallas_call(
        _sb_kernel,
        out_shape=jax.ShapeDtypeStruct((batch, seq, W_HEADS), BF16),
        grid=(batch, seq // TQ),
        in_specs=[blk, allk, allk, _const_spec(tmat.shape)],
        out_specs=blk,
        scratch_shapes=[pltpu.VMEM((N_PAIRS, 2 * TQ, LANES), BF16),
                        pltpu.VMEM((N_PAIRS, 2 * TQ, LANES), F32),
                        pltpu.VMEM((N_PAIRS, TQ, LANES), F32)],
        compiler_params=_cparams(2),
        name="sb",
    )(q, k, v, tmat)


def _dsa_kernel(qd_st, qi_st, wit_ref, kk_ref, vt_ref, kki_ref, o_ref,
                sc_ref, top_ref, m_ref, acc_ref, buf_ref, qit_ref, qdt_ref, *, topk):
    TQ = qd_st.shape[1]
    i = pl.program_id(1)
    t0 = i * TQ

    n_s = (i + CH_S // TQ) // (CH_S // TQ)

    for q_st, qt_ref in ((qi_st, qit_ref), (qd_st, qdt_ref)):
        for p in range(N_PAIRS):
            qt_ref[p] = jnp.concatenate(
                [q_st[h].astype(F32).T for h in (2 * p, 2 * p + 1)], axis=1).astype(BF16)

    def products(qt_ref, k_ref):
        def produce(k0):
            kch = k_ref[pl.ds(k0, CH_H), :]
            return [_dot(kch, qt_ref[p]) for p in range(N_PAIRS)]
        return produce

    def store_products(vals):
        for p in range(N_PAIRS):
            buf_ref[2 * p] = vals[p][:, :TQ]
            buf_ref[2 * p + 1] = vals[p][:, TQ:]

    def sweep(produce, consume):
        from_buf = lambda h: buf_ref[h]

        def trip(c, prefetch):
            k_a = pl.multiple_of(c * CH_S, CH_S)
            k_b = pl.multiple_of(k_a + CH_H, CH_H)
            vals_b = produce(k_b)
            consume(from_buf, k_a)
            if prefetch:
                vals_next = produce(pl.multiple_of(k_a + CH_S, CH_S))
            consume(lambda h: vals_b[h // 2][:, (h % 2) * TQ:(h % 2 + 1) * TQ], k_b)
            if prefetch:
                store_products(vals_next)

        store_products(produce(0))

        def body(c, _):
            trip(c, True)
            return 0

        lax.fori_loop(0, n_s - 1, body, 0)
        trip(n_s - 1, False)

    wit = wit_ref[...]

    def score_consume(prod, k0):
        score = None
        for h in range(N_HEADS):
            term = jnp.maximum(prod(h), 0.0) * wit[h:h + 1, :]
            score = term if score is None else score + term
        causal = (k0 + lax.broadcasted_iota(jnp.int32, (CH_H, TQ), 0)
                  <= t0 + lax.broadcasted_iota(jnp.int32, (CH_H, TQ), 1))
        score = jnp.where(causal, score, -jnp.inf)
        sc_ref[pl.ds(k0, CH_H), :] = score
        top_ref[pl.ds(k0, CH_H), :] = score.astype(BF16)

    sweep(products(qit_ref, kki_ref), score_consume)

    def float_of_rank(u):
        key = u ^ INT_MIN
        return pltpu.bitcast(jnp.where(key < 0, key ^ np.int32(0x7FFFFFFF), key), F32)

    def count_top_ge(cand):
        one, zero = jnp.ones((BF16_SUBLANES, TQ), BF16), jnp.zeros((BF16_SUBLANES, TQ), BF16)

        def cbody(c, acc):
            k0 = pl.multiple_of(c * CH_S, CH_S)
            tops = top_ref[pl.ds(k0, CH_S), :]
            parts = [jnp.where(tops[j * BF16_SUBLANES:(j + 1) * BF16_SUBLANES] >= cand, one, zero)
                     for j in range(CH_S // BF16_SUBLANES)]
            while len(parts) > 1:
                parts = [a + b for a, b in zip(parts[::2], parts[1::2])]
            return acc + parts[0]

        acc = lax.fori_loop(0, n_s, cbody, zero)
        return jnp.sum(acc.astype(F32), axis=0, keepdims=True)

    def top_body(b, u16):
        cand = u16 | jnp.left_shift(jnp.int32(1), 15 - b)
        cand_f = float_of_rank(jnp.left_shift(cand, 16)).astype(BF16)
        return jnp.where(count_top_ge(jnp.broadcast_to(cand_f, (BF16_SUBLANES, TQ))) >= topk,
                         cand, u16)

    u16 = lax.fori_loop(0, 16, top_body, jnp.zeros((1, TQ), jnp.int32))

    def count(pred):
        def cbody(c, acc):
            k0 = pl.multiple_of(c * CH_S, CH_S)
            sch = sc_ref[pl.ds(k0, CH_S), :].reshape(CH_S // SUBLANES, SUBLANES, TQ)
            return acc + jnp.sum(jnp.where(pred(sch), 1, 0), axis=0)
        acc = lax.fori_loop(0, n_s, cbody, jnp.zeros((SUBLANES, TQ), jnp.int32))
        tot = jnp.sum(acc.astype(F32), axis=0, keepdims=True)
        return jnp.broadcast_to(tot, (SUBLANES, TQ))

    base = jnp.broadcast_to(jnp.left_shift(u16, 16) - (1 << 15), (SUBLANES, TQ))

    def window_body(b, state):
        off, n_off = state
        cand = off | jnp.left_shift(jnp.int32(1), SELECT_WINDOW_BITS - 1 - b)
        cand_f = float_of_rank(base + cand)
        n_cand = count(lambda sch: sch >= cand_f[None])
        take = n_cand >= topk
        return jnp.where(take, cand, off), jnp.where(take, n_cand, n_off)

    few_keys = t0 + lax.broadcasted_iota(jnp.int32, (SUBLANES, TQ), 1) < topk
    unknown = jnp.full((SUBLANES, TQ), -1.0, F32)
    state = lax.fori_loop(0, SELECT_EARLY_PASSES, window_body,
                          (jnp.zeros((SUBLANES, TQ), jnp.int32), unknown))
    still_open = jnp.max(jnp.where(jnp.logical_or(state[1] == topk, few_keys), 0.0, 1.0))
    offset, n_off = lax.cond(
        still_open > 0.0,
        lambda st: lax.fori_loop(SELECT_EARLY_PASSES, SELECT_WINDOW_BITS, window_body, st),
        lambda st: st, state)
    thr = jnp.where(few_keys, jnp.finfo(F32).min, float_of_rank(base + offset))
    thr_row = thr[0:1, :]

    n_ge = jnp.where(few_keys, 0.0, jnp.where(n_off < 0.0, topk + 1.0, n_off))

    @pl.when(jnp.max(n_ge) > topk)
    def _():
        need = topk - count(lambda sch: sch > thr[None])[0:1, :]
        before = jnp.where(lax.broadcasted_iota(jnp.int32, (CH_H, CH_H), 1)
                           < lax.broadcasted_iota(jnp.int32, (CH_H, CH_H), 0), 1.0, 0.0).astype(BF16)

        def tie_body(c, seen):
            k0s = [pl.multiple_of(c * CH_S + j * CH_H, CH_H) for j in range(CH_S // CH_H)]
            schs = [sc_ref[pl.ds(k0, CH_H), :] for k0 in k0s]
            eqs = [jnp.where(sch == thr_row, 1.0, 0.0) for sch in schs]
            ranks = [_dot(before, eq.astype(BF16)) for eq in eqs]
            for k0, sch, eq, rank in zip(k0s, schs, eqs, ranks):
                tie_rank = jnp.where(sch == thr_row, rank + seen, -1.0)
                sc_ref[pl.ds(k0, CH_H), :] = jnp.where(tie_rank >= need, -jnp.inf, sch)
                seen = seen + jnp.sum(eq, axis=0, keepdims=True)
            return seen

        lax.fori_loop(0, n_s, tie_body, jnp.zeros((1, TQ), F32))

    m_ref[...] = jnp.full(m_ref.shape, NEG_BIG, F32)
    acc_ref[...] = jnp.zeros(acc_ref.shape, F32)

    def attn_consume(prod, k0):
        vch = vt_ref[:, pl.ds(k0, CH_H)]
        sel = sc_ref[pl.ds(k0, CH_H), :] >= thr_row
        for p in range(N_PAIRS):
            probs, alphas = [], []
            for h in (2 * p, 2 * p + 1):
                lg = jnp.where(sel, prod(h), NEG_BIG)
                m_old = m_ref[h]
                m_new = jnp.maximum(m_old, jnp.max(lg, axis=0, keepdims=True))
                m_ref[h] = m_new
                alphas.append(jnp.exp2(m_old - m_new))
                probs.append(jnp.exp2(lg - m_new).astype(BF16))
            upd = _dot(vch, jnp.concatenate(probs, axis=1))
            acc_ref[p] = acc_ref[p] * jnp.concatenate(alphas, axis=1) + upd

    sweep(products(qdt_ref, kk_ref), attn_consume)
    for p in range(N_PAIRS):
        acc = acc_ref[p]
        out_t = acc[:HEAD_DIM] / acc[HEAD_DIM:HEAD_DIM + 1]
        pair = jnp.concatenate([out_t[:, :TQ], out_t[:, TQ:]], axis=0)
        o_ref[:, p * LANES:(p + 1) * LANES] = pair.T.astype(o_ref.dtype)


def _dsa_call(qd_rows, qi_rows, wit, kk, vt, kki, batch, seq):
    r3 = lambda t: t.reshape(batch, seq, t.shape[-1])
    kk, kki = map(r3, (kk, kki))
    topk = min(TOPK_MAX, seq // 4)
    assert seq // BF16_SUBLANES <= BF16_MAX_EXACT_COUNT
    TQ = TQ_DSA
    nq = seq // TQ
    rows = pl.BlockSpec((None, N_HEADS, TQ, LANES), lambda b, i: (b * nq + i, 0, 0, 0))
    allk = pl.BlockSpec((None, seq, LANES), lambda b, i: (b, 0, 0))
    return pl.pallas_call(
        functools.partial(_dsa_kernel, topk=topk),
        out_shape=jax.ShapeDtypeStruct((batch, seq, W_HEADS), BF16),
        grid=(batch, nq),
        in_specs=[rows, rows,
                  pl.BlockSpec((None, N_HEADS, TQ), lambda b, i: (b, 0, i)),
                  allk, pl.BlockSpec((None, VT_ROWS, seq), lambda b, i: (b, 0, 0)), allk],
        out_specs=pl.BlockSpec((None, TQ, W_HEADS), lambda b, i: (b, i, 0)),
        scratch_shapes=[pltpu.VMEM((seq, TQ), F32),
                        pltpu.VMEM((seq, TQ), BF16),
                        pltpu.VMEM((N_HEADS, 1, TQ), F32),
                        pltpu.VMEM((N_PAIRS, VT_ROWS, 2 * TQ), F32),
                        pltpu.VMEM((N_HEADS, CH_H, TQ), F32),
                        pltpu.VMEM((N_PAIRS, LANES, 2 * TQ), BF16),
                        pltpu.VMEM((N_PAIRS, LANES, 2 * TQ), BF16)],
        compiler_params=_cparams(2),
        name="dsa",
    )(qd_rows, qi_rows, wit, kk, vt, kki)


def _mix_kernel(h_ref, n_ref, ysb_ref, yd_ref, mod_ref, wps_ref, wpd_ref, wg_ref, bg_ref,
                wo_ref, o_ref):
    d = h_ref.shape[1]
    y_sb = _dot(ysb_ref[...], wps_ref[...])
    y_d = _dot(yd_ref[...], wpd_ref[...])
    gates = jax.nn.sigmoid(_dot(n_ref[...], wg_ref[...]) + bg_ref[...])
    merged = (gates[:, :d] * y_sb + gates[:, d:] * y_d).astype(BF16)
    o_ref[...] = h_ref[...] + mod_ref[5] * _dot(merged, wo_ref[...])


def _mix_call(h, n2, y_sb, y_d, mod, w_proj_sb, w_proj_dsa, w_gate, b_gate, w_out, *, seq, tm=512):
    n_tok, d = h.shape
    tok = lambda w: pl.BlockSpec((tm, w), lambda i: (i, 0))
    ws = [w.astype(BF16) for w in (w_proj_sb, w_proj_dsa, w_gate)]
    bg = b_gate.reshape(1, -1)
    wo = w_out.astype(BF16)
    return pl.pallas_call(
        _mix_kernel,
        out_shape=jax.ShapeDtypeStruct((n_tok, d), F32),
        grid=(n_tok // tm,),
        in_specs=[tok(d), tok(d), tok(W_HEADS), tok(W_HEADS), _mod_spec(seq // tm, d)]
                 + [_const_spec(w.shape) for w in (*ws, bg, wo)],
        out_specs=tok(d),
        compiler_params=_cparams(1),
        name="mix",
    )(h, n2, y_sb, y_d, mod, *ws, bg, wo)


def kernel(x, c, w_ada, b_ada, g_ffn1, w1_ffn1, w3_ffn1, w2_ffn1, g_mix, w_in, g_q_dsa, g_k_dsa,
           w_proj_sb, w_proj_dsa, w_gate, b_gate, w_out, g_ffn2, w1_ffn2, w3_ffn2, w2_ffn2):
    batch, seq, d = x.shape
    h = x.reshape(batch * seq, d)
    for l in range(w_ada.shape[0]):
        bf = lambda w: w[l].astype(BF16)
        mod = _mod_call(c, w_ada[l], b_ada[l])
        h, n2 = _ffn_call(h, mod, g_ffn1[l], bf(w1_ffn1), bf(w3_ffn1), bf(w2_ffn1), g_mix[l],
                          mod_base=0, emit_next=True, seq=seq)
        q_sb, k_sb, v_sb, q_d, kk, q_i, kki, vt, wit = _inproj_call(
            n2, w_in[l], g_q_dsa[l], g_k_dsa[l], batch, seq)
        y_sb = _sb_call(q_sb, k_sb, v_sb, batch, seq).reshape(batch * seq, W_HEADS)
        y_d = _dsa_call(q_d, q_i, wit, kk, vt, kki, batch, seq).reshape(batch * seq, W_HEADS)
        h = _mix_call(h, n2, y_sb, y_d, mod, w_proj_sb[l], w_proj_dsa[l], w_gate[l], b_gate[l],
                      w_out[l], seq=seq)
        (h,) = _ffn_call(h, mod, g_ffn2[l], bf(w1_ffn2), bf(w3_ffn2), bf(w2_ffn2), g_ffn2[l],
                         mod_base=6, emit_next=False, seq=seq)
    return h.reshape(batch, seq, d)
```

```python
import functools

import numpy as np
import jax
import jax.numpy as jnp
from jax import lax
from jax.experimental import pallas as pl
from jax.experimental.pallas import tpu as pltpu

F32 = jnp.float32
BF16 = jnp.bfloat16

HEAD_DIM = 64
N_HEADS = 8
W_HEADS = N_HEADS * HEAD_DIM
N_PAIRS = N_HEADS // 2
LANES = 128
SUBLANES = 8
BF16_SUBLANES = 16
BF16_MAX_EXACT_COUNT = 256
SELECT_WINDOW_BITS = 17
SELECT_EARLY_PASSES = 11
TOPK_MAX = 256
N_MOD = 9
RMS_EPS = 1e-6
D_IDX = 64

TQ = 128
TQ_DSA = 256
TK = 128
CH_S = 512
CH_H = 256
VT_ROWS = 80
LANE_POS_HI, LANE_POS_LO = HEAD_DIM, HEAD_DIM + 1
N_POS_LANES = 2
N_SLOPE_PARTS = 3
LOG2E = float(np.log2(np.e))
NEG_BIG = -1e30
F32_EXP_UNDERFLOW = 104.0
INT_MIN = np.int32(-2**31)
VMEM_LIMIT = 56 * 1024 * 1024


def _cparams(n_axes):
    return pltpu.CompilerParams(
        dimension_semantics=("arbitrary",) * n_axes, vmem_limit_bytes=VMEM_LIMIT)


def _const_spec(shape):
    nd = len(shape)
    return pl.BlockSpec(shape, lambda *_: (0,) * nd, pipeline_mode=pl.Buffered(1))


def _dot(a, b):
    return jnp.dot(a, b, preferred_element_type=F32)


def _dot_nt(a, b):
    return lax.dot_general(a, b, (((1,), (1,)), ((), ())), preferred_element_type=F32)


def _mod_kernel(c_ref, w_ref, b_ref, o_ref, sc_ref):
    nb, kdim, _ = c_ref.shape
    tn = w_ref.shape[1]
    nch = tn // LANES

    @pl.when(pl.program_id(0) == 0)
    def _():
        c = c_ref[...]
        sc_ref[...] = c * jax.nn.sigmoid(c)

    def body(kb, accs):
        k0 = pl.multiple_of(kb * SUBLANES, SUBLANES)
        wblk = w_ref[pl.ds(k0, SUBLANES), :]
        out = []
        for b in range(nb):
            sb = sc_ref[b, pl.ds(k0, SUBLANES), :]
            for ch in range(nch):
                out.append(accs[b * nch + ch] + wblk[:, ch * LANES:(ch + 1) * LANES] * sb)
        return tuple(out)

    accs = lax.fori_loop(0, kdim // SUBLANES, body,
                         tuple(jnp.zeros((SUBLANES, LANES), F32) for _ in range(nb * nch)))
    for b in range(nb):
        row = jnp.concatenate(
            [jnp.sum(accs[b * nch + ch], axis=0, keepdims=True) for ch in range(nch)], axis=1)
        o_ref[0, b] = row + b_ref[...]


def _mod_call(c, w_ada, b_ada):
    nb, d = c.shape
    n_out = w_ada.shape[1]
    c_rep = jnp.broadcast_to(c[:, :, None], (nb, d, LANES))
    return pl.pallas_call(
        _mod_kernel,
        out_shape=jax.ShapeDtypeStruct((n_out // d, nb, 1, d), F32),
        grid=(n_out // d,),
        in_specs=[_const_spec((nb, d, LANES)),
                  pl.BlockSpec((d, d), lambda j: (0, j)),
                  pl.BlockSpec((1, d), lambda j: (0, j))],
        out_specs=pl.BlockSpec((1, nb, 1, d), lambda j: (j, 0, 0, 0)),
        scratch_shapes=[pltpu.VMEM((nb, d, LANES), F32)],
        compiler_params=_cparams(1),
        name="mod",
    )(c_rep, w_ada, b_ada.reshape(1, n_out))


def _norm_mod(h, g, shift, scale):
    y = h * lax.rsqrt(jnp.mean(h * h, axis=-1, keepdims=True) + RMS_EPS)
    return (y * g) * (1.0 + scale) + shift


def _ffn_kernel(h_ref, mod_ref, g_ref, w1_ref, w3_ref, w2_ref, gn_ref, *out_refs,
                mod_base, emit_next, n_chunks):
    o_ref = out_refs[0]
    h = h_ref[...]
    n = _norm_mod(h, g_ref[...], mod_ref[mod_base], mod_ref[mod_base + 1]).astype(BF16)
    fc = w1_ref.shape[1] // n_chunks
    acc = jnp.zeros(h.shape, F32)
    for c in range(n_chunks):
        a = _dot(n, w1_ref[:, c * fc:(c + 1) * fc])
        b = _dot(n, w3_ref[:, c * fc:(c + 1) * fc])
        act = (a * jax.nn.sigmoid(a) * b).astype(BF16)
        acc = acc + _dot(act, w2_ref[c * fc:(c + 1) * fc, :])
    hn = h + (0.5 * mod_ref[mod_base + 2]) * acc
    o_ref[...] = hn
    if emit_next:
        out_refs[1][...] = _norm_mod(hn, gn_ref[...], mod_ref[mod_base + 3],
                                     mod_ref[mod_base + 4]).astype(BF16)


def _mod_spec(tiles_per_batch, d):
    return pl.BlockSpec((N_MOD, None, 1, d), lambda i: (0, i // tiles_per_batch, 0, 0))


def _ffn_call(h, mod, g, w1, w3, w2, g_next, *, mod_base, emit_next, seq, tm=512):
    n_tok, d = h.shape
    f = w1.shape[1]
    tok = pl.BlockSpec((tm, d), lambda i: (i, 0))
    out_shape = [jax.ShapeDtypeStruct((n_tok, d), F32)]
    out_specs = [tok]
    if emit_next:
        out_shape.append(jax.ShapeDtypeStruct((n_tok, d), BF16))
        out_specs.append(tok)
    return pl.pallas_call(
        functools.partial(_ffn_kernel, mod_base=mod_base, emit_next=emit_next, n_chunks=2),
        out_shape=out_shape,
        grid=(n_tok // tm,),
        in_specs=[tok, _mod_spec(seq // tm, d), _const_spec((1, d)),
                  _const_spec((d, f)), _const_spec((d, f)), _const_spec((f, d)),
                  _const_spec((1, d))],
        out_specs=out_specs,
        compiler_params=_cparams(1),
        name="ffn",
    )(h, mod, g.reshape(1, d), w1, w3, w2, g_next.reshape(1, d))


def _alibi_slope_parts():
    out = []
    for h in range(N_HEADS):
        rest, parts = np.float32(2.0 ** (-8.0 * (h + 1) / N_HEADS) * LOG2E), []
        for _ in range(N_SLOPE_PARTS):
            parts.append(np.float32(np.asarray(rest, dtype=BF16)))
            rest = np.float32(rest - parts[-1])
        out.append(parts)
    return out


def _store_head_rows(o_ref, x, slope_parts):
    tm = x.shape[0]
    lane = lax.broadcasted_iota(jnp.int32, (tm, LANES), 1)
    for h in range(N_HEADS):
        rows = x[:, (h // 2) * LANES:(h // 2 + 1) * LANES]
        if h % 2:
            rows = pltpu.roll(rows, HEAD_DIM, axis=1)
        rows = jnp.where(lane < HEAD_DIM, rows, 0.0)
        if slope_parts is not None:
            for rep, part in enumerate(slope_parts[h]):
                rows = jnp.where(lane == LANE_POS_HI + rep * N_POS_LANES, part * HEAD_DIM, rows)
                rows = jnp.where(lane == LANE_POS_LO + rep * N_POS_LANES, part, rows)
        o_ref[:, h] = rows.astype(BF16).reshape(tm // TQ_DSA, TQ_DSA, LANES)


def _inproj_kernel(n_ref, wsb_ref, wqd_ref, wkd_ref, wqi_ref, wki_ref, wvt_ref, wwt_ref,
                   gq_ref, gk_ref, hm_ref,
                   qsb_ref, ksb_ref, vsb_ref, qd_ref, kk_ref, qi_ref, kki_ref, vt_ref, wit_ref,
                   *, tiles_per_batch):
    n = n_ref[...]
    tm = n.shape[0]
    qk_scale = HEAD_DIM ** -0.5
    sb = _dot(n, wsb_ref[...])
    qsb_ref[...] = (sb[:, :W_HEADS] * qk_scale).astype(BF16)
    ksb_ref[...] = sb[:, W_HEADS:2 * W_HEADS].astype(BF16)
    vsb_ref[...] = sb[:, 2 * W_HEADS:].astype(BF16)

    qd = _dot(n, wqd_ref[...])
    ms = _dot((qd * qd).astype(BF16), hm_ref[...])
    _store_head_rows(qd_ref, qd * lax.rsqrt(ms + RMS_EPS) * (gq_ref[...] * (qk_scale * LOG2E)),
                     _alibi_slope_parts())

    kd = _dot(n, wkd_ref[...])
    kn = kd * lax.rsqrt(jnp.mean(kd * kd, axis=-1, keepdims=True) + RMS_EPS) * gk_ref[...]
    lane = lax.broadcasted_iota(jnp.int32, (tm, LANES), 1)
    pos = ((pl.program_id(0) % tiles_per_batch) * tm
           + lax.broadcasted_iota(jnp.int32, (tm, LANES), 0))
    feat = jnp.zeros((tm, LANES), jnp.int32)
    for rep in range(N_SLOPE_PARTS):
        feat = jnp.where(lane == LANE_POS_HI + rep * N_POS_LANES,
                         pos >> (HEAD_DIM.bit_length() - 1), feat)
        feat = jnp.where(lane == LANE_POS_LO + rep * N_POS_LANES, pos & (HEAD_DIM - 1), feat)
    feat = feat.astype(F32)
    kk_ref[...] = jnp.where(lane < HEAD_DIM, kn, feat).astype(BF16)

    _store_head_rows(qi_ref, _dot(n, wqi_ref[...]), None)
    kki_ref[...] = _dot(n, wki_ref[...]).astype(BF16)
    vt = _dot_nt(wvt_ref[...], n)
    row = lax.broadcasted_iota(jnp.int32, vt.shape, 0)
    vt_ref[...] = jnp.where(row == HEAD_DIM, 1.0, vt).astype(BF16)
    wit_ref[...] = _dot_nt(wwt_ref[...], n)[:N_HEADS]


def _inproj_call(n2, w_in, g_q, g_k, batch, seq, tm=512):
    n_tok, d = n2.shape
    sizes = (W_HEADS, W_HEADS, W_HEADS, W_HEADS, HEAD_DIM, HEAD_DIM, N_HEADS * D_IDX, D_IDX, N_HEADS)
    offs = np.concatenate([[0], np.cumsum(sizes)])
    col = lambda k: w_in[:, offs[k]:offs[k + 1]]
    idx_scale = (D_IDX ** -0.5) * (N_HEADS ** -0.5)
    w_sb = jnp.concatenate([col(0), col(1), col(2)], axis=1).astype(BF16)
    w_qd = col(3).astype(BF16)
    w_kd = jnp.concatenate([col(4), col(4)], axis=1).astype(BF16)
    w_qi = col(6).astype(BF16)
    w_ki = jnp.pad(col(7), ((0, 0), (0, LANES - D_IDX))).astype(BF16)
    w_vt = jnp.pad(col(5).T, ((0, VT_ROWS - HEAD_DIM), (0, 0))).astype(BF16)
    w_wt = jnp.pad((col(8) * idx_scale).T, ((0, 16 - N_HEADS), (0, 0))).astype(BF16)
    gq = jnp.tile(g_q, N_HEADS).reshape(1, W_HEADS)
    gk = jnp.tile(g_k, 2).reshape(1, LANES)
    head_of = np.arange(W_HEADS) // HEAD_DIM
    head_mean = jnp.asarray((head_of[:, None] == head_of[None, :]) / HEAD_DIM, BF16)

    tpb = seq // tm
    tok = lambda w: pl.BlockSpec((tm, w), lambda i: (i, 0))
    tr = lambda r: pl.BlockSpec((None, r, tm), lambda i: (i // tpb, 0, i % tpb))
    sd = lambda w, dt: jax.ShapeDtypeStruct((n_tok, w), dt)
    consts = (w_sb, w_qd, w_kd, w_qi, w_ki, w_vt, w_wt, gq, gk, head_mean)
    head_rows = jax.ShapeDtypeStruct((n_tok // TQ_DSA, N_HEADS, TQ_DSA, LANES), BF16)
    head_rows_spec = pl.BlockSpec((tm // TQ_DSA, N_HEADS, TQ_DSA, LANES), lambda i: (i, 0, 0, 0))
    return pl.pallas_call(
        functools.partial(_inproj_kernel, tiles_per_batch=tpb),
        out_shape=[sd(W_HEADS, BF16)] * 3 + [head_rows, sd(LANES, BF16), head_rows, sd(LANES, BF16),
                                             jax.ShapeDtypeStruct((batch, VT_ROWS, seq), BF16),
                                             jax.ShapeDtypeStruct((batch, N_HEADS, seq), F32)],
        grid=(n_tok // tm,),
        in_specs=[tok(d)] + [_const_spec(w.shape) for w in consts],
        out_specs=[tok(W_HEADS)] * 3 + [head_rows_spec, tok(LANES), head_rows_spec, tok(LANES),
                                        tr(VT_ROWS), tr(N_HEADS)],
        compiler_params=_cparams(1),
        name="inproj",
    )(n2, *consts)


def _split_heads(x, lane):
    xf = x.astype(F32)
    lo = jnp.where(lane < HEAD_DIM, xf, 0.0).astype(x.dtype)
    hi = jnp.where(lane >= HEAD_DIM, xf, 0.0).astype(x.dtype)
    return lo, hi


def _sb_kernel(q_ref, k_ref, v_ref, t_ref, o_ref, qs_ref, run_ref, acc_ref):
    i = pl.program_id(1)
    lane = lax.broadcasted_iota(jnp.int32, (TQ, LANES), 1)
    lane_k = lax.broadcasted_iota(jnp.int32, (TK, LANES), 1)
    for p in range(N_PAIRS):
        lo, hi = _split_heads(q_ref[:, p * LANES:(p + 1) * LANES], lane)
        qs_ref[p, :TQ, :] = lo
        qs_ref[p, TQ:, :] = hi

    def block(k0, diag):
        pairs = range(N_PAIRS)
        strict = (lax.broadcasted_iota(jnp.int32, (2 * TQ, TK), 1)
                  < lax.broadcasted_iota(jnp.int32, (2 * TQ, TK), 0) % TQ) if diag else None
        zs = [_dot_nt(qs_ref[p], k_ref[pl.ds(k0, TK), p * LANES:(p + 1) * LANES]) for p in pairs]
        lbs, ws = [], []
        for p in pairs:
            z = zs[p]
            sp = jnp.maximum(z, 0.0) + jnp.log(1.0 + jnp.exp(-jnp.abs(z)))
            l1m = jnp.where(strict, sp, 0.0) if diag else sp
            l_hi = l1m.astype(BF16)
            l_lo = (l1m - l_hi.astype(F32)).astype(BF16)
            ws.append(_dot(jnp.concatenate([l_hi, l_lo], axis=1), t_ref[...]))
            lbs.append(z - sp)
        for p in pairs:
            if diag:
                a = jnp.where(strict, jnp.exp(lbs[p] + ws[p][:, :TK]), 0.0).astype(BF16)
            else:
                a = jnp.exp(lbs[p] + ws[p][:, :TK] + run_ref[p]).astype(BF16)
            v_lo, v_hi = _split_heads(v_ref[pl.ds(k0, TK), p * LANES:(p + 1) * LANES], lane_k)
            upd = _dot(jnp.concatenate([a[:TQ], a[TQ:]], axis=1),
                       jnp.concatenate([v_lo, v_hi], axis=0))
            if diag:
                acc_ref[p] = upd
                run_ref[p] = ws[p][:, TK:]
            else:
                acc_ref[p] += upd
                run_ref[p] += ws[p][:, TK:]

    def more_blocks(jj):
        run_max = jnp.max(functools.reduce(jnp.maximum, [run_ref[p] for p in range(N_PAIRS)]))
        return jnp.logical_and(jj <= i, run_max > -F32_EXP_UNDERFLOW)

    block(pl.multiple_of(i * TK, TK), True)

    def body(state):
        jj, _ = state
        block(pl.multiple_of((i - jj) * TK, TK), False)
        return jj + 1, more_blocks(jj + 1)

    lax.while_loop(lambda state: state[1], body, (jnp.int32(1), i >= 1))
    for p in range(N_PAIRS):
        o_ref[:, p * LANES:(p + 1) * LANES] = acc_ref[p].astype(o_ref.dtype)


def _sb_call(q, k, v, batch, seq):
    q, k, v = (t.reshape(batch, seq, W_HEADS) for t in (q, k, v))
    j_idx = np.arange(TK)
    suffix = (j_idx[:, None] > j_idx[None, :]).astype(np.float32)
    tmat = -np.concatenate([suffix, np.ones((TK, LANES), np.float32)], axis=1)
    tmat = jnp.asarray(np.concatenate([tmat, tmat], axis=0), BF16)
    blk = pl.BlockSpec((None, TQ, W_HEADS), lambda b, i: (b, i, 0))
    allk = pl.BlockSpec((None, seq, W_HEADS), lambda b, i: (b, 0, 0))
    return pl.pallas_call(
        _sb_kernel,
        out_shape=jax.ShapeDtypeStruct((batch, seq, W_HEADS), BF16),
        grid=(batch, seq // TQ),
        in_specs=[blk, allk, allk, _const_spec(tmat.shape)],
        out_specs=blk,
        scratch_shapes=[pltpu.VMEM((N_PAIRS, 2 * TQ, LANES), BF16),
                        pltpu.VMEM((N_PAIRS, 2 * TQ, LANES), F32),
                        pltpu.VMEM((N_PAIRS, TQ, LANES), F32)],
        compiler_params=_cparams(2),
        name="sb",
    )(q, k, v, tmat)


def _dsa_kernel(qd_st, qi_st, wit_ref, kk_ref, vt_ref, kki_ref, o_ref,
                sc_ref, top_ref, m_ref, acc_ref, buf_ref, qit_ref, qdt_ref, *, topk):
    TQ = qd_st.shape[1]
    i = pl.program_id(1)
    t0 = i * TQ

    n_s = (i + CH_S // TQ) // (CH_S // TQ)

    for q_st, qt_ref in ((qi_st, qit_ref), (qd_st, qdt_ref)):
        for p in range(N_PAIRS):
            qt_ref[p] = jnp.concatenate(
                [q_st[h].astype(F32).T for h in (2 * p, 2 * p + 1)], axis=1).astype(BF16)

    def products(qt_ref, k_ref):
        def produce(k0):
            kch = k_ref[pl.ds(k0, CH_H), :]
            return [_dot(kch, qt_ref[p]) for p in range(N_PAIRS)]
        return produce

    def store_products(vals):
        for p in range(N_PAIRS):
            buf_ref[2 * p] = vals[p][:, :TQ]
            buf_ref[2 * p + 1] = vals[p][:, TQ:]

    def sweep(produce, consume):
        from_buf = lambda h: buf_ref[h]

        def trip(c, prefetch):
            k_a = pl.multiple_of(c * CH_S, CH_S)
            k_b = pl.multiple_of(k_a + CH_H, CH_H)
            vals_b = produce(k_b)
            consume(from_buf, k_a)
            if prefetch:
                vals_next = produce(pl.multiple_of(k_a + CH_S, CH_S))
            consume(lambda h: vals_b[h // 2][:, (h % 2) * TQ:(h % 2 + 1) * TQ], k_b)
            if prefetch:
                store_products(vals_next)

        store_products(produce(0))

        def body(c, _):
            trip(c, True)
            return 0

        lax.fori_loop(0, n_s - 1, body, 0)
        trip(n_s - 1, False)

    wit = wit_ref[...]

    def score_consume(prod, k0):
        score = None
        for h in range(N_HEADS):
            term = jnp.maximum(prod(h), 0.0) * wit[h:h + 1, :]
            score = term if score is None else score + term
        causal = (k0 + lax.broadcasted_iota(jnp.int32, (CH_H, TQ), 0)
                  <= t0 + lax.broadcasted_iota(jnp.int32, (CH_H, TQ), 1))
        score = jnp.where(causal, score, -jnp.inf)
        sc_ref[pl.ds(k0, CH_H), :] = score
        top_ref[pl.ds(k0, CH_H), :] = score.astype(BF16)

    sweep(products(qit_ref, kki_ref), score_consume)

    def float_of_rank(u):
        key = u ^ INT_MIN
        return pltpu.bitcast(jnp.where(key < 0, key ^ np.int32(0x7FFFFFFF), key), F32)

    def count_top_ge(cand):
        one, zero = jnp.ones((BF16_SUBLANES, TQ), BF16), jnp.zeros((BF16_SUBLANES, TQ), BF16)

        def cbody(c, acc):
            k0 = pl.multiple_of(c * CH_S, CH_S)
            tops = top_ref[pl.ds(k0, CH_S), :]
            parts = [jnp.where(tops[j * BF16_SUBLANES:(j + 1) * BF16_SUBLANES] >= cand, one, zero)
                     for j in range(CH_S // BF16_SUBLANES)]
            while len(parts) > 1:
                parts = [a + b for a, b in zip(parts[::2], parts[1::2])]
            return acc + parts[0]

        acc = lax.fori_loop(0, n_s, cbody, zero)
        return jnp.sum(acc.astype(F32), axis=0, keepdims=True)

    def top_body(b, u16):
        cand = u16 | jnp.left_shift(jnp.int32(1), 15 - b)
        cand_f = float_of_rank(jnp.left_shift(cand, 16)).astype(BF16)
        return jnp.where(count_top_ge(jnp.broadcast_to(cand_f, (BF16_SUBLANES, TQ))) >= topk,
                         cand, u16)

    u16 = lax.fori_loop(0, 16, top_body, jnp.zeros((1, TQ), jnp.int32))

    def count(pred):
        def cbody(c, acc):
            k0 = pl.multiple_of(c * CH_S, CH_S)
            sch = sc_ref[pl.ds(k0, CH_S), :].reshape(CH_S // SUBLANES, SUBLANES, TQ)
            return acc + jnp.sum(jnp.where(pred(sch), 1, 0), axis=0)
        acc = lax.fori_loop(0, n_s, cbody, jnp.zeros((SUBLANES, TQ), jnp.int32))
        tot = jnp.sum(acc.astype(F32), axis=0, keepdims=True)
        return jnp.broadcast_to(tot, (SUBLANES, TQ))

    base = jnp.broadcast_to(jnp.left_shift(u16, 16) - (1 << 15), (SUBLANES, TQ))

    def window_body(b, state):
        off, n_off = state
        cand = off | jnp.left_shift(jnp.int32(1), SELECT_WINDOW_BITS - 1 - b)
        cand_f = float_of_rank(base + cand)
        n_cand = count(lambda sch: sch >= cand_f[None])
        take = n_cand >= topk
        return jnp.where(take, cand, off), jnp.where(take, n_cand, n_off)

    few_keys = t0 + lax.broadcasted_iota(jnp.int32, (SUBLANES, TQ), 1) < topk
    unknown = jnp.full((SUBLANES, TQ), -1.0, F32)
    state = lax.fori_loop(0, SELECT_EARLY_PASSES, window_body,
                          (jnp.zeros((SUBLANES, TQ), jnp.int32), unknown))
    still_open = jnp.max(jnp.where(jnp.logical_or(state[1] == topk, few_keys), 0.0, 1.0))
    offset, n_off = lax.cond(
        still_open > 0.0,
        lambda st: lax.fori_loop(SELECT_EARLY_PASSES, SELECT_WINDOW_BITS, window_body, st),
        lambda st: st, state)
    thr = jnp.where(few_keys, jnp.finfo(F32).min, float_of_rank(base + offset))
    thr_row = thr[0:1, :]

    n_ge = jnp.where(few_keys, 0.0, jnp.where(n_off < 0.0, topk + 1.0, n_off))

    @pl.when(jnp.max(n_ge) > topk)
    def _():
        need = topk - count(lambda sch: sch > thr[None])[0:1, :]
        before = jnp.where(lax.broadcasted_iota(jnp.int32, (CH_H, CH_H), 1)
                           < lax.broadcasted_iota(jnp.int32, (CH_H, CH_H), 0), 1.0, 0.0).astype(BF16)

        def tie_body(c, seen):
            k0s = [pl.multiple_of(c * CH_S + j * CH_H, CH_H) for j in range(CH_S // CH_H)]
            schs = [sc_ref[pl.ds(k0, CH_H), :] for k0 in k0s]
            eqs = [jnp.where(sch == thr_row, 1.0, 0.0) for sch in schs]
            ranks = [_dot(before, eq.astype(BF16)) for eq in eqs]
            for k0, sch, eq, rank in zip(k0s, schs, eqs, ranks):
                tie_rank = jnp.where(sch == thr_row, rank + seen, -1.0)
                sc_ref[pl.ds(k0, CH_H), :] = jnp.where(tie_rank >= need, -jnp.inf, sch)
                seen = seen + jnp.sum(eq, axis=0, keepdims=True)
            return seen

        lax.fori_loop(0, n_s, tie_body, jnp.zeros((1, TQ), F32))

    m_ref[...] = jnp.full(m_ref.shape, NEG_BIG, F32)
    acc_ref[...] = jnp.zeros(acc_ref.shape, F32)

    def attn_consume(prod, k0):
        vch = vt_ref[:, pl.ds(k0, CH_H)]
        sel = sc_ref[pl.ds(k0, CH_H), :] >= thr_row
        for p in range(N_PAIRS):
            probs, alphas = [], []
            for h in (2 * p, 2 * p + 1):
                lg = jnp.where(sel, prod(h), NEG_BIG)
                m_old = m_ref[h]
                m_new = jnp.maximum(m_old, jnp.max(lg, axis=0, keepdims=True))
                m_ref[h] = m_new
                alphas.append(jnp.exp2(m_old - m_new))
                probs.append(jnp.exp2(lg - m_new).astype(BF16))
            upd = _dot(vch, jnp.concatenate(probs, axis=1))
            acc_ref[p] = acc_ref[p] * jnp.concatenate(alphas, axis=1) + upd

    sweep(products(qdt_ref, kk_ref), attn_consume)
    for p in range(N_PAIRS):
        acc = acc_ref[p]
        out_t = acc[:HEAD_DIM] / acc[HEAD_DIM:HEAD_DIM + 1]
        pair = jnp.concatenate([out_t[:, :TQ], out_t[:, TQ:]], axis=0)
        o_ref[:, p * LANES:(p + 1) * LANES] = pair.T.astype(o_ref.dtype)


def _dsa_call(qd_rows, qi_rows, wit, kk, vt, kki, batch, seq):
    r3 = lambda t: t.reshape(batch, seq, t.shape[-1])
    kk, kki = map(r3, (kk, kki))
    topk = min(TOPK_MAX, seq // 4)
    assert seq // BF16_SUBLANES <= BF16_MAX_EXACT_COUNT
    TQ = TQ_DSA
    nq = seq // TQ
    rows = pl.BlockSpec((None, N_HEADS, TQ, LANES), lambda b, i: (b * nq + i, 0, 0, 0))
    allk = pl.BlockSpec((None, seq, LANES), lambda b, i: (b, 0, 0))
    return pl.pallas_call(
        functools.partial(_dsa_kernel, topk=topk),
        out_shape=jax.ShapeDtypeStruct((batch, seq, W_HEADS), BF16),
        grid=(batch, nq),
        in_specs=[rows, rows,
                  pl.BlockSpec((None, N_HEADS, TQ), lambda b, i: (b, 0, i)),
                  allk, pl.BlockSpec((None, VT_ROWS, seq), lambda b, i: (b, 0, 0)), allk],
        out_specs=pl.BlockSpec((None, TQ, W_HEADS), lambda b, i: (b, i, 0)),
        scratch_shapes=[pltpu.VMEM((seq, TQ), F32),
                        pltpu.VMEM((seq, TQ), BF16),
                        pltpu.VMEM((N_HEADS, 1, TQ), F32),
                        pltpu.VMEM((N_PAIRS, VT_ROWS, 2 * TQ), F32),
                        pltpu.VMEM((N_HEADS, CH_H, TQ), F32),
                        pltpu.VMEM((N_PAIRS, LANES, 2 * TQ), BF16),
                        pltpu.VMEM((N_PAIRS, LANES, 2 * TQ), BF16)],
        compiler_params=_cparams(2),
        name="dsa",
    )(qd_rows, qi_rows, wit, kk, vt, kki)


def _mix_kernel(h_ref, n_ref, ysb_ref, yd_ref, mod_ref, wps_ref, wpd_ref, wg_ref, bg_ref,
                wo_ref, o_ref):
    d = h_ref.shape[1]
    y_sb = _dot(ysb_ref[...], wps_ref[...])
    y_d = _dot(yd_ref[...], wpd_ref[...])
    gates = jax.nn.sigmoid(_dot(n_ref[...], wg_ref[...]) + bg_ref[...])
    merged = (gates[:, :d] * y_sb + gates[:, d:] * y_d).astype(BF16)
    o_ref[...] = h_ref[...] + mod_ref[5] * _dot(merged, wo_ref[...])


def _mix_call(h, n2, y_sb, y_d, mod, w_proj_sb, w_proj_dsa, w_gate, b_gate, w_out, *, seq, tm=512):
    n_tok, d = h.shape
    tok = lambda w: pl.BlockSpec((tm, w), lambda i: (i, 0))
    ws = [w.astype(BF16) for w in (w_proj_sb, w_proj_dsa, w_gate)]
    bg = b_gate.reshape(1, -1)
    wo = w_out.astype(BF16)
    return pl.pallas_call(
        _mix_kernel,
        out_shape=jax.ShapeDtypeStruct((n_tok, d), F32),
        grid=(n_tok // tm,),
        in_specs=[tok(d), tok(d), tok(W_HEADS), tok(W_HEADS), _mod_spec(seq // tm, d)]
                 + [_const_spec(w.shape) for w in (*ws, bg, wo)],
        out_specs=tok(d),
        compiler_params=_cparams(1),
        name="mix",
    )(h, n2, y_sb, y_d, mod, *ws, bg, wo)


def kernel(x, c, w_ada, b_ada, g_ffn1, w1_ffn1, w3_ffn1, w2_ffn1, g_mix, w_in, g_q_dsa, g_k_dsa,
           w_proj_sb, w_proj_dsa, w_gate, b_gate, w_out, g_ffn2, w1_ffn2, w3_ffn2, w2_ffn2):
    batch, seq, d = x.shape
    h = x.reshape(batch * seq, d)
    for l in range(w_ada.shape[0]):
        bf = lambda w: w[l].astype(BF16)
        mod = _mod_call(c, w_ada[l], b_ada[l])
        h, n2 = _ffn_call(h, mod, g_ffn1[l], bf(w1_ffn1), bf(w3_ffn1), bf(w2_ffn1), g_mix[l],
                          mod_base=0, emit_next=True, seq=seq)
        q_sb, k_sb, v_sb, q_d, kk, q_i, kki, vt, wit = _inproj_call(
            n2, w_in[l], g_q_dsa[l], g_k_dsa[l], batch, seq)
        y_sb = _sb_call(q_sb, k_sb, v_sb, batch, seq).reshape(batch * seq, W_HEADS)
        y_d = _dsa_call(q_d, q_i, wit, kk, vt, kki, batch, seq).reshape(batch * seq, W_HEADS)
        h = _mix_call(h, n2, y_sb, y_d, mod, w_proj_sb[l], w_proj_dsa[l], w_gate[l], b_gate[l],
                      w_out[l], seq=seq)
        (h,) = _ffn_call(h, mod, g_ffn2[l], bf(w1_ffn2), bf(w3_ffn2), bf(w2_ffn2), g_ffn2[l],
                         mod_base=6, emit_next=False, seq=seq)
    return h.reshape(batch, seq, d)
```

```python
import functools

import numpy as np
import jax
import jax.numpy as jnp
from jax import lax
from jax.experimental import pallas as pl
from jax.experimental.pallas import tpu as pltpu

F32 = jnp.float32
BF16 = jnp.bfloat16

HEAD_DIM = 64
N_HEADS = 8
W_HEADS = N_HEADS * HEAD_DIM
N_PAIRS = N_HEADS // 2
LANES = 128
SUBLANES = 8
BF16_SUBLANES = 16
BF16_MAX_EXACT_COUNT = 256
SELECT_WINDOW_BITS = 17
SELECT_EARLY_PASSES = 11
TOPK_MAX = 256
N_MOD = 9
RMS_EPS = 1e-6
D_IDX = 64

TQ_SB = 256
TQ_DSA = 512
TK = 128
CH_S = 512
CH_H = 256
VT_ROWS = 80
LANE_POS_HI, LANE_POS_LO = HEAD_DIM, HEAD_DIM + 1
N_POS_LANES = 2
N_SLOPE_PARTS = 3
LOG2E = float(np.log2(np.e))
NEG_BIG = -1e30
F32_EXP_UNDERFLOW = 104.0
INT_MIN = np.int32(-2**31)
VMEM_LIMIT = 56 * 1024 * 1024


def _cparams(n_axes):
    return pltpu.CompilerParams(
        dimension_semantics=("arbitrary",) * n_axes, vmem_limit_bytes=VMEM_LIMIT)


def _const_spec(shape):
    nd = len(shape)
    return pl.BlockSpec(shape, lambda *_: (0,) * nd, pipeline_mode=pl.Buffered(1))


def _dot(a, b):
    return jnp.dot(a, b, preferred_element_type=F32)


def _dot_nt(a, b):
    return lax.dot_general(a, b, (((1,), (1,)), ((), ())), preferred_element_type=F32)


def _mod_kernel(c_ref, w_ref, b_ref, o_ref, sc_ref):
    nb, kdim, _ = c_ref.shape
    tn = w_ref.shape[1]
    nch = tn // LANES

    @pl.when(pl.program_id(0) == 0)
    def _():
        c = c_ref[...]
        sc_ref[...] = c * jax.nn.sigmoid(c)

    def body(kb, accs):
        k0 = pl.multiple_of(kb * SUBLANES, SUBLANES)
        wblk = w_ref[pl.ds(k0, SUBLANES), :]
        out = []
        for b in range(nb):
            sb = sc_ref[b, pl.ds(k0, SUBLANES), :]
            for ch in range(nch):
                out.append(accs[b * nch + ch] + wblk[:, ch * LANES:(ch + 1) * LANES] * sb)
        return tuple(out)

    accs = lax.fori_loop(0, kdim // SUBLANES, body,
                         tuple(jnp.zeros((SUBLANES, LANES), F32) for _ in range(nb * nch)))
    for b in range(nb):
        row = jnp.concatenate(
            [jnp.sum(accs[b * nch + ch], axis=0, keepdims=True) for ch in range(nch)], axis=1)
        o_ref[0, b] = row + b_ref[...]


def _mod_call(c, w_ada, b_ada):
    nb, d = c.shape
    n_out = w_ada.shape[1]
    c_rep = jnp.broadcast_to(c[:, :, None], (nb, d, LANES))
    return pl.pallas_call(
        _mod_kernel,
        out_shape=jax.ShapeDtypeStruct((n_out // d, nb, 1, d), F32),
        grid=(n_out // d,),
        in_specs=[_const_spec((nb, d, LANES)),
                  pl.BlockSpec((d, d), lambda j: (0, j)),
                  pl.BlockSpec((1, d), lambda j: (0, j))],
        out_specs=pl.BlockSpec((1, nb, 1, d), lambda j: (j, 0, 0, 0)),
        scratch_shapes=[pltpu.VMEM((nb, d, LANES), F32)],
        compiler_params=_cparams(1),
        name="mod",
    )(c_rep, w_ada, b_ada.reshape(1, n_out))


def _norm_mod(h, g, shift, scale):
    y = h * lax.rsqrt(jnp.mean(h * h, axis=-1, keepdims=True) + RMS_EPS)
    return (y * g) * (1.0 + scale) + shift


def _ffn_kernel(h_ref, mod_ref, g_ref, w1_ref, w3_ref, w2_ref, gn_ref, *out_refs,
                mod_base, emit_next, n_chunks):
    o_ref = out_refs[0]
    h = h_ref[...]
    n = _norm_mod(h, g_ref[...], mod_ref[mod_base], mod_ref[mod_base + 1]).astype(BF16)
    fc = w1_ref.shape[1] // n_chunks
    acc = jnp.zeros(h.shape, F32)
    for c in range(n_chunks):
        a = _dot(n, w1_ref[:, c * fc:(c + 1) * fc])
        b = _dot(n, w3_ref[:, c * fc:(c + 1) * fc])
        act = (a * jax.nn.sigmoid(a) * b).astype(BF16)
        acc = acc + _dot(act, w2_ref[c * fc:(c + 1) * fc, :])
    hn = h + (0.5 * mod_ref[mod_base + 2]) * acc
    o_ref[...] = hn
    if emit_next:
        out_refs[1][...] = _norm_mod(hn, gn_ref[...], mod_ref[mod_base + 3],
                                     mod_ref[mod_base + 4]).astype(BF16)


def _mod_spec(tiles_per_batch, d):
    return pl.BlockSpec((N_MOD, None, 1, d), lambda i: (0, i // tiles_per_batch, 0, 0))


def _ffn_call(h, mod, g, w1, w3, w2, g_next, *, mod_base, emit_next, seq, tm=512):
    n_tok, d = h.shape
    f = w1.shape[1]
    tok = pl.BlockSpec((tm, d), lambda i: (i, 0))
    out_shape = [jax.ShapeDtypeStruct((n_tok, d), F32)]
    out_specs = [tok]
    if emit_next:
        out_shape.append(jax.ShapeDtypeStruct((n_tok, d), BF16))
        out_specs.append(tok)
    return pl.pallas_call(
        functools.partial(_ffn_kernel, mod_base=mod_base, emit_next=emit_next, n_chunks=2),
        out_shape=out_shape,
        grid=(n_tok // tm,),
        in_specs=[tok, _mod_spec(seq // tm, d), _const_spec((1, d)),
                  _const_spec((d, f)), _const_spec((d, f)), _const_spec((f, d)),
                  _const_spec((1, d))],
        out_specs=out_specs,
        compiler_params=_cparams(1),
        name="ffn",
    )(h, mod, g.reshape(1, d), w1, w3, w2, g_next.reshape(1, d))


def _alibi_slope_parts():
    out = []
    for h in range(N_HEADS):
        rest, parts = np.float32(2.0 ** (-8.0 * (h + 1) / N_HEADS) * LOG2E), []
        for _ in range(N_SLOPE_PARTS):
            parts.append(np.float32(np.asarray(rest, dtype=BF16)))
            rest = np.float32(rest - parts[-1])
        out.append(parts)
    return out


def _store_head_rows(o_ref, x, slope_parts):
    tm = x.shape[0]
    lane = lax.broadcasted_iota(jnp.int32, (tm, LANES), 1)
    for h in range(N_HEADS):
        rows = x[:, (h // 2) * LANES:(h // 2 + 1) * LANES]
        if h % 2:
            rows = pltpu.roll(rows, HEAD_DIM, axis=1)
        rows = jnp.where(lane < HEAD_DIM, rows, 0.0)
        if slope_parts is not None:
            for rep, part in enumerate(slope_parts[h]):
                rows = jnp.where(lane == LANE_POS_HI + rep * N_POS_LANES, part * HEAD_DIM, rows)
                rows = jnp.where(lane == LANE_POS_LO + rep * N_POS_LANES, part, rows)
        o_ref[:, h] = rows.astype(BF16).reshape(tm // TQ_DSA, TQ_DSA, LANES)


def _inproj_kernel(n_ref, wsb_ref, wqd_ref, wkd_ref, wqi_ref, wki_ref, wvt_ref, wwt_ref,
                   gq_ref, gk_ref, hm_ref,
                   qsb_ref, ksb_ref, vsb_ref, qd_ref, kk_ref, qi_ref, kki_ref, vt_ref, wit_ref,
                   *, tiles_per_batch):
    n = n_ref[...]
    tm = n.shape[0]
    qk_scale = HEAD_DIM ** -0.5
    sb = _dot(n, wsb_ref[...])
    qsb_ref[...] = (sb[:, :W_HEADS] * qk_scale).astype(BF16)
    ksb_ref[...] = sb[:, W_HEADS:2 * W_HEADS].astype(BF16)
    vsb_ref[...] = sb[:, 2 * W_HEADS:].astype(BF16)

    qd = _dot(n, wqd_ref[...])
    ms = _dot((qd * qd).astype(BF16), hm_ref[...])
    _store_head_rows(qd_ref, qd * lax.rsqrt(ms + RMS_EPS) * (gq_ref[...] * (qk_scale * LOG2E)),
                     _alibi_slope_parts())

    kd = _dot(n, wkd_ref[...])
    kn = kd * lax.rsqrt(jnp.mean(kd * kd, axis=-1, keepdims=True) + RMS_EPS) * gk_ref[...]
    lane = lax.broadcasted_iota(jnp.int32, (tm, LANES), 1)
    pos = ((pl.program_id(0) % tiles_per_batch) * tm
           + lax.broadcasted_iota(jnp.int32, (tm, LANES), 0))
    feat = jnp.zeros((tm, LANES), jnp.int32)
    for rep in range(N_SLOPE_PARTS):
        feat = jnp.where(lane == LANE_POS_HI + rep * N_POS_LANES,
                         pos >> (HEAD_DIM.bit_length() - 1), feat)
        feat = jnp.where(lane == LANE_POS_LO + rep * N_POS_LANES, pos & (HEAD_DIM - 1), feat)
    feat = feat.astype(F32)
    kk_ref[...] = jnp.where(lane < HEAD_DIM, kn, feat).astype(BF16)

    _store_head_rows(qi_ref, _dot(n, wqi_ref[...]), None)
    kki_ref[...] = _dot(n, wki_ref[...]).astype(BF16)
    vt = _dot_nt(wvt_ref[...], n)
    row = lax.broadcasted_iota(jnp.int32, vt.shape, 0)
    vt_ref[...] = jnp.where(row == HEAD_DIM, 1.0, vt).astype(BF16)
    wit_ref[...] = _dot_nt(wwt_ref[...], n)[:N_HEADS]


def _inproj_call(n2, w_in, g_q, g_k, batch, seq, tm=512):
    n_tok, d = n2.shape
    sizes = (W_HEADS, W_HEADS, W_HEADS, W_HEADS, HEAD_DIM, HEAD_DIM, N_HEADS * D_IDX, D_IDX, N_HEADS)
    offs = np.concatenate([[0], np.cumsum(sizes)])
    col = lambda k: w_in[:, offs[k]:offs[k + 1]]
    idx_scale = (D_IDX ** -0.5) * (N_HEADS ** -0.5)
    w_sb = jnp.concatenate([col(0), col(1), col(2)], axis=1).astype(BF16)
    w_qd = col(3).astype(BF16)
    w_kd = jnp.concatenate([col(4), col(4)], axis=1).astype(BF16)
    w_qi = col(6).astype(BF16)
    w_ki = jnp.pad(col(7), ((0, 0), (0, LANES - D_IDX))).astype(BF16)
    w_vt = jnp.pad(col(5).T, ((0, VT_ROWS - HEAD_DIM), (0, 0))).astype(BF16)
    w_wt = jnp.pad((col(8) * idx_scale).T, ((0, 16 - N_HEADS), (0, 0))).astype(BF16)
    gq = jnp.tile(g_q, N_HEADS).reshape(1, W_HEADS)
    gk = jnp.tile(g_k, 2).reshape(1, LANES)
    head_of = np.arange(W_HEADS) // HEAD_DIM
    head_mean = jnp.asarray((head_of[:, None] == head_of[None, :]) / HEAD_DIM, BF16)

    tpb = seq // tm
    tok = lambda w: pl.BlockSpec((tm, w), lambda i: (i, 0))
    tr = lambda r: pl.BlockSpec((None, r, tm), lambda i: (i // tpb, 0, i % tpb))
    sd = lambda w, dt: jax.ShapeDtypeStruct((n_tok, w), dt)
    consts = (w_sb, w_qd, w_kd, w_qi, w_ki, w_vt, w_wt, gq, gk, head_mean)
    head_rows = jax.ShapeDtypeStruct((n_tok // TQ_DSA, N_HEADS, TQ_DSA, LANES), BF16)
    head_rows_spec = pl.BlockSpec((tm // TQ_DSA, N_HEADS, TQ_DSA, LANES), lambda i: (i, 0, 0, 0))
    return pl.pallas_call(
        functools.partial(_inproj_kernel, tiles_per_batch=tpb),
        out_shape=[sd(W_HEADS, BF16)] * 3 + [head_rows, sd(LANES, BF16), head_rows, sd(LANES, BF16),
                                             jax.ShapeDtypeStruct((batch, VT_ROWS, seq), BF16),
                                             jax.ShapeDtypeStruct((batch, N_HEADS, seq), F32)],
        grid=(n_tok // tm,),
        in_specs=[tok(d)] + [_const_spec(w.shape) for w in consts],
        out_specs=[tok(W_HEADS)] * 3 + [head_rows_spec, tok(LANES), head_rows_spec, tok(LANES),
                                        tr(VT_ROWS), tr(N_HEADS)],
        compiler_params=_cparams(1),
        name="inproj",
    )(n2, *consts)


def _split_heads(x, lane):
    xf = x.astype(F32)
    lo = jnp.where(lane < HEAD_DIM, xf, 0.0).astype(x.dtype)
    hi = jnp.where(lane >= HEAD_DIM, xf, 0.0).astype(x.dtype)
    return lo, hi


def _sb_kernel(q_ref, k_ref, v_ref, t_ref, o_ref, qs_ref, run_ref, acc_ref):
    tq = q_ref.shape[0]
    n_diag = tq // TK
    i = pl.program_id(1)
    lane = lax.broadcasted_iota(jnp.int32, (tq, LANES), 1)
    lane_k = lax.broadcasted_iota(jnp.int32, (TK, LANES), 1)
    for p in range(N_PAIRS):
        lo, hi = _split_heads(q_ref[:, p * LANES:(p + 1) * LANES], lane)
        qs_ref[p, :tq, :] = lo
        qs_ref[p, tq:, :] = hi

    def block(k0, key_offset, first):
        pairs = range(N_PAIRS)
        masked = key_offset is not None
        strict = (lax.broadcasted_iota(jnp.int32, (2 * tq, TK), 1) + key_offset
                  < lax.broadcasted_iota(jnp.int32, (2 * tq, TK), 0) % tq) if masked else None
        zs = [_dot_nt(qs_ref[p], k_ref[pl.ds(k0, TK), p * LANES:(p + 1) * LANES]) for p in pairs]
        lbs, ws = [], []
        for p in pairs:
            z = zs[p]
            sp = jnp.maximum(z, 0.0) + jnp.log(1.0 + jnp.exp(-jnp.abs(z)))
            l1m = jnp.where(strict, sp, 0.0) if masked else sp
            l_hi = l1m.astype(BF16)
            l_lo = (l1m - l_hi.astype(F32)).astype(BF16)
            ws.append(_dot(jnp.concatenate([l_hi, l_lo], axis=1), t_ref[...]))
            lbs.append(z - sp)
        for p in pairs:
            arg = lbs[p] + ws[p][:, :TK]
            if not first:
                arg = arg + run_ref[p]
            a = jnp.exp(arg)
            if masked:
                a = jnp.where(strict, a, 0.0)
            a = a.astype(BF16)
            v_lo, v_hi = _split_heads(v_ref[pl.ds(k0, TK), p * LANES:(p + 1) * LANES], lane_k)
            upd = _dot(jnp.concatenate([a[:tq], a[tq:]], axis=1),
                       jnp.concatenate([v_lo, v_hi], axis=0))
            if first:
                acc_ref[p] = upd
                run_ref[p] = ws[p][:, TK:]
            else:
                acc_ref[p] += upd
                run_ref[p] += ws[p][:, TK:]

    last = (i + 1) * n_diag - 1
    for jj in range(n_diag):
        block(pl.multiple_of((last - jj) * TK, TK), (n_diag - 1 - jj) * TK, jj == 0)

    def more_blocks(jj):
        run_max = jnp.max(functools.reduce(jnp.maximum, [run_ref[p] for p in range(N_PAIRS)]))
        return jnp.logical_and(jj <= last, run_max > -F32_EXP_UNDERFLOW)

    def body(state):
        jj, _ = state
        block(pl.multiple_of((last - jj) * TK, TK), None, False)
        return jj + 1, more_blocks(jj + 1)

    lax.while_loop(lambda state: state[1], body, (jnp.int32(n_diag), i >= 1))
    for p in range(N_PAIRS):
        o_ref[:, p * LANES:(p + 1) * LANES] = acc_ref[p].astype(o_ref.dtype)


def _sb_call(q, k, v, batch, seq):
    q, k, v = (t.reshape(batch, seq, W_HEADS) for t in (q, k, v))
    j_idx = np.arange(TK)
    suffix = (j_idx[:, None] > j_idx[None, :]).astype(np.float32)
    tmat = -np.concatenate([suffix, np.ones((TK, LANES), np.float32)], axis=1)
    tmat = jnp.asarray(np.concatenate([tmat, tmat], axis=0), BF16)
    blk = pl.BlockSpec((None, TQ_SB, W_HEADS), lambda b, i: (b, i, 0))
    allk = pl.BlockSpec((None, seq, W_HEADS), lambda b, i: (b, 0, 0))
    return pl.pallas_call(
        _sb_kernel,
        out_shape=jax.ShapeDtypeStruct((batch, seq, W_HEADS), BF16),
        grid=(batch, seq // TQ_SB),
        in_specs=[blk, allk, allk, _const_spec(tmat.shape)],
        out_specs=blk,
        scratch_shapes=[pltpu.VMEM((N_PAIRS, 2 * TQ_SB, LANES), BF16),
                        pltpu.VMEM((N_PAIRS, 2 * TQ_SB, LANES), F32),
                        pltpu.VMEM((N_PAIRS, TQ_SB, LANES), F32)],
        compiler_params=_cparams(2),
        name="sb",
    )(q, k, v, tmat)


def _dsa_kernel(qd_st, qi_st, wit_ref, kk_ref, vt_ref, kki_ref, o_ref,
                sc_ref, top_ref, m_ref, acc_ref, buf_ref, qit_ref, qdt_ref, *, topk):
    TQ = qd_st.shape[1]
    i = pl.program_id(1)
    t0 = i * TQ

    n_s = (i + CH_S // TQ) // (CH_S // TQ)

    for q_st, qt_ref in ((qi_st, qit_ref), (qd_st, qdt_ref)):
        for p in range(N_PAIRS):
            qt_ref[p] = jnp.concatenate(
                [q_st[h].astype(F32).T for h in (2 * p, 2 * p + 1)], axis=1).astype(BF16)

    def products(qt_ref, k_ref):
        def produce(k0):
            kch = k_ref[pl.ds(k0, CH_H), :]
            return [_dot(kch, qt_ref[p]) for p in range(N_PAIRS)]
        return produce

    def store_products(vals):
        for p in range(N_PAIRS):
            buf_ref[2 * p] = vals[p][:, :TQ]
            buf_ref[2 * p + 1] = vals[p][:, TQ:]

    def sweep(produce, consume):
        from_buf = lambda h: buf_ref[h]

        def trip(c, prefetch):
            k_a = pl.multiple_of(c * CH_S, CH_S)
            k_b = pl.multiple_of(k_a + CH_H, CH_H)
            vals_b = produce(k_b)
            consume(from_buf, k_a)
            if prefetch:
                vals_next = produce(pl.multiple_of(k_a + CH_S, CH_S))
            consume(lambda h: vals_b[h // 2][:, (h % 2) * TQ:(h % 2 + 1) * TQ], k_b)
            if prefetch:
                store_products(vals_next)

        store_products(produce(0))

        def body(c, _):
            trip(c, True)
            return 0

        lax.fori_loop(0, n_s - 1, body, 0)
        trip(n_s - 1, False)

    wit = wit_ref[...]

    def score_consume(prod, k0):
        score = None
        for h in range(N_HEADS):
            term = jnp.maximum(prod(h), 0.0) * wit[h:h + 1, :]
            score = term if score is None else score + term
        causal = (k0 + lax.broadcasted_iota(jnp.int32, (CH_H, TQ), 0)
                  <= t0 + lax.broadcasted_iota(jnp.int32, (CH_H, TQ), 1))
        score = jnp.where(causal, score, -jnp.inf)
        sc_ref[pl.ds(k0, CH_H), :] = score
        top_ref[pl.ds(k0, CH_H), :] = score.astype(BF16)

    sweep(products(qit_ref, kki_ref), score_consume)

    def float_of_rank(u):
        key = u ^ INT_MIN
        return pltpu.bitcast(jnp.where(key < 0, key ^ np.int32(0x7FFFFFFF), key), F32)

    def count_top_ge(cand):
        one, zero = jnp.ones((BF16_SUBLANES, TQ), BF16), jnp.zeros((BF16_SUBLANES, TQ), BF16)

        def cbody(c, acc):
            k0 = pl.multiple_of(c * CH_S, CH_S)
            tops = top_ref[pl.ds(k0, CH_S), :]
            parts = [jnp.where(tops[j * BF16_SUBLANES:(j + 1) * BF16_SUBLANES] >= cand, one, zero)
                     for j in range(CH_S // BF16_SUBLANES)]
            while len(parts) > 1:
                parts = [a + b for a, b in zip(parts[::2], parts[1::2])]
            return acc + parts[0]

        acc = lax.fori_loop(0, n_s, cbody, zero)
        return jnp.sum(acc.astype(F32), axis=0, keepdims=True)

    def top_body(b, u16):
        cand = u16 | jnp.left_shift(jnp.int32(1), 15 - b)
        cand_f = float_of_rank(jnp.left_shift(cand, 16)).astype(BF16)
        return jnp.where(count_top_ge(jnp.broadcast_to(cand_f, (BF16_SUBLANES, TQ))) >= topk,
                         cand, u16)

    u16 = lax.fori_loop(0, 16, top_body, jnp.zeros((1, TQ), jnp.int32))

    def count(pred):
        def cbody(c, acc):
            k0 = pl.multiple_of(c * CH_S, CH_S)
            sch = sc_ref[pl.ds(k0, CH_S), :].reshape(CH_S // SUBLANES, SUBLANES, TQ)
            return acc + jnp.sum(jnp.where(pred(sch), 1, 0), axis=0)
        acc = lax.fori_loop(0, n_s, cbody, jnp.zeros((SUBLANES, TQ), jnp.int32))
        tot = jnp.sum(acc.astype(F32), axis=0, keepdims=True)
        return jnp.broadcast_to(tot, (SUBLANES, TQ))

    base = jnp.broadcast_to(jnp.left_shift(u16, 16) - (1 << 15), (SUBLANES, TQ))

    def window_body(b, state):
        off, n_off = state
        cand = off | jnp.left_shift(jnp.int32(1), SELECT_WINDOW_BITS - 1 - b)
        cand_f = float_of_rank(base + cand)
        n_cand = count(lambda sch: sch >= cand_f[None])
        take = n_cand >= topk
        return jnp.where(take, cand, off), jnp.where(take, n_cand, n_off)

    few_keys = t0 + lax.broadcasted_iota(jnp.int32, (SUBLANES, TQ), 1) < topk
    unknown = jnp.full((SUBLANES, TQ), -1.0, F32)
    state = lax.fori_loop(0, SELECT_EARLY_PASSES, window_body,
                          (jnp.zeros((SUBLANES, TQ), jnp.int32), unknown))
    still_open = jnp.max(jnp.where(jnp.logical_or(state[1] == topk, few_keys), 0.0, 1.0))
    offset, n_off = lax.cond(
        still_open > 0.0,
        lambda st: lax.fori_loop(SELECT_EARLY_PASSES, SELECT_WINDOW_BITS, window_body, st),
        lambda st: st, state)
    thr = jnp.where(few_keys, jnp.finfo(F32).min, float_of_rank(base + offset))
    thr_row = thr[0:1, :]

    n_ge = jnp.where(few_keys, 0.0, jnp.where(n_off < 0.0, topk + 1.0, n_off))

    @pl.when(jnp.max(n_ge) > topk)
    def _():
        need = topk - count(lambda sch: sch > thr[None])[0:1, :]
        before = jnp.where(lax.broadcasted_iota(jnp.int32, (CH_H, CH_H), 1)
                           < lax.broadcasted_iota(jnp.int32, (CH_H, CH_H), 0), 1.0, 0.0).astype(BF16)

        def tie_body(c, seen):
            k0s = [pl.multiple_of(c * CH_S + j * CH_H, CH_H) for j in range(CH_S // CH_H)]
            schs = [sc_ref[pl.ds(k0, CH_H), :] for k0 in k0s]
            eqs = [jnp.where(sch == thr_row, 1.0, 0.0) for sch in schs]
            ranks = [_dot(before, eq.astype(BF16)) for eq in eqs]
            for k0, sch, eq, rank in zip(k0s, schs, eqs, ranks):
                tie_rank = jnp.where(sch == thr_row, rank + seen, -1.0)
                sc_ref[pl.ds(k0, CH_H), :] = jnp.where(tie_rank >= need, -jnp.inf, sch)
                seen = seen + jnp.sum(eq, axis=0, keepdims=True)
            return seen

        lax.fori_loop(0, n_s, tie_body, jnp.zeros((1, TQ), F32))

    m_ref[...] = jnp.full(m_ref.shape, NEG_BIG, F32)
    acc_ref[...] = jnp.zeros(acc_ref.shape, F32)

    def attn_consume(prod, k0):
        vch = vt_ref[:, pl.ds(k0, CH_H)]
        sel = sc_ref[pl.ds(k0, CH_H), :] >= thr_row
        for p in range(N_PAIRS):
            probs, alphas = [], []
            for h in (2 * p, 2 * p + 1):
                lg = jnp.where(sel, prod(h), NEG_BIG)
                m_old = m_ref[h]
                m_new = jnp.maximum(m_old, jnp.max(lg, axis=0, keepdims=True))
                m_ref[h] = m_new
                alphas.append(jnp.exp2(m_old - m_new))
                probs.append(jnp.exp2(lg - m_new).astype(BF16))
            upd = _dot(vch, jnp.concatenate(probs, axis=1))
            acc_ref[p] = acc_ref[p] * jnp.concatenate(alphas, axis=1) + upd

    sweep(products(qdt_ref, kk_ref), attn_consume)
    for p in range(N_PAIRS):
        acc = acc_ref[p]
        out_t = acc[:HEAD_DIM] / acc[HEAD_DIM:HEAD_DIM + 1]
        pair = jnp.concatenate([out_t[:, :TQ], out_t[:, TQ:]], axis=0)
        o_ref[:, p * LANES:(p + 1) * LANES] = pair.T.astype(o_ref.dtype)


def _dsa_call(qd_rows, qi_rows, wit, kk, vt, kki, batch, seq):
    r3 = lambda t: t.reshape(batch, seq, t.shape[-1])
    kk, kki = map(r3, (kk, kki))
    topk = min(TOPK_MAX, seq // 4)
    assert seq // BF16_SUBLANES <= BF16_MAX_EXACT_COUNT
    TQ = TQ_DSA
    nq = seq // TQ
    rows = pl.BlockSpec((None, N_HEADS, TQ, LANES), lambda b, i: (b * nq + i, 0, 0, 0))
    allk = pl.BlockSpec((None, seq, LANES), lambda b, i: (b, 0, 0))
    return pl.pallas_call(
        functools.partial(_dsa_kernel, topk=topk),
        out_shape=jax.ShapeDtypeStruct((batch, seq, W_HEADS), BF16),
        grid=(batch, nq),
        in_specs=[rows, rows,
                  pl.BlockSpec((None, N_HEADS, TQ), lambda b, i: (b, 0, i)),
                  allk, pl.BlockSpec((None, VT_ROWS, seq), lambda b, i: (b, 0, 0)), allk],
        out_specs=pl.BlockSpec((None, TQ, W_HEADS), lambda b, i: (b, i, 0)),
        scratch_shapes=[pltpu.VMEM((seq, TQ), F32),
                        pltpu.VMEM((seq, TQ), BF16),
                        pltpu.VMEM((N_HEADS, 1, TQ), F32),
                        pltpu.VMEM((N_PAIRS, VT_ROWS, 2 * TQ), F32),
                        pltpu.VMEM((N_HEADS, CH_H, TQ), F32),
                        pltpu.VMEM((N_PAIRS, LANES, 2 * TQ), BF16),
                        pltpu.VMEM((N_PAIRS, LANES, 2 * TQ), BF16)],
        compiler_params=_cparams(2),
        name="dsa",
    )(qd_rows, qi_rows, wit, kk, vt, kki)


def _mix_kernel(h_ref, n_ref, ysb_ref, yd_ref, mod_ref, wps_ref, wpd_ref, wg_ref, bg_ref,
                wo_ref, o_ref):
    d = h_ref.shape[1]
    y_sb = _dot(ysb_ref[...], wps_ref[...])
    y_d = _dot(yd_ref[...], wpd_ref[...])
    gates = jax.nn.sigmoid(_dot(n_ref[...], wg_ref[...]) + bg_ref[...])
    merged = (gates[:, :d] * y_sb + gates[:, d:] * y_d).astype(BF16)
    o_ref[...] = h_ref[...] + mod_ref[5] * _dot(merged, wo_ref[...])


def _mix_call(h, n2, y_sb, y_d, mod, w_proj_sb, w_proj_dsa, w_gate, b_gate, w_out, *, seq, tm=512):
    n_tok, d = h.shape
    tok = lambda w: pl.BlockSpec((tm, w), lambda i: (i, 0))
    ws = [w.astype(BF16) for w in (w_proj_sb, w_proj_dsa, w_gate)]
    bg = b_gate.reshape(1, -1)
    wo = w_out.astype(BF16)
    return pl.pallas_call(
        _mix_kernel,
        out_shape=jax.ShapeDtypeStruct((n_tok, d), F32),
        grid=(n_tok // tm,),
        in_specs=[tok(d), tok(d), tok(W_HEADS), tok(W_HEADS), _mod_spec(seq // tm, d)]
                 + [_const_spec(w.shape) for w in (*ws, bg, wo)],
        out_specs=tok(d),
        compiler_params=_cparams(1),
        name="mix",
    )(h, n2, y_sb, y_d, mod, *ws, bg, wo)


def kernel(x, c, w_ada, b_ada, g_ffn1, w1_ffn1, w3_ffn1, w2_ffn1, g_mix, w_in, g_q_dsa, g_k_dsa,
           w_proj_sb, w_proj_dsa, w_gate, b_gate, w_out, g_ffn2, w1_ffn2, w3_ffn2, w2_ffn2):
    batch, seq, d = x.shape
    h = x.reshape(batch * seq, d)
    for l in range(w_ada.shape[0]):
        bf = lambda w: w[l].astype(BF16)
        mod = _mod_call(c, w_ada[l], b_ada[l])
        h, n2 = _ffn_call(h, mod, g_ffn1[l], bf(w1_ffn1), bf(w3_ffn1), bf(w2_ffn1), g_mix[l],
                          mod_base=0, emit_next=True, seq=seq)
        q_sb, k_sb, v_sb, q_d, kk, q_i, kki, vt, wit = _inproj_call(
            n2, w_in[l], g_q_dsa[l], g_k_dsa[l], batch, seq)
        y_sb = _sb_call(q_sb, k_sb, v_sb, batch, seq).reshape(batch * seq, W_HEADS)
        y_d = _dsa_call(q_d, q_i, wit, kk, vt, kki, batch, seq).reshape(batch * seq, W_HEADS)
        h = _mix_call(h, n2, y_sb, y_d, mod, w_proj_sb[l], w_proj_dsa[l], w_gate[l], b_gate[l],
                      w_out[l], seq=seq)
        (h,) = _ffn_call(h, mod, g_ffn2[l], bf(w1_ffn2), bf(w3_ffn2), bf(w2_ffn2), g_ffn2[l],
                         mod_base=6, emit_next=False, seq=seq)
    return h.reshape(batch, seq, d)
```

```python
import functools

import numpy as np
import jax
import jax.numpy as jnp
from jax import lax
from jax.experimental import pallas as pl
from jax.experimental.pallas import tpu as pltpu

F32 = jnp.float32
BF16 = jnp.bfloat16

HEAD_DIM = 64
N_HEADS = 8
W_HEADS = N_HEADS * HEAD_DIM
N_PAIRS = N_HEADS // 2
LANES = 128
SUBLANES = 8
BF16_SUBLANES = 16
BF16_MAX_EXACT_COUNT = 256
SELECT_WINDOW_BITS = 17
SELECT_EARLY_PASSES = 11
TOPK_MAX = 256
N_MOD = 9
RMS_EPS = 1e-6
D_IDX = 64

TQ_SB = 256
TQ_DSA = 512
TK = 128
CH_S = 512
CH_H = 256
VT_ROWS = 80
LANE_POS_HI, LANE_POS_LO = HEAD_DIM, HEAD_DIM + 1
N_POS_LANES = 2
N_SLOPE_PARTS = 3
LOG2E = float(np.log2(np.e))
NEG_BIG = -1e30
F32_EXP_UNDERFLOW = 104.0
INT_MIN = np.int32(-2**31)
VMEM_LIMIT = 56 * 1024 * 1024


def _cparams(n_axes):
    return pltpu.CompilerParams(
        dimension_semantics=("arbitrary",) * n_axes, vmem_limit_bytes=VMEM_LIMIT)


def _const_spec(shape):
    nd = len(shape)
    return pl.BlockSpec(shape, lambda *_: (0,) * nd, pipeline_mode=pl.Buffered(1))


def _dot(a, b):
    return jnp.dot(a, b, preferred_element_type=F32)


def _dot_nt(a, b):
    return lax.dot_general(a, b, (((1,), (1,)), ((), ())), preferred_element_type=F32)


def _mod_kernel(c_ref, w_ref, b_ref, o_ref, sc_ref):
    nb, kdim, _ = c_ref.shape
    tn = w_ref.shape[1]
    nch = tn // LANES

    @pl.when(pl.program_id(0) == 0)
    def _():
        c = c_ref[...]
        sc_ref[...] = c * jax.nn.sigmoid(c)

    def body(kb, accs):
        k0 = pl.multiple_of(kb * SUBLANES, SUBLANES)
        wblk = w_ref[pl.ds(k0, SUBLANES), :]
        out = []
        for b in range(nb):
            sb = sc_ref[b, pl.ds(k0, SUBLANES), :]
            for ch in range(nch):
                out.append(accs[b * nch + ch] + wblk[:, ch * LANES:(ch + 1) * LANES] * sb)
        return tuple(out)

    accs = lax.fori_loop(0, kdim // SUBLANES, body,
                         tuple(jnp.zeros((SUBLANES, LANES), F32) for _ in range(nb * nch)))
    for b in range(nb):
        row = jnp.concatenate(
            [jnp.sum(accs[b * nch + ch], axis=0, keepdims=True) for ch in range(nch)], axis=1)
        o_ref[0, b] = row + b_ref[...]


def _mod_call(c, w_ada, b_ada):
    nb, d = c.shape
    n_out = w_ada.shape[1]
    c_rep = jnp.broadcast_to(c[:, :, None], (nb, d, LANES))
    return pl.pallas_call(
        _mod_kernel,
        out_shape=jax.ShapeDtypeStruct((n_out // d, nb, 1, d), F32),
        grid=(n_out // d,),
        in_specs=[_const_spec((nb, d, LANES)),
                  pl.BlockSpec((d, d), lambda j: (0, j)),
                  pl.BlockSpec((1, d), lambda j: (0, j))],
        out_specs=pl.BlockSpec((1, nb, 1, d), lambda j: (j, 0, 0, 0)),
        scratch_shapes=[pltpu.VMEM((nb, d, LANES), F32)],
        compiler_params=_cparams(1),
        name="mod",
    )(c_rep, w_ada, b_ada.reshape(1, n_out))


def _norm_mod(h, g, shift, scale):
    y = h * lax.rsqrt(jnp.mean(h * h, axis=-1, keepdims=True) + RMS_EPS)
    return (y * g) * (1.0 + scale) + shift


def _ffn_kernel(h_ref, mod_ref, g_ref, w1_ref, w3_ref, w2_ref, gn_ref, *out_refs,
                mod_base, emit_next, n_chunks):
    o_ref = out_refs[0]
    h = h_ref[...]
    n = _norm_mod(h, g_ref[...], mod_ref[mod_base], mod_ref[mod_base + 1]).astype(BF16)
    fc = w1_ref.shape[1] // n_chunks
    acc = jnp.zeros(h.shape, F32)
    for c in range(n_chunks):
        a = _dot(n, w1_ref[:, c * fc:(c + 1) * fc])
        b = _dot(n, w3_ref[:, c * fc:(c + 1) * fc])
        act = (a * jax.nn.sigmoid(a) * b).astype(BF16)
        acc = acc + _dot(act, w2_ref[c * fc:(c + 1) * fc, :])
    hn = h + (0.5 * mod_ref[mod_base + 2]) * acc
    o_ref[...] = hn
    if emit_next:
        out_refs[1][...] = _norm_mod(hn, gn_ref[...], mod_ref[mod_base + 3],
                                     mod_ref[mod_base + 4]).astype(BF16)


def _mod_spec(tiles_per_batch, d):
    return pl.BlockSpec((N_MOD, None, 1, d), lambda i: (0, i // tiles_per_batch, 0, 0))


def _ffn_call(h, mod, g, w1, w3, w2, g_next, *, mod_base, emit_next, seq, tm=512):
    n_tok, d = h.shape
    f = w1.shape[1]
    tok = pl.BlockSpec((tm, d), lambda i: (i, 0))
    out_shape = [jax.ShapeDtypeStruct((n_tok, d), F32)]
    out_specs = [tok]
    if emit_next:
        out_shape.append(jax.ShapeDtypeStruct((n_tok, d), BF16))
        out_specs.append(tok)
    return pl.pallas_call(
        functools.partial(_ffn_kernel, mod_base=mod_base, emit_next=emit_next, n_chunks=2),
        out_shape=out_shape,
        grid=(n_tok // tm,),
        in_specs=[tok, _mod_spec(seq // tm, d), _const_spec((1, d)),
                  _const_spec((d, f)), _const_spec((d, f)), _const_spec((f, d)),
                  _const_spec((1, d))],
        out_specs=out_specs,
        compiler_params=_cparams(1),
        name="ffn",
    )(h, mod, g.reshape(1, d), w1, w3, w2, g_next.reshape(1, d))


def _alibi_slope_parts():
    out = []
    for h in range(N_HEADS):
        rest, parts = np.float32(2.0 ** (-8.0 * (h + 1) / N_HEADS) * LOG2E), []
        for _ in range(N_SLOPE_PARTS):
            parts.append(np.float32(np.asarray(rest, dtype=BF16)))
            rest = np.float32(rest - parts[-1])
        out.append(parts)
    return out


def _store_head_rows(o_ref, x, slope_parts):
    tm = x.shape[0]
    lane = lax.broadcasted_iota(jnp.int32, (tm, LANES), 1)
    for h in range(N_HEADS):
        rows = x[:, (h // 2) * LANES:(h // 2 + 1) * LANES]
        if h % 2:
            rows = pltpu.roll(rows, HEAD_DIM, axis=1)
        rows = jnp.where(lane < HEAD_DIM, rows, 0.0)
        if slope_parts is not None:
            for rep, part in enumerate(slope_parts[h]):
                rows = jnp.where(lane == LANE_POS_HI + rep * N_POS_LANES, part * HEAD_DIM, rows)
                rows = jnp.where(lane == LANE_POS_LO + rep * N_POS_LANES, part, rows)
        o_ref[:, h] = rows.astype(BF16).reshape(tm // TQ_DSA, TQ_DSA, LANES)


def _inproj_kernel(n_ref, wsb_ref, wqd_ref, wkd_ref, wqi_ref, wki_ref, wvt_ref, wwt_ref,
                   gq_ref, gk_ref, hm_ref,
                   qsb_ref, ksb_ref, vsb_ref, qd_ref, kk_ref, qi_ref, kki_ref, vt_ref, wit_ref,
                   *, tiles_per_batch):
    n = n_ref[...]
    tm = n.shape[0]
    qk_scale = HEAD_DIM ** -0.5

    qd = _dot(n, wqd_ref[...])
    ms = _dot((qd * qd).astype(BF16), hm_ref[...])
    _store_head_rows(qd_ref, qd * lax.rsqrt(ms + RMS_EPS) * (gq_ref[...] * (qk_scale * LOG2E)),
                     _alibi_slope_parts())

    kd = _dot(n, wkd_ref[...])
    kn = kd * lax.rsqrt(jnp.mean(kd * kd, axis=-1, keepdims=True) + RMS_EPS) * gk_ref[...]
    lane = lax.broadcasted_iota(jnp.int32, (tm, LANES), 1)
    pos = ((pl.program_id(0) % tiles_per_batch) * tm
           + lax.broadcasted_iota(jnp.int32, (tm, LANES), 0))
    feat = jnp.zeros((tm, LANES), jnp.int32)
    for rep in range(N_SLOPE_PARTS):
        feat = jnp.where(lane == LANE_POS_HI + rep * N_POS_LANES,
                         pos >> (HEAD_DIM.bit_length() - 1), feat)
        feat = jnp.where(lane == LANE_POS_LO + rep * N_POS_LANES, pos & (HEAD_DIM - 1), feat)
    feat = feat.astype(F32)
    kk_ref[...] = jnp.where(lane < HEAD_DIM, kn, feat).astype(BF16)

    _store_head_rows(qi_ref, _dot(n, wqi_ref[...]), None)
    kki_ref[...] = _dot(n, wki_ref[...]).astype(BF16)
    vt = _dot_nt(wvt_ref[...], n)
    row = lax.broadcasted_iota(jnp.int32, vt.shape, 0)
    vt_ref[...] = jnp.where(row == HEAD_DIM, 1.0, vt).astype(BF16)
    wit_ref[...] = _dot_nt(wwt_ref[...], n)[:N_HEADS]

    sb = _dot(n, wsb_ref[...])
    qsb_ref[...] = (sb[:, :W_HEADS] * qk_scale).astype(BF16)
    ksb_ref[...] = sb[:, W_HEADS:2 * W_HEADS].astype(BF16)
    vsb_ref[...] = sb[:, 2 * W_HEADS:].astype(BF16)


def _inproj_call(n2, w_in, g_q, g_k, batch, seq, tm=1024):
    n_tok, d = n2.shape
    sizes = (W_HEADS, W_HEADS, W_HEADS, W_HEADS, HEAD_DIM, HEAD_DIM, N_HEADS * D_IDX, D_IDX, N_HEADS)
    offs = np.concatenate([[0], np.cumsum(sizes)])
    col = lambda k: w_in[:, offs[k]:offs[k + 1]]
    idx_scale = (D_IDX ** -0.5) * (N_HEADS ** -0.5)
    w_sb = jnp.concatenate([col(0), col(1), col(2)], axis=1).astype(BF16)
    w_qd = col(3).astype(BF16)
    w_kd = jnp.concatenate([col(4), col(4)], axis=1).astype(BF16)
    w_qi = col(6).astype(BF16)
    w_ki = jnp.pad(col(7), ((0, 0), (0, LANES - D_IDX))).astype(BF16)
    w_vt = jnp.pad(col(5).T, ((0, VT_ROWS - HEAD_DIM), (0, 0))).astype(BF16)
    w_wt = jnp.pad((col(8) * idx_scale).T, ((0, 16 - N_HEADS), (0, 0))).astype(BF16)
    gq = jnp.tile(g_q, N_HEADS).reshape(1, W_HEADS)
    gk = jnp.tile(g_k, 2).reshape(1, LANES)
    head_of = np.arange(W_HEADS) // HEAD_DIM
    head_mean = jnp.asarray((head_of[:, None] == head_of[None, :]) / HEAD_DIM, BF16)

    tpb = seq // tm
    tok = lambda w: pl.BlockSpec((tm, w), lambda i: (i, 0))
    tr = lambda r: pl.BlockSpec((None, r, tm), lambda i: (i // tpb, 0, i % tpb))
    sd = lambda w, dt: jax.ShapeDtypeStruct((n_tok, w), dt)
    consts = (w_sb, w_qd, w_kd, w_qi, w_ki, w_vt, w_wt, gq, gk, head_mean)
    head_rows = jax.ShapeDtypeStruct((n_tok // TQ_DSA, N_HEADS, TQ_DSA, LANES), BF16)
    head_rows_spec = pl.BlockSpec((tm // TQ_DSA, N_HEADS, TQ_DSA, LANES), lambda i: (i, 0, 0, 0))
    return pl.pallas_call(
        functools.partial(_inproj_kernel, tiles_per_batch=tpb),
        out_shape=[sd(W_HEADS, BF16)] * 3 + [head_rows, sd(LANES, BF16), head_rows, sd(LANES, BF16),
                                             jax.ShapeDtypeStruct((batch, VT_ROWS, seq), BF16),
                                             jax.ShapeDtypeStruct((batch, N_HEADS, seq), F32)],
        grid=(n_tok // tm,),
        in_specs=[tok(d)] + [_const_spec(w.shape) for w in consts],
        out_specs=[tok(W_HEADS)] * 3 + [head_rows_spec, tok(LANES), head_rows_spec, tok(LANES),
                                        tr(VT_ROWS), tr(N_HEADS)],
        compiler_params=_cparams(1),
        name="inproj",
    )(n2, *consts)


def _split_heads(x, lane):
    xf = x.astype(F32)
    lo = jnp.where(lane < HEAD_DIM, xf, 0.0).astype(x.dtype)
    hi = jnp.where(lane >= HEAD_DIM, xf, 0.0).astype(x.dtype)
    return lo, hi


def _sb_kernel(q_ref, k_ref, v_ref, t_ref, o_ref, qs_ref, run_ref, acc_ref):
    tq = q_ref.shape[0]
    n_diag = tq // TK
    i = pl.program_id(1)
    lane = lax.broadcasted_iota(jnp.int32, (tq, LANES), 1)
    lane_k = lax.broadcasted_iota(jnp.int32, (TK, LANES), 1)
    for p in range(N_PAIRS):
        lo, hi = _split_heads(q_ref[:, p * LANES:(p + 1) * LANES], lane)
        qs_ref[p, :tq, :] = lo
        qs_ref[p, tq:, :] = hi

    def block(k0, key_offset, first):
        pairs = range(N_PAIRS)
        masked = key_offset is not None
        strict = (lax.broadcasted_iota(jnp.int32, (2 * tq, TK), 1) + key_offset
                  < lax.broadcasted_iota(jnp.int32, (2 * tq, TK), 0) % tq) if masked else None
        zs = [_dot_nt(qs_ref[p], k_ref[pl.ds(k0, TK), p * LANES:(p + 1) * LANES]) for p in pairs]
        lbs, ws = [], []
        for p in pairs:
            z = zs[p]
            sp = jnp.maximum(z, 0.0) + jnp.log(1.0 + jnp.exp(-jnp.abs(z)))
            l1m = jnp.where(strict, sp, 0.0) if masked else sp
            l_hi = l1m.astype(BF16)
            l_lo = (l1m - l_hi.astype(F32)).astype(BF16)
            ws.append(_dot(jnp.concatenate([l_hi, l_lo], axis=1), t_ref[...]))
            lbs.append(z - sp)
        for p in pairs:
            arg = lbs[p] + ws[p][:, :TK]
            if not first:
                arg = arg + run_ref[p]
            a = jnp.exp(arg)
            if masked:
                a = jnp.where(strict, a, 0.0)
            a = a.astype(BF16)
            v_lo, v_hi = _split_heads(v_ref[pl.ds(k0, TK), p * LANES:(p + 1) * LANES], lane_k)
            upd = _dot(jnp.concatenate([a[:tq], a[tq:]], axis=1),
                       jnp.concatenate([v_lo, v_hi], axis=0))
            if first:
                acc_ref[p] = upd
                run_ref[p] = ws[p][:, TK:]
            else:
                acc_ref[p] += upd
                run_ref[p] += ws[p][:, TK:]

    last = (i + 1) * n_diag - 1
    for jj in range(n_diag):
        block(pl.multiple_of((last - jj) * TK, TK), (n_diag - 1 - jj) * TK, jj == 0)

    def more_blocks(jj):
        run_max = jnp.max(functools.reduce(jnp.maximum, [run_ref[p] for p in range(N_PAIRS)]))
        return jnp.logical_and(jj <= last, run_max > -F32_EXP_UNDERFLOW)

    def body(state):
        jj, _ = state
        block(pl.multiple_of((last - jj) * TK, TK), None, False)
        return jj + 1, more_blocks(jj + 1)

    lax.while_loop(lambda state: state[1], body, (jnp.int32(n_diag), i >= 1))
    for p in range(N_PAIRS):
        o_ref[:, p * LANES:(p + 1) * LANES] = acc_ref[p].astype(o_ref.dtype)


def _sb_call(q, k, v, batch, seq):
    q, k, v = (t.reshape(batch, seq, W_HEADS) for t in (q, k, v))
    j_idx = np.arange(TK)
    suffix = (j_idx[:, None] > j_idx[None, :]).astype(np.float32)
    tmat = -np.concatenate([suffix, np.ones((TK, LANES), np.float32)], axis=1)
    tmat = jnp.asarray(np.concatenate([tmat, tmat], axis=0), BF16)
    blk = pl.BlockSpec((None, TQ_SB, W_HEADS), lambda b, i: (b, i, 0))
    allk = pl.BlockSpec((None, seq, W_HEADS), lambda b, i: (b, 0, 0))
    return pl.pallas_call(
        _sb_kernel,
        out_shape=jax.ShapeDtypeStruct((batch, seq, W_HEADS), BF16),
        grid=(batch, seq // TQ_SB),
        in_specs=[blk, allk, allk, _const_spec(tmat.shape)],
        out_specs=blk,
        scratch_shapes=[pltpu.VMEM((N_PAIRS, 2 * TQ_SB, LANES), BF16),
                        pltpu.VMEM((N_PAIRS, 2 * TQ_SB, LANES), F32),
                        pltpu.VMEM((N_PAIRS, TQ_SB, LANES), F32)],
        compiler_params=_cparams(2),
        name="sb",
    )(q, k, v, tmat)


def _dsa_kernel(qd_st, qi_st, wit_ref, kk_ref, vt_ref, kki_ref, o_ref,
                sc_ref, top_ref, m_ref, acc_ref, buf_ref, qit_ref, qdt_ref, *, topk):
    TQ = qd_st.shape[1]
    i = pl.program_id(1)
    t0 = i * TQ

    n_s = (i + CH_S // TQ) // (CH_S // TQ)

    for q_st, qt_ref in ((qi_st, qit_ref), (qd_st, qdt_ref)):
        for p in range(N_PAIRS):
            qt_ref[p] = jnp.concatenate(
                [q_st[h].astype(F32).T for h in (2 * p, 2 * p + 1)], axis=1).astype(BF16)

    def products(qt_ref, k_ref):
        def produce(k0):
            kch = k_ref[pl.ds(k0, CH_H), :]
            return [_dot(kch, qt_ref[p]) for p in range(N_PAIRS)]
        return produce

    def store_products(vals):
        for p in range(N_PAIRS):
            buf_ref[2 * p] = vals[p][:, :TQ]
            buf_ref[2 * p + 1] = vals[p][:, TQ:]

    def sweep(produce, consume):
        from_buf = lambda h: buf_ref[h]

        def trip(c, prefetch):
            k_a = pl.multiple_of(c * CH_S, CH_S)
            k_b = pl.multiple_of(k_a + CH_H, CH_H)
            vals_b = produce(k_b)
            consume(from_buf, k_a)
            if prefetch:
                vals_next = produce(pl.multiple_of(k_a + CH_S, CH_S))
            consume(lambda h: vals_b[h // 2][:, (h % 2) * TQ:(h % 2 + 1) * TQ], k_b)
            if prefetch:
                store_products(vals_next)

        store_products(produce(0))

        def body(c, _):
            trip(c, True)
            return 0

        lax.fori_loop(0, n_s - 1, body, 0)
        trip(n_s - 1, False)

    wit = wit_ref[...]

    def score_consume(prod, k0):
        score = None
        for h in range(N_HEADS):
            term = jnp.maximum(prod(h), 0.0) * wit[h:h + 1, :]
            score = term if score is None else score + term
        causal = (k0 + lax.broadcasted_iota(jnp.int32, (CH_H, TQ), 0)
                  <= t0 + lax.broadcasted_iota(jnp.int32, (CH_H, TQ), 1))
        score = jnp.where(causal, score, -jnp.inf)
        sc_ref[pl.ds(k0, CH_H), :] = score
        top_ref[pl.ds(k0, CH_H), :] = score.astype(BF16)

    sweep(products(qit_ref, kki_ref), score_consume)

    def float_of_rank(u):
        key = u ^ INT_MIN
        return pltpu.bitcast(jnp.where(key < 0, key ^ np.int32(0x7FFFFFFF), key), F32)

    def count_top_ge(cand):
        one, zero = jnp.ones((BF16_SUBLANES, TQ), BF16), jnp.zeros((BF16_SUBLANES, TQ), BF16)

        def cbody(c, acc):
            k0 = pl.multiple_of(c * CH_S, CH_S)
            tops = top_ref[pl.ds(k0, CH_S), :]
            parts = [jnp.where(tops[j * BF16_SUBLANES:(j + 1) * BF16_SUBLANES] >= cand, one, zero)
                     for j in range(CH_S // BF16_SUBLANES)]
            while len(parts) > 1:
                parts = [a + b for a, b in zip(parts[::2], parts[1::2])]
            return acc + parts[0]

        acc = lax.fori_loop(0, n_s, cbody, zero)
        return jnp.sum(acc.astype(F32), axis=0, keepdims=True)

    def top_body(b, u16):
        cand = u16 | jnp.left_shift(jnp.int32(1), 15 - b)
        cand_f = float_of_rank(jnp.left_shift(cand, 16)).astype(BF16)
        return jnp.where(count_top_ge(jnp.broadcast_to(cand_f, (BF16_SUBLANES, TQ))) >= topk,
                         cand, u16)

    u16 = lax.fori_loop(0, 16, top_body, jnp.zeros((1, TQ), jnp.int32))

    def count(pred):
        def cbody(c, acc):
            k0 = pl.multiple_of(c * CH_S, CH_S)
            sch = sc_ref[pl.ds(k0, CH_S), :].reshape(CH_S // SUBLANES, SUBLANES, TQ)
            return acc + jnp.sum(jnp.where(pred(sch), 1, 0), axis=0)
        acc = lax.fori_loop(0, n_s, cbody, jnp.zeros((SUBLANES, TQ), jnp.int32))
        tot = jnp.sum(acc.astype(F32), axis=0, keepdims=True)
        return jnp.broadcast_to(tot, (SUBLANES, TQ))

    base = jnp.broadcast_to(jnp.left_shift(u16, 16) - (1 << 15), (SUBLANES, TQ))

    def window_body(b, state):
        off, n_off = state
        cand = off | jnp.left_shift(jnp.int32(1), SELECT_WINDOW_BITS - 1 - b)
        cand_f = float_of_rank(base + cand)
        n_cand = count(lambda sch: sch >= cand_f[None])
        take = n_cand >= topk
        return jnp.where(take, cand, off), jnp.where(take, n_cand, n_off)

    few_keys = t0 + lax.broadcasted_iota(jnp.int32, (SUBLANES, TQ), 1) < topk
    unknown = jnp.full((SUBLANES, TQ), -1.0, F32)
    state = lax.fori_loop(0, SELECT_EARLY_PASSES, window_body,
                          (jnp.zeros((SUBLANES, TQ), jnp.int32), unknown))
    still_open = jnp.max(jnp.where(jnp.logical_or(state[1] == topk, few_keys), 0.0, 1.0))
    offset, n_off = lax.cond(
        still_open > 0.0,
        lambda st: lax.fori_loop(SELECT_EARLY_PASSES, SELECT_WINDOW_BITS, window_body, st),
        lambda st: st, state)
    thr = jnp.where(few_keys, jnp.finfo(F32).min, float_of_rank(base + offset))
    thr_row = thr[0:1, :]

    n_ge = jnp.where(few_keys, 0.0, jnp.where(n_off < 0.0, topk + 1.0, n_off))

    @pl.when(jnp.max(n_ge) > topk)
    def _():
        need = topk - count(lambda sch: sch > thr[None])[0:1, :]
        before = jnp.where(lax.broadcasted_iota(jnp.int32, (CH_H, CH_H), 1)
                           < lax.broadcasted_iota(jnp.int32, (CH_H, CH_H), 0), 1.0, 0.0).astype(BF16)

        def tie_body(c, seen):
            k0s = [pl.multiple_of(c * CH_S + j * CH_H, CH_H) for j in range(CH_S // CH_H)]
            schs = [sc_ref[pl.ds(k0, CH_H), :] for k0 in k0s]
            eqs = [jnp.where(sch == thr_row, 1.0, 0.0) for sch in schs]
            ranks = [_dot(before, eq.astype(BF16)) for eq in eqs]
            for k0, sch, eq, rank in zip(k0s, schs, eqs, ranks):
                tie_rank = jnp.where(sch == thr_row, rank + seen, -1.0)
                sc_ref[pl.ds(k0, CH_H), :] = jnp.where(tie_rank >= need, -jnp.inf, sch)
                seen = seen + jnp.sum(eq, axis=0, keepdims=True)
            return seen

        lax.fori_loop(0, n_s, tie_body, jnp.zeros((1, TQ), F32))

    m_ref[...] = jnp.full(m_ref.shape, NEG_BIG, F32)
    acc_ref[...] = jnp.zeros(acc_ref.shape, F32)

    def attn_consume(prod, k0):
        vch = vt_ref[:, pl.ds(k0, CH_H)]
        sel = sc_ref[pl.ds(k0, CH_H), :] >= thr_row
        for p in range(N_PAIRS):
            probs, alphas = [], []
            for h in (2 * p, 2 * p + 1):
                lg = jnp.where(sel, prod(h), NEG_BIG)
                m_old = m_ref[h]
                m_new = jnp.maximum(m_old, jnp.max(lg, axis=0, keepdims=True))
                m_ref[h] = m_new
                alphas.append(jnp.exp2(m_old - m_new))
                probs.append(jnp.exp2(lg - m_new).astype(BF16))
            upd = _dot(vch, jnp.concatenate(probs, axis=1))
            acc_ref[p] = acc_ref[p] * jnp.concatenate(alphas, axis=1) + upd

    sweep(products(qdt_ref, kk_ref), attn_consume)
    for p in range(N_PAIRS):
        acc = acc_ref[p]
        out_t = acc[:HEAD_DIM] / acc[HEAD_DIM:HEAD_DIM + 1]
        pair = jnp.concatenate([out_t[:, :TQ], out_t[:, TQ:]], axis=0)
        o_ref[:, p * LANES:(p + 1) * LANES] = pair.T.astype(o_ref.dtype)


def _dsa_call(qd_rows, qi_rows, wit, kk, vt, kki, batch, seq):
    r3 = lambda t: t.reshape(batch, seq, t.shape[-1])
    kk, kki = map(r3, (kk, kki))
    topk = min(TOPK_MAX, seq // 4)
    assert seq // BF16_SUBLANES <= BF16_MAX_EXACT_COUNT
    TQ = TQ_DSA
    nq = seq // TQ
    rows = pl.BlockSpec((None, N_HEADS, TQ, LANES), lambda b, i: (b * nq + i, 0, 0, 0))
    allk = pl.BlockSpec((None, seq, LANES), lambda b, i: (b, 0, 0))
    return pl.pallas_call(
        functools.partial(_dsa_kernel, topk=topk),
        out_shape=jax.ShapeDtypeStruct((batch, seq, W_HEADS), BF16),
        grid=(batch, nq),
        in_specs=[rows, rows,
                  pl.BlockSpec((None, N_HEADS, TQ), lambda b, i: (b, 0, i)),
                  allk, pl.BlockSpec((None, VT_ROWS, seq), lambda b, i: (b, 0, 0)), allk],
        out_specs=pl.BlockSpec((None, TQ, W_HEADS), lambda b, i: (b, i, 0)),
        scratch_shapes=[pltpu.VMEM((seq, TQ), F32),
                        pltpu.VMEM((seq, TQ), BF16),
                        pltpu.VMEM((N_HEADS, 1, TQ), F32),
                        pltpu.VMEM((N_PAIRS, VT_ROWS, 2 * TQ), F32),
                        pltpu.VMEM((N_HEADS, CH_H, TQ), F32),
                        pltpu.VMEM((N_PAIRS, LANES, 2 * TQ), BF16),
                        pltpu.VMEM((N_PAIRS, LANES, 2 * TQ), BF16)],
        compiler_params=_cparams(2),
        name="dsa",
    )(qd_rows, qi_rows, wit, kk, vt, kki)


def _mix_kernel(h_ref, n_ref, ysb_ref, yd_ref, mod_ref, wps_ref, wpd_ref, wg_ref, bg_ref,
                wo_ref, o_ref):
    d = h_ref.shape[1]
    y_sb = _dot(ysb_ref[...], wps_ref[...])
    y_d = _dot(yd_ref[...], wpd_ref[...])
    gates = jax.nn.sigmoid(_dot(n_ref[...], wg_ref[...]) + bg_ref[...])
    merged = (gates[:, :d] * y_sb + gates[:, d:] * y_d).astype(BF16)
    o_ref[...] = h_ref[...] + mod_ref[5] * _dot(merged, wo_ref[...])


def _mix_call(h, n2, y_sb, y_d, mod, w_proj_sb, w_proj_dsa, w_gate, b_gate, w_out, *, seq, tm=512):
    n_tok, d = h.shape
    tok = lambda w: pl.BlockSpec((tm, w), lambda i: (i, 0))
    ws = [w.astype(BF16) for w in (w_proj_sb, w_proj_dsa, w_gate)]
    bg = b_gate.reshape(1, -1)
    wo = w_out.astype(BF16)
    return pl.pallas_call(
        _mix_kernel,
        out_shape=jax.ShapeDtypeStruct((n_tok, d), F32),
        grid=(n_tok // tm,),
        in_specs=[tok(d), tok(d), tok(W_HEADS), tok(W_HEADS), _mod_spec(seq // tm, d)]
                 + [_const_spec(w.shape) for w in (*ws, bg, wo)],
        out_specs=tok(d),
        compiler_params=_cparams(1),
        name="mix",
    )(h, n2, y_sb, y_d, mod, *ws, bg, wo)


def kernel(x, c, w_ada, b_ada, g_ffn1, w1_ffn1, w3_ffn1, w2_ffn1, g_mix, w_in, g_q_dsa, g_k_dsa,
           w_proj_sb, w_proj_dsa, w_gate, b_gate, w_out, g_ffn2, w1_ffn2, w3_ffn2, w2_ffn2):
    batch, seq, d = x.shape
    h = x.reshape(batch * seq, d)
    for l in range(w_ada.shape[0]):
        bf = lambda w: w[l].astype(BF16)
        mod = _mod_call(c, w_ada[l], b_ada[l])
        h, n2 = _ffn_call(h, mod, g_ffn1[l], bf(w1_ffn1), bf(w3_ffn1), bf(w2_ffn1), g_mix[l],
                          mod_base=0, emit_next=True, seq=seq)
        q_sb, k_sb, v_sb, q_d, kk, q_i, kki, vt, wit = _inproj_call(
            n2, w_in[l], g_q_dsa[l], g_k_dsa[l], batch, seq)
        y_sb = _sb_call(q_sb, k_sb, v_sb, batch, seq).reshape(batch * seq, W_HEADS)
        y_d = _dsa_call(q_d, q_i, wit, kk, vt, kki, batch, seq).reshape(batch * seq, W_HEADS)
        h = _mix_call(h, n2, y_sb, y_d, mod, w_proj_sb[l], w_proj_dsa[l], w_gate[l], b_gate[l],
                      w_out[l], seq=seq)
        (h,) = _ffn_call(h, mod, g_ffn2[l], bf(w1_ffn2), bf(w3_ffn2), bf(w2_ffn2), g_ffn2[l],
                         mod_base=6, emit_next=False, seq=seq)
    return h.reshape(batch, seq, d)
```

```python
import functools

import numpy as np
import jax
import jax.numpy as jnp
from jax import lax
from jax.experimental import pallas as pl
from jax.experimental.pallas import tpu as pltpu

F32 = jnp.float32
BF16 = jnp.bfloat16

HEAD_DIM = 64
N_HEADS = 8
W_HEADS = N_HEADS * HEAD_DIM
N_PAIRS = N_HEADS // 2
LANES = 128
SUBLANES = 8
BF16_SUBLANES = 16
BF16_MAX_EXACT_COUNT = 256
SELECT_WINDOW_BITS = 17
SELECT_EARLY_PASSES = 11
TOPK_MAX = 256
N_MOD = 9
RMS_EPS = 1e-6
D_IDX = 64

TQ_SB = 256
TQ_DSA = 512
TK = 128
SB_PAIR_GROUPS = ((0, 1, 2, 3),)
CH_S = 512
CH_H = 256
VT_ROWS = 80
LANE_POS_HI, LANE_POS_LO = HEAD_DIM, HEAD_DIM + 1
N_POS_LANES = 2
N_SLOPE_PARTS = 3
LOG2E = float(np.log2(np.e))
NEG_BIG = -1e30
F32_EXP_UNDERFLOW = 104.0
INT_MIN = np.int32(-2**31)
VMEM_LIMIT = 56 * 1024 * 1024


def _cparams(n_axes):
    return pltpu.CompilerParams(
        dimension_semantics=("arbitrary",) * n_axes, vmem_limit_bytes=VMEM_LIMIT)


def _const_spec(shape):
    nd = len(shape)
    return pl.BlockSpec(shape, lambda *_: (0,) * nd, pipeline_mode=pl.Buffered(1))


def _dot(a, b):
    return jnp.dot(a, b, preferred_element_type=F32)


def _dot_nt(a, b):
    return lax.dot_general(a, b, (((1,), (1,)), ((), ())), preferred_element_type=F32)


def _mod_kernel(c_ref, w_ref, b_ref, o_ref, sc_ref):
    nb, kdim, _ = c_ref.shape
    tn = w_ref.shape[1]
    nch = tn // LANES

    @pl.when(pl.program_id(0) == 0)
    def _():
        c = c_ref[...]
        sc_ref[...] = c * jax.nn.sigmoid(c)

    def body(kb, accs):
        k0 = pl.multiple_of(kb * SUBLANES, SUBLANES)
        wblk = w_ref[pl.ds(k0, SUBLANES), :]
        out = []
        for b in range(nb):
            sb = sc_ref[b, pl.ds(k0, SUBLANES), :]
            for ch in range(nch):
                out.append(accs[b * nch + ch] + wblk[:, ch * LANES:(ch + 1) * LANES] * sb)
        return tuple(out)

    accs = lax.fori_loop(0, kdim // SUBLANES, body,
                         tuple(jnp.zeros((SUBLANES, LANES), F32) for _ in range(nb * nch)))
    for b in range(nb):
        row = jnp.concatenate(
            [jnp.sum(accs[b * nch + ch], axis=0, keepdims=True) for ch in range(nch)], axis=1)
        o_ref[0, b] = row + b_ref[...]


def _mod_call(c, w_ada, b_ada):
    nb, d = c.shape
    n_out = w_ada.shape[1]
    c_rep = jnp.broadcast_to(c[:, :, None], (nb, d, LANES))
    return pl.pallas_call(
        _mod_kernel,
        out_shape=jax.ShapeDtypeStruct((n_out // d, nb, 1, d), F32),
        grid=(n_out // d,),
        in_specs=[_const_spec((nb, d, LANES)),
                  pl.BlockSpec((d, d), lambda j: (0, j)),
                  pl.BlockSpec((1, d), lambda j: (0, j))],
        out_specs=pl.BlockSpec((1, nb, 1, d), lambda j: (j, 0, 0, 0)),
        scratch_shapes=[pltpu.VMEM((nb, d, LANES), F32)],
        compiler_params=_cparams(1),
        name="mod",
    )(c_rep, w_ada, b_ada.reshape(1, n_out))


def _norm_mod(h, g, shift, scale):
    y = h * lax.rsqrt(jnp.mean(h * h, axis=-1, keepdims=True) + RMS_EPS)
    return (y * g) * (1.0 + scale) + shift


def _ffn_kernel(h_ref, mod_ref, g_ref, w1_ref, w3_ref, w2_ref, gn_ref, *out_refs,
                mod_base, emit_next, n_chunks):
    o_ref = out_refs[0]
    h = h_ref[...]
    n = _norm_mod(h, g_ref[...], mod_ref[mod_base], mod_ref[mod_base + 1]).astype(BF16)
    fc = w1_ref.shape[1] // n_chunks
    acc = jnp.zeros(h.shape, F32)
    for c in range(n_chunks):
        a = _dot(n, w1_ref[:, c * fc:(c + 1) * fc])
        b = _dot(n, w3_ref[:, c * fc:(c + 1) * fc])
        act = (a * jax.nn.sigmoid(a) * b).astype(BF16)
        acc = acc + _dot(act, w2_ref[c * fc:(c + 1) * fc, :])
    hn = h + (0.5 * mod_ref[mod_base + 2]) * acc
    o_ref[...] = hn
    if emit_next:
        out_refs[1][...] = _norm_mod(hn, gn_ref[...], mod_ref[mod_base + 3],
                                     mod_ref[mod_base + 4]).astype(BF16)


def _mod_spec(tiles_per_batch, d):
    return pl.BlockSpec((N_MOD, None, 1, d), lambda i: (0, i // tiles_per_batch, 0, 0))


def _ffn_call(h, mod, g, w1, w3, w2, g_next, *, mod_base, emit_next, seq, tm=512):
    n_tok, d = h.shape
    f = w1.shape[1]
    tok = pl.BlockSpec((tm, d), lambda i: (i, 0))
    out_shape = [jax.ShapeDtypeStruct((n_tok, d), F32)]
    out_specs = [tok]
    if emit_next:
        out_shape.append(jax.ShapeDtypeStruct((n_tok, d), BF16))
        out_specs.append(tok)
    return pl.pallas_call(
        functools.partial(_ffn_kernel, mod_base=mod_base, emit_next=emit_next, n_chunks=2),
        out_shape=out_shape,
        grid=(n_tok // tm,),
        in_specs=[tok, _mod_spec(seq // tm, d), _const_spec((1, d)),
                  _const_spec((d, f)), _const_spec((d, f)), _const_spec((f, d)),
                  _const_spec((1, d))],
        out_specs=out_specs,
        compiler_params=_cparams(1),
        name="ffn",
    )(h, mod, g.reshape(1, d), w1, w3, w2, g_next.reshape(1, d))


def _alibi_slope_parts():
    out = []
    for h in range(N_HEADS):
        rest, parts = np.float32(2.0 ** (-8.0 * (h + 1) / N_HEADS) * LOG2E), []
        for _ in range(N_SLOPE_PARTS):
            parts.append(np.float32(np.asarray(rest, dtype=BF16)))
            rest = np.float32(rest - parts[-1])
        out.append(parts)
    return out


def _store_head_rows(o_ref, x, slope_parts):
    tm = x.shape[0]
    lane = lax.broadcasted_iota(jnp.int32, (tm, LANES), 1)
    for h in range(N_HEADS):
        rows = x[:, (h // 2) * LANES:(h // 2 + 1) * LANES]
        if h % 2:
            rows = pltpu.roll(rows, HEAD_DIM, axis=1)
        rows = jnp.where(lane < HEAD_DIM, rows, 0.0)
        if slope_parts is not None:
            for rep, part in enumerate(slope_parts[h]):
                rows = jnp.where(lane == LANE_POS_HI + rep * N_POS_LANES, part * HEAD_DIM, rows)
                rows = jnp.where(lane == LANE_POS_LO + rep * N_POS_LANES, part, rows)
        o_ref[:, h] = rows.astype(BF16).reshape(tm // TQ_DSA, TQ_DSA, LANES)


def _inproj_kernel(n_ref, wsb_ref, wqd_ref, wkd_ref, wqi_ref, wki_ref, wvt_ref, wwt_ref,
                   gq_ref, gk_ref, hm_ref,
                   qsb_ref, ksb_ref, vsb_ref, qd_ref, kk_ref, qi_ref, kki_ref, vt_ref, wit_ref,
                   *, tiles_per_batch):
    n = n_ref[...]
    tm = n.shape[0]
    qk_scale = HEAD_DIM ** -0.5

    qd = _dot(n, wqd_ref[...])
    ms = _dot((qd * qd).astype(BF16), hm_ref[...])
    _store_head_rows(qd_ref, qd * lax.rsqrt(ms + RMS_EPS) * (gq_ref[...] * (qk_scale * LOG2E)),
                     _alibi_slope_parts())

    kd = _dot(n, wkd_ref[...])
    kn = kd * lax.rsqrt(jnp.mean(kd * kd, axis=-1, keepdims=True) + RMS_EPS) * gk_ref[...]
    lane = lax.broadcasted_iota(jnp.int32, (tm, LANES), 1)
    pos = ((pl.program_id(0) % tiles_per_batch) * tm
           + lax.broadcasted_iota(jnp.int32, (tm, LANES), 0))
    feat = jnp.zeros((tm, LANES), jnp.int32)
    for rep in range(N_SLOPE_PARTS):
        feat = jnp.where(lane == LANE_POS_HI + rep * N_POS_LANES,
                         pos >> (HEAD_DIM.bit_length() - 1), feat)
        feat = jnp.where(lane == LANE_POS_LO + rep * N_POS_LANES, pos & (HEAD_DIM - 1), feat)
    feat = feat.astype(F32)
    kk_ref[...] = jnp.where(lane < HEAD_DIM, kn, feat).astype(BF16)

    _store_head_rows(qi_ref, _dot(n, wqi_ref[...]), None)
    kki_ref[...] = _dot(n, wki_ref[...]).astype(BF16)
    vt = _dot_nt(wvt_ref[...], n)
    row = lax.broadcasted_iota(jnp.int32, vt.shape, 0)
    vt_ref[...] = jnp.where(row == HEAD_DIM, 1.0, vt).astype(BF16)
    wit_ref[...] = _dot_nt(wwt_ref[...], n)[:N_HEADS]

    sb = _dot(n, wsb_ref[...])
    qsb_ref[...] = (sb[:, :W_HEADS] * qk_scale).astype(BF16)
    ksb_ref[...] = sb[:, W_HEADS:2 * W_HEADS].astype(BF16)
    vsb_ref[...] = sb[:, 2 * W_HEADS:].astype(BF16)


def _inproj_call(n2, w_in, g_q, g_k, batch, seq, tm=1024):
    n_tok, d = n2.shape
    sizes = (W_HEADS, W_HEADS, W_HEADS, W_HEADS, HEAD_DIM, HEAD_DIM, N_HEADS * D_IDX, D_IDX, N_HEADS)
    offs = np.concatenate([[0], np.cumsum(sizes)])
    col = lambda k: w_in[:, offs[k]:offs[k + 1]]
    idx_scale = (D_IDX ** -0.5) * (N_HEADS ** -0.5)
    w_sb = jnp.concatenate([col(0), col(1), col(2)], axis=1).astype(BF16)
    w_qd = col(3).astype(BF16)
    w_kd = jnp.concatenate([col(4), col(4)], axis=1).astype(BF16)
    w_qi = col(6).astype(BF16)
    w_ki = jnp.pad(col(7), ((0, 0), (0, LANES - D_IDX))).astype(BF16)
    w_vt = jnp.pad(col(5).T, ((0, VT_ROWS - HEAD_DIM), (0, 0))).astype(BF16)
    w_wt = jnp.pad((col(8) * idx_scale).T, ((0, 16 - N_HEADS), (0, 0))).astype(BF16)
    gq = jnp.tile(g_q, N_HEADS).reshape(1, W_HEADS)
    gk = jnp.tile(g_k, 2).reshape(1, LANES)
    head_of = np.arange(W_HEADS) // HEAD_DIM
    head_mean = jnp.asarray((head_of[:, None] == head_of[None, :]) / HEAD_DIM, BF16)

    tpb = seq // tm
    tok = lambda w: pl.BlockSpec((tm, w), lambda i: (i, 0))
    tr = lambda r: pl.BlockSpec((None, r, tm), lambda i: (i // tpb, 0, i % tpb))
    sd = lambda w, dt: jax.ShapeDtypeStruct((n_tok, w), dt)
    consts = (w_sb, w_qd, w_kd, w_qi, w_ki, w_vt, w_wt, gq, gk, head_mean)
    head_rows = jax.ShapeDtypeStruct((n_tok // TQ_DSA, N_HEADS, TQ_DSA, LANES), BF16)
    head_rows_spec = pl.BlockSpec((tm // TQ_DSA, N_HEADS, TQ_DSA, LANES), lambda i: (i, 0, 0, 0))
    return pl.pallas_call(
        functools.partial(_inproj_kernel, tiles_per_batch=tpb),
        out_shape=[sd(W_HEADS, BF16)] * 3 + [head_rows, sd(LANES, BF16), head_rows, sd(LANES, BF16),
                                             jax.ShapeDtypeStruct((batch, VT_ROWS, seq), BF16),
                                             jax.ShapeDtypeStruct((batch, N_HEADS, seq), F32)],
        grid=(n_tok // tm,),
        in_specs=[tok(d)] + [_const_spec(w.shape) for w in consts],
        out_specs=[tok(W_HEADS)] * 3 + [head_rows_spec, tok(LANES), head_rows_spec, tok(LANES),
                                        tr(VT_ROWS), tr(N_HEADS)],
        compiler_params=_cparams(1),
        name="inproj",
    )(n2, *consts)


def _split_heads(x, lane):
    xf = x.astype(F32)
    lo = jnp.where(lane < HEAD_DIM, xf, 0.0).astype(x.dtype)
    hi = jnp.where(lane >= HEAD_DIM, xf, 0.0).astype(x.dtype)
    return lo, hi


def _sb_kernel(q_ref, k_ref, v_ref, t_ref, o_ref, qs_ref, run_ref, acc_ref):
    tq = q_ref.shape[0]
    n_diag = tq // TK
    i = pl.program_id(1)
    lane = lax.broadcasted_iota(jnp.int32, (tq, LANES), 1)
    lane_k = lax.broadcasted_iota(jnp.int32, (TK, LANES), 1)
    for p in range(N_PAIRS):
        lo, hi = _split_heads(q_ref[:, p * LANES:(p + 1) * LANES], lane)
        qs_ref[p, :tq, :] = lo
        qs_ref[p, tq:, :] = hi

    def block(k0, key_offset, first):
        masked = key_offset is not None
        strict = (lax.broadcasted_iota(jnp.int32, (2 * tq, TK), 1) + key_offset
                  < lax.broadcasted_iota(jnp.int32, (2 * tq, TK), 0) % tq) if masked else None

        def group(pairs):
            zs = {p: _dot_nt(qs_ref[p], k_ref[pl.ds(k0, TK), p * LANES:(p + 1) * LANES])
                  for p in pairs}
            lbs, ws = {}, {}
            for p in pairs:
                z = zs[p]
                sp = jnp.maximum(z, 0.0) + jnp.log(1.0 + jnp.exp(-jnp.abs(z)))
                l1m = jnp.where(strict, sp, 0.0) if masked else sp
                l_hi = l1m.astype(BF16)
                l_lo = (l1m - l_hi.astype(F32)).astype(BF16)
                ws[p] = _dot(jnp.concatenate([l_hi, l_lo], axis=1), t_ref[...])
                lbs[p] = z - sp
            for p in pairs:
                arg = lbs[p] + ws[p][:, :TK]
                if not first:
                    arg = arg + run_ref[p]
                a = jnp.exp(arg)
                if masked:
                    a = jnp.where(strict, a, 0.0)
                a = a.astype(BF16)
                v_lo, v_hi = _split_heads(v_ref[pl.ds(k0, TK), p * LANES:(p + 1) * LANES], lane_k)
                upd = _dot(jnp.concatenate([a[:tq], a[tq:]], axis=1),
                           jnp.concatenate([v_lo, v_hi], axis=0))
                if first:
                    acc_ref[p] = upd
                    run_ref[p] = ws[p][:, TK:]
                else:
                    acc_ref[p] += upd
                    run_ref[p] += ws[p][:, TK:]

        for pairs in SB_PAIR_GROUPS:
            group(pairs)

    last = (i + 1) * n_diag - 1
    for jj in range(n_diag):
        block(pl.multiple_of((last - jj) * TK, TK), (n_diag - 1 - jj) * TK, jj == 0)

    def more_blocks(jj):
        run_max = jnp.max(functools.reduce(jnp.maximum, [run_ref[p] for p in range(N_PAIRS)]))
        return jnp.logical_and(jj <= last, run_max > -F32_EXP_UNDERFLOW)

    def body(state):
        jj, _ = state
        block(pl.multiple_of((last - jj) * TK, TK), None, False)
        return jj + 1, more_blocks(jj + 1)

    lax.while_loop(lambda state: state[1], body, (jnp.int32(n_diag), i >= 1))
    for p in range(N_PAIRS):
        o_ref[:, p * LANES:(p + 1) * LANES] = acc_ref[p].astype(o_ref.dtype)


def _sb_call(q, k, v, batch, seq):
    q, k, v = (t.reshape(batch, seq, W_HEADS) for t in (q, k, v))
    j_idx = np.arange(TK)
    suffix = (j_idx[:, None] > j_idx[None, :]).astype(np.float32)
    tmat = -np.concatenate([suffix, np.ones((TK, LANES), np.float32)], axis=1)
    tmat = jnp.asarray(np.concatenate([tmat, tmat], axis=0), BF16)
    blk = pl.BlockSpec((None, TQ_SB, W_HEADS), lambda b, i: (b, i, 0))
    allk = pl.BlockSpec((None, seq, W_HEADS), lambda b, i: (b, 0, 0))
    return pl.pallas_call(
        _sb_kernel,
        out_shape=jax.ShapeDtypeStruct((batch, seq, W_HEADS), BF16),
        grid=(batch, seq // TQ_SB),
        in_specs=[blk, allk, allk, _const_spec(tmat.shape)],
        out_specs=blk,
        scratch_shapes=[pltpu.VMEM((N_PAIRS, 2 * TQ_SB, LANES), BF16),
                        pltpu.VMEM((N_PAIRS, 2 * TQ_SB, LANES), F32),
                        pltpu.VMEM((N_PAIRS, TQ_SB, LANES), F32)],
        compiler_params=_cparams(2),
        name="sb",
    )(q, k, v, tmat)


def _dsa_kernel(qd_st, qi_st, wit_ref, kk_ref, vt_ref, kki_ref, o_ref,
                sc_ref, top_ref, m_ref, acc_ref, buf_ref, qit_ref, qdt_ref, *, topk):
    TQ = qd_st.shape[1]
    i = pl.program_id(1)
    t0 = i * TQ

    n_s = (i + CH_S // TQ) // (CH_S // TQ)

    for q_st, qt_ref in ((qi_st, qit_ref), (qd_st, qdt_ref)):
        for p in range(N_PAIRS):
            qt_ref[p] = jnp.concatenate(
                [q_st[h].astype(F32).T for h in (2 * p, 2 * p + 1)], axis=1).astype(BF16)

    def products(qt_ref, k_ref):
        def produce(k0):
            kch = k_ref[pl.ds(k0, CH_H), :]
            return [_dot(kch, qt_ref[p]) for p in range(N_PAIRS)]
        return produce

    def store_products(vals):
        for p in range(N_PAIRS):
            buf_ref[2 * p] = vals[p][:, :TQ]
            buf_ref[2 * p + 1] = vals[p][:, TQ:]

    def sweep(produce, consume):
        from_buf = lambda h: buf_ref[h]

        def trip(c, prefetch):
            k_a = pl.multiple_of(c * CH_S, CH_S)
            k_b = pl.multiple_of(k_a + CH_H, CH_H)
            vals_b = produce(k_b)
            consume(from_buf, k_a)
            if prefetch:
                vals_next = produce(pl.multiple_of(k_a + CH_S, CH_S))
            consume(lambda h: vals_b[h // 2][:, (h % 2) * TQ:(h % 2 + 1) * TQ], k_b)
            if prefetch:
                store_products(vals_next)

        store_products(produce(0))

        def body(c, _):
            trip(c, True)
            return 0

        lax.fori_loop(0, n_s - 1, body, 0)
        trip(n_s - 1, False)

    wit = wit_ref[...]

    def score_consume(prod, k0):
        score = None
        for h in range(N_HEADS):
            term = jnp.maximum(prod(h), 0.0) * wit[h:h + 1, :]
            score = term if score is None else score + term
        causal = (k0 + lax.broadcasted_iota(jnp.int32, (CH_H, TQ), 0)
                  <= t0 + lax.broadcasted_iota(jnp.int32, (CH_H, TQ), 1))
        score = jnp.where(causal, score, -jnp.inf)
        sc_ref[pl.ds(k0, CH_H), :] = score
        top_ref[pl.ds(k0, CH_H), :] = score.astype(BF16)

    sweep(products(qit_ref, kki_ref), score_consume)

    def float_of_rank(u):
        key = u ^ INT_MIN
        return pltpu.bitcast(jnp.where(key < 0, key ^ np.int32(0x7FFFFFFF), key), F32)

    def count_top_ge(cand):
        one, zero = jnp.ones((BF16_SUBLANES, TQ), BF16), jnp.zeros((BF16_SUBLANES, TQ), BF16)

        def cbody(c, acc):
            k0 = pl.multiple_of(c * CH_S, CH_S)
            tops = top_ref[pl.ds(k0, CH_S), :]
            sums = [acc, zero]
            for j in range(CH_S // BF16_SUBLANES):
                hit = jnp.where(tops[j * BF16_SUBLANES:(j + 1) * BF16_SUBLANES] >= cand, one, zero)
                sums[j % 2] = sums[j % 2] + hit
            return sums[0] + sums[1]

        acc = lax.fori_loop(0, n_s, cbody, zero)
        return jnp.sum(acc.astype(F32), axis=0, keepdims=True)

    def top_body(b, u16):
        cand = u16 | jnp.left_shift(jnp.int32(1), 15 - b)
        cand_f = float_of_rank(jnp.left_shift(cand, 16)).astype(BF16)
        return jnp.where(count_top_ge(jnp.broadcast_to(cand_f, (BF16_SUBLANES, TQ))) >= topk,
                         cand, u16)

    u16 = lax.fori_loop(0, 16, top_body, jnp.zeros((1, TQ), jnp.int32))

    def count(pred):
        def cbody(c, acc):
            k0 = pl.multiple_of(c * CH_S, CH_S)
            sch = sc_ref[pl.ds(k0, CH_S), :].reshape(CH_S // SUBLANES, SUBLANES, TQ)
            return acc + jnp.sum(jnp.where(pred(sch), 1, 0), axis=0)
        acc = lax.fori_loop(0, n_s, cbody, jnp.zeros((SUBLANES, TQ), jnp.int32))
        tot = jnp.sum(acc.astype(F32), axis=0, keepdims=True)
        return jnp.broadcast_to(tot, (SUBLANES, TQ))

    base = jnp.broadcast_to(jnp.left_shift(u16, 16) - (1 << 15), (SUBLANES, TQ))

    def window_body(b, state):
        off, n_off = state
        cand = off | jnp.left_shift(jnp.int32(1), SELECT_WINDOW_BITS - 1 - b)
        cand_f = float_of_rank(base + cand)
        n_cand = count(lambda sch: sch >= cand_f[None])
        take = n_cand >= topk
        return jnp.where(take, cand, off), jnp.where(take, n_cand, n_off)

    few_keys = t0 + lax.broadcasted_iota(jnp.int32, (SUBLANES, TQ), 1) < topk
    unknown = jnp.full((SUBLANES, TQ), -1.0, F32)
    state = lax.fori_loop(0, SELECT_EARLY_PASSES, window_body,
                          (jnp.zeros((SUBLANES, TQ), jnp.int32), unknown))
    still_open = jnp.max(jnp.where(jnp.logical_or(state[1] == topk, few_keys), 0.0, 1.0))
    offset, n_off = lax.cond(
        still_open > 0.0,
        lambda st: lax.fori_loop(SELECT_EARLY_PASSES, SELECT_WINDOW_BITS, window_body, st),
        lambda st: st, state)
    thr = jnp.where(few_keys, jnp.finfo(F32).min, float_of_rank(base + offset))
    thr_row = thr[0:1, :]

    n_ge = jnp.where(few_keys, 0.0, jnp.where(n_off < 0.0, topk + 1.0, n_off))

    @pl.when(jnp.max(n_ge) > topk)
    def _():
        need = topk - count(lambda sch: sch > thr[None])[0:1, :]
        before = jnp.where(lax.broadcasted_iota(jnp.int32, (CH_H, CH_H), 1)
                           < lax.broadcasted_iota(jnp.int32, (CH_H, CH_H), 0), 1.0, 0.0).astype(BF16)

        def tie_body(c, seen):
            k0s = [pl.multiple_of(c * CH_S + j * CH_H, CH_H) for j in range(CH_S // CH_H)]
            schs = [sc_ref[pl.ds(k0, CH_H), :] for k0 in k0s]
            eqs = [jnp.where(sch == thr_row, 1.0, 0.0) for sch in schs]
            ranks = [_dot(before, eq.astype(BF16)) for eq in eqs]
            for k0, sch, eq, rank in zip(k0s, schs, eqs, ranks):
                tie_rank = jnp.where(sch == thr_row, rank + seen, -1.0)
                sc_ref[pl.ds(k0, CH_H), :] = jnp.where(tie_rank >= need, -jnp.inf, sch)
                seen = seen + jnp.sum(eq, axis=0, keepdims=True)
            return seen

        lax.fori_loop(0, n_s, tie_body, jnp.zeros((1, TQ), F32))

    m_ref[...] = jnp.full(m_ref.shape, NEG_BIG, F32)
    acc_ref[...] = jnp.zeros(acc_ref.shape, F32)

    def attn_consume(prod, k0):
        vch = vt_ref[:, pl.ds(k0, CH_H)]
        sel = sc_ref[pl.ds(k0, CH_H), :] >= thr_row
        for p in range(N_PAIRS):
            probs, alphas = [], []
            for h in (2 * p, 2 * p + 1):
                lg = jnp.where(sel, prod(h), NEG_BIG)
                m_old = m_ref[h]
                m_new = jnp.maximum(m_old, jnp.max(lg, axis=0, keepdims=True))
                m_ref[h] = m_new
                alphas.append(jnp.exp2(m_old - m_new))
                probs.append(jnp.exp2(lg - m_new).astype(BF16))
            upd = _dot(vch, jnp.concatenate(probs, axis=1))
            acc_ref[p] = acc_ref[p] * jnp.concatenate(alphas, axis=1) + upd

    sweep(products(qdt_ref, kk_ref), attn_consume)
    for p in range(N_PAIRS):
        acc = acc_ref[p]
        out_t = acc[:HEAD_DIM] / acc[HEAD_DIM:HEAD_DIM + 1]
        pair = jnp.concatenate([out_t[:, :TQ], out_t[:, TQ:]], axis=0)
        o_ref[:, p * LANES:(p + 1) * LANES] = pair.T.astype(o_ref.dtype)


def _dsa_call(qd_rows, qi_rows, wit, kk, vt, kki, batch, seq):
    r3 = lambda t: t.reshape(batch, seq, t.shape[-1])
    kk, kki = map(r3, (kk, kki))
    topk = min(TOPK_MAX, seq // 4)
    assert seq // BF16_SUBLANES <= BF16_MAX_EXACT_COUNT
    TQ = TQ_DSA
    nq = seq // TQ
    rows = pl.BlockSpec((None, N_HEADS, TQ, LANES), lambda b, i: (b * nq + i, 0, 0, 0))
    allk = pl.BlockSpec((None, seq, LANES), lambda b, i: (b, 0, 0))
    return pl.pallas_call(
        functools.partial(_dsa_kernel, topk=topk),
        out_shape=jax.ShapeDtypeStruct((batch, seq, W_HEADS), BF16),
        grid=(batch, nq),
        in_specs=[rows, rows,
                  pl.BlockSpec((None, N_HEADS, TQ), lambda b, i: (b, 0, i)),
                  allk, pl.BlockSpec((None, VT_ROWS, seq), lambda b, i: (b, 0, 0)), allk],
        out_specs=pl.BlockSpec((None, TQ, W_HEADS), lambda b, i: (b, i, 0)),
        scratch_shapes=[pltpu.VMEM((seq, TQ), F32),
                        pltpu.VMEM((seq, TQ), BF16),
                        pltpu.VMEM((N_HEADS, 1, TQ), F32),
                        pltpu.VMEM((N_PAIRS, VT_ROWS, 2 * TQ), F32),
                        pltpu.VMEM((N_HEADS, CH_H, TQ), F32),
                        pltpu.VMEM((N_PAIRS, LANES, 2 * TQ), BF16),
                        pltpu.VMEM((N_PAIRS, LANES, 2 * TQ), BF16)],
        compiler_params=_cparams(2),
        name="dsa",
    )(qd_rows, qi_rows, wit, kk, vt, kki)


def _mix_kernel(h_ref, n_ref, ysb_ref, yd_ref, mod_ref, wps_ref, wpd_ref, wg_ref, bg_ref,
                wo_ref, o_ref):
    d = h_ref.shape[1]
    y_sb = _dot(ysb_ref[...], wps_ref[...])
    y_d = _dot(yd_ref[...], wpd_ref[...])
    gates = jax.nn.sigmoid(_dot(n_ref[...], wg_ref[...]) + bg_ref[...])
    merged = (gates[:, :d] * y_sb + gates[:, d:] * y_d).astype(BF16)
    o_ref[...] = h_ref[...] + mod_ref[5] * _dot(merged, wo_ref[...])


def _mix_call(h, n2, y_sb, y_d, mod, w_proj_sb, w_proj_dsa, w_gate, b_gate, w_out, *, seq, tm=512):
    n_tok, d = h.shape
    tok = lambda w: pl.BlockSpec((tm, w), lambda i: (i, 0))
    ws = [w.astype(BF16) for w in (w_proj_sb, w_proj_dsa, w_gate)]
    bg = b_gate.reshape(1, -1)
    wo = w_out.astype(BF16)
    return pl.pallas_call(
        _mix_kernel,
        out_shape=jax.ShapeDtypeStruct((n_tok, d), F32),
        grid=(n_tok // tm,),
        in_specs=[tok(d), tok(d), tok(W_HEADS), tok(W_HEADS), _mod_spec(seq // tm, d)]
                 + [_const_spec(w.shape) for w in (*ws, bg, wo)],
        out_specs=tok(d),
        compiler_params=_cparams(1),
        name="mix",
    )(h, n2, y_sb, y_d, mod, *ws, bg, wo)


def kernel(x, c, w_ada, b_ada, g_ffn1, w1_ffn1, w3_ffn1, w2_ffn1, g_mix, w_in, g_q_dsa, g_k_dsa,
           w_proj_sb, w_proj_dsa, w_gate, b_gate, w_out, g_ffn2, w1_ffn2, w3_ffn2, w2_ffn2):
    batch, seq, d = x.shape
    h = x.reshape(batch * seq, d)
    for l in range(w_ada.shape[0]):
        bf = lambda w: w[l].astype(BF16)
        mod = _mod_call(c, w_ada[l], b_ada[l])
        h, n2 = _ffn_call(h, mod, g_ffn1[l], bf(w1_ffn1), bf(w3_ffn1), bf(w2_ffn1), g_mix[l],
                          mod_base=0, emit_next=True, seq=seq)
        q_sb, k_sb, v_sb, q_d, kk, q_i, kki, vt, wit = _inproj_call(
            n2, w_in[l], g_q_dsa[l], g_k_dsa[l], batch, seq)
        y_sb = _sb_call(q_sb, k_sb, v_sb, batch, seq).reshape(batch * seq, W_HEADS)
        y_d = _dsa_call(q_d, q_i, wit, kk, vt, kki, batch, seq).reshape(batch * seq, W_HEADS)
        h = _mix_call(h, n2, y_sb, y_d, mod, w_proj_sb[l], w_proj_dsa[l], w_gate[l], b_gate[l],
                      w_out[l], seq=seq)
        (h,) = _ffn_call(h, mod, g_ffn2[l], bf(w1_ffn2), bf(w3_ffn2), bf(w2_ffn2), g_ffn2[l],
                         mod_base=6, emit_next=False, seq=seq)
    return h.reshape(batch, seq, d)
```

```python
import functools

import numpy as np
import jax
import jax.numpy as jnp
from jax import lax
from jax.experimental import pallas as pl
from jax.experimental.pallas import tpu as pltpu

F32 = jnp.float32
BF16 = jnp.bfloat16

HEAD_DIM = 64
N_HEADS = 8
W_HEADS = N_HEADS * HEAD_DIM
N_PAIRS = N_HEADS // 2
LANES = 128
SUBLANES = 8
BF16_SUBLANES = 16
BF16_MAX_EXACT_COUNT = 256
SELECT_WINDOW_BITS = 17
SELECT_EARLY_PASSES = 11
TOPK_MAX = 256
N_MOD = 9
RMS_EPS = 1e-6
D_IDX = 64

TQ_SB = 256
TQ_DSA = 512
TK = 128
SB_PAIR_GROUPS = ((0, 1, 2, 3),)
CH_S = 512
CH_H = 256
VT_ROWS = 80
LANE_POS_HI, LANE_POS_LO = HEAD_DIM, HEAD_DIM + 1
N_POS_LANES = 2
N_SLOPE_PARTS = 3
LOG2E = float(np.log2(np.e))
NEG_BIG = -1e30
F32_EXP_UNDERFLOW = 104.0
INT_MIN = np.int32(-2**31)
VMEM_LIMIT = 56 * 1024 * 1024


def _cparams(n_axes):
    return pltpu.CompilerParams(
        dimension_semantics=("arbitrary",) * n_axes, vmem_limit_bytes=VMEM_LIMIT)


def _const_spec(shape):
    nd = len(shape)
    return pl.BlockSpec(shape, lambda *_: (0,) * nd, pipeline_mode=pl.Buffered(1))


def _dot(a, b):
    return jnp.dot(a, b, preferred_element_type=F32)


def _dot_nt(a, b):
    return lax.dot_general(a, b, (((1,), (1,)), ((), ())), preferred_element_type=F32)


def _mod_kernel(c_ref, w_ref, b_ref, o_ref, sc_ref):
    nb, kdim, _ = c_ref.shape
    tn = w_ref.shape[1]
    nch = tn // LANES

    @pl.when(pl.program_id(0) == 0)
    def _():
        c = c_ref[...]
        sc_ref[...] = c * jax.nn.sigmoid(c)

    def body(kb, accs):
        k0 = pl.multiple_of(kb * SUBLANES, SUBLANES)
        wblk = w_ref[pl.ds(k0, SUBLANES), :]
        out = []
        for b in range(nb):
            sb = sc_ref[b, pl.ds(k0, SUBLANES), :]
            for ch in range(nch):
                out.append(accs[b * nch + ch] + wblk[:, ch * LANES:(ch + 1) * LANES] * sb)
        return tuple(out)

    accs = lax.fori_loop(0, kdim // SUBLANES, body,
                         tuple(jnp.zeros((SUBLANES, LANES), F32) for _ in range(nb * nch)))
    for b in range(nb):
        row = jnp.concatenate(
            [jnp.sum(accs[b * nch + ch], axis=0, keepdims=True) for ch in range(nch)], axis=1)
        o_ref[0, b] = row + b_ref[...]


def _mod_call(c, w_ada, b_ada):
    nb, d = c.shape
    n_out = w_ada.shape[1]
    c_rep = jnp.broadcast_to(c[:, :, None], (nb, d, LANES))
    return pl.pallas_call(
        _mod_kernel,
        out_shape=jax.ShapeDtypeStruct((n_out // d, nb, 1, d), F32),
        grid=(n_out // d,),
        in_specs=[_const_spec((nb, d, LANES)),
                  pl.BlockSpec((d, d), lambda j: (0, j)),
                  pl.BlockSpec((1, d), lambda j: (0, j))],
        out_specs=pl.BlockSpec((1, nb, 1, d), lambda j: (j, 0, 0, 0)),
        scratch_shapes=[pltpu.VMEM((nb, d, LANES), F32)],
        compiler_params=_cparams(1),
        name="mod",
    )(c_rep, w_ada, b_ada.reshape(1, n_out))


def _norm_mod(h, g, shift, scale):
    y = h * lax.rsqrt(jnp.mean(h * h, axis=-1, keepdims=True) + RMS_EPS)
    return (y * g) * (1.0 + scale) + shift


def _ffn_kernel(h_ref, mod_ref, g_ref, w1_ref, w3_ref, w2_ref, gn_ref, *out_refs,
                mod_base, emit_next, n_chunks):
    o_ref = out_refs[0]
    h = h_ref[...]
    n = _norm_mod(h, g_ref[...], mod_ref[mod_base], mod_ref[mod_base + 1]).astype(BF16)
    fc = w1_ref.shape[1] // n_chunks
    acc = jnp.zeros(h.shape, F32)
    for c in range(n_chunks):
        a = _dot(n, w1_ref[:, c * fc:(c + 1) * fc])
        b = _dot(n, w3_ref[:, c * fc:(c + 1) * fc])
        act = (a * jax.nn.sigmoid(a) * b).astype(BF16)
        acc = acc + _dot(act, w2_ref[c * fc:(c + 1) * fc, :])
    hn = h + (0.5 * mod_ref[mod_base + 2]) * acc
    o_ref[...] = hn
    if emit_next:
        out_refs[1][...] = _norm_mod(hn, gn_ref[...], mod_ref[mod_base + 3],
                                     mod_ref[mod_base + 4]).astype(BF16)


def _mod_spec(tiles_per_batch, d):
    return pl.BlockSpec((N_MOD, None, 1, d), lambda i: (0, i // tiles_per_batch, 0, 0))


def _ffn_call(h, mod, g, w1, w3, w2, g_next, *, mod_base, emit_next, seq, tm=512):
    n_tok, d = h.shape
    f = w1.shape[1]
    tok = pl.BlockSpec((tm, d), lambda i: (i, 0))
    out_shape = [jax.ShapeDtypeStruct((n_tok, d), F32)]
    out_specs = [tok]
    if emit_next:
        out_shape.append(jax.ShapeDtypeStruct((n_tok, d), BF16))
        out_specs.append(tok)
    return pl.pallas_call(
        functools.partial(_ffn_kernel, mod_base=mod_base, emit_next=emit_next, n_chunks=2),
        out_shape=out_shape,
        grid=(n_tok // tm,),
        in_specs=[tok, _mod_spec(seq // tm, d), _const_spec((1, d)),
                  _const_spec((d, f)), _const_spec((d, f)), _const_spec((f, d)),
                  _const_spec((1, d))],
        out_specs=out_specs,
        compiler_params=_cparams(1),
        name="ffn",
    )(h, mod, g.reshape(1, d), w1, w3, w2, g_next.reshape(1, d))


def _alibi_slope_parts():
    out = []
    for h in range(N_HEADS):
        rest, parts = np.float32(2.0 ** (-8.0 * (h + 1) / N_HEADS) * LOG2E), []
        for _ in range(N_SLOPE_PARTS):
            parts.append(np.float32(np.asarray(rest, dtype=BF16)))
            rest = np.float32(rest - parts[-1])
        out.append(parts)
    return out


def _store_head_rows(o_ref, x, slope_parts):
    tm = x.shape[0]
    lane = lax.broadcasted_iota(jnp.int32, (tm, LANES), 1)
    for h in range(N_HEADS):
        rows = x[:, (h // 2) * LANES:(h // 2 + 1) * LANES]
        if h % 2:
            rows = pltpu.roll(rows, HEAD_DIM, axis=1)
        rows = jnp.where(lane < HEAD_DIM, rows, 0.0)
        if slope_parts is not None:
            for rep, part in enumerate(slope_parts[h]):
                rows = jnp.where(lane == LANE_POS_HI + rep * N_POS_LANES, part * HEAD_DIM, rows)
                rows = jnp.where(lane == LANE_POS_LO + rep * N_POS_LANES, part, rows)
        o_ref[:, h] = rows.astype(BF16).reshape(tm // TQ_DSA, TQ_DSA, LANES)


def _inproj_kernel(n_ref, wsb_ref, wqd_ref, wkd_ref, wqi_ref, wki_ref, wvt_ref, wwt_ref,
                   gq_ref, gk_ref, hm_ref,
                   qsb_ref, ksb_ref, vsb_ref, qd_ref, kk_ref, qi_ref, kki_ref, vt_ref, wit_ref,
                   *, tiles_per_batch):
    n = n_ref[...]
    tm = n.shape[0]
    qk_scale = HEAD_DIM ** -0.5

    qd = _dot(n, wqd_ref[...])
    ms = _dot((qd * qd).astype(BF16), hm_ref[...])
    _store_head_rows(qd_ref, qd * lax.rsqrt(ms + RMS_EPS) * (gq_ref[...] * (qk_scale * LOG2E)),
                     _alibi_slope_parts())

    kd = _dot(n, wkd_ref[...])
    kn = kd * lax.rsqrt(jnp.mean(kd * kd, axis=-1, keepdims=True) + RMS_EPS) * gk_ref[...]
    lane = lax.broadcasted_iota(jnp.int32, (tm, LANES), 1)
    pos = ((pl.program_id(0) % tiles_per_batch) * tm
           + lax.broadcasted_iota(jnp.int32, (tm, LANES), 0))
    feat = jnp.zeros((tm, LANES), jnp.int32)
    for rep in range(N_SLOPE_PARTS):
        feat = jnp.where(lane == LANE_POS_HI + rep * N_POS_LANES,
                         pos >> (HEAD_DIM.bit_length() - 1), feat)
        feat = jnp.where(lane == LANE_POS_LO + rep * N_POS_LANES, pos & (HEAD_DIM - 1), feat)
    feat = feat.astype(F32)
    kk_ref[...] = jnp.where(lane < HEAD_DIM, kn, feat).astype(BF16)

    _store_head_rows(qi_ref, _dot(n, wqi_ref[...]), None)
    kki_ref[...] = _dot(n, wki_ref[...]).astype(BF16)
    vt = _dot_nt(wvt_ref[...], n)
    row = lax.broadcasted_iota(jnp.int32, vt.shape, 0)
    vt_ref[...] = jnp.where(row == HEAD_DIM, 1.0, vt).astype(BF16)
    wit_ref[...] = _dot_nt(wwt_ref[...], n)[:N_HEADS]

    sb = _dot(n, wsb_ref[...])
    qsb_ref[...] = (sb[:, :W_HEADS] * qk_scale).astype(BF16)
    ksb_ref[...] = sb[:, W_HEADS:2 * W_HEADS].astype(BF16)
    vsb_ref[...] = sb[:, 2 * W_HEADS:].astype(BF16)


def _inproj_call(n2, w_in, g_q, g_k, batch, seq, tm=1024):
    n_tok, d = n2.shape
    sizes = (W_HEADS, W_HEADS, W_HEADS, W_HEADS, HEAD_DIM, HEAD_DIM, N_HEADS * D_IDX, D_IDX, N_HEADS)
    offs = np.concatenate([[0], np.cumsum(sizes)])
    col = lambda k: w_in[:, offs[k]:offs[k + 1]]
    idx_scale = (D_IDX ** -0.5) * (N_HEADS ** -0.5)
    w_sb = jnp.concatenate([col(0), col(1), col(2)], axis=1).astype(BF16)
    w_qd = col(3).astype(BF16)
    w_kd = jnp.concatenate([col(4), col(4)], axis=1).astype(BF16)
    w_qi = col(6).astype(BF16)
    w_ki = jnp.pad(col(7), ((0, 0), (0, LANES - D_IDX))).astype(BF16)
    w_vt = jnp.pad(col(5).T, ((0, VT_ROWS - HEAD_DIM), (0, 0))).astype(BF16)
    w_wt = jnp.pad((col(8) * idx_scale).T, ((0, 16 - N_HEADS), (0, 0))).astype(BF16)
    gq = jnp.tile(g_q, N_HEADS).reshape(1, W_HEADS)
    gk = jnp.tile(g_k, 2).reshape(1, LANES)
    head_of = np.arange(W_HEADS) // HEAD_DIM
    head_mean = jnp.asarray((head_of[:, None] == head_of[None, :]) / HEAD_DIM, BF16)

    tpb = seq // tm
    tok = lambda w: pl.BlockSpec((tm, w), lambda i: (i, 0))
    tr = lambda r: pl.BlockSpec((None, r, tm), lambda i: (i // tpb, 0, i % tpb))
    sd = lambda w, dt: jax.ShapeDtypeStruct((n_tok, w), dt)
    consts = (w_sb, w_qd, w_kd, w_qi, w_ki, w_vt, w_wt, gq, gk, head_mean)
    head_rows = jax.ShapeDtypeStruct((n_tok // TQ_DSA, N_HEADS, TQ_DSA, LANES), BF16)
    head_rows_spec = pl.BlockSpec((tm // TQ_DSA, N_HEADS, TQ_DSA, LANES), lambda i: (i, 0, 0, 0))
    return pl.pallas_call(
        functools.partial(_inproj_kernel, tiles_per_batch=tpb),
        out_shape=[sd(W_HEADS, BF16)] * 3 + [head_rows, sd(LANES, BF16), head_rows, sd(LANES, BF16),
                                             jax.ShapeDtypeStruct((batch, VT_ROWS, seq), BF16),
                                             jax.ShapeDtypeStruct((batch, N_HEADS, seq), F32)],
        grid=(n_tok // tm,),
        in_specs=[tok(d)] + [_const_spec(w.shape) for w in consts],
        out_specs=[tok(W_HEADS)] * 3 + [head_rows_spec, tok(LANES), head_rows_spec, tok(LANES),
                                        tr(VT_ROWS), tr(N_HEADS)],
        compiler_params=_cparams(1),
        name="inproj",
    )(n2, *consts)


def _split_heads(x, lane):
    xf = x.astype(F32)
    lo = jnp.where(lane < HEAD_DIM, xf, 0.0).astype(x.dtype)
    hi = jnp.where(lane >= HEAD_DIM, xf, 0.0).astype(x.dtype)
    return lo, hi


def _sb_kernel(q_ref, k_ref, v_ref, t_ref, o_ref, qs_ref, run_ref, acc_ref):
    tq = q_ref.shape[0]
    n_diag = tq // TK
    i = pl.program_id(1)
    lane = lax.broadcasted_iota(jnp.int32, (tq, LANES), 1)
    lane_k = lax.broadcasted_iota(jnp.int32, (TK, LANES), 1)
    for p in range(N_PAIRS):
        lo, hi = _split_heads(q_ref[:, p * LANES:(p + 1) * LANES], lane)
        qs_ref[p, :tq, :] = lo
        qs_ref[p, tq:, :] = hi

    def block(k0, key_offset, first):
        masked = key_offset is not None
        strict = (lax.broadcasted_iota(jnp.int32, (2 * tq, TK), 1) + key_offset
                  < lax.broadcasted_iota(jnp.int32, (2 * tq, TK), 0) % tq) if masked else None

        def group(pairs):
            zs = {p: _dot_nt(qs_ref[p], k_ref[pl.ds(k0, TK), p * LANES:(p + 1) * LANES])
                  for p in pairs}
            lbs, ws = {}, {}
            for p in pairs:
                z = zs[p]
                sp = jnp.maximum(z, 0.0) + jnp.log(1.0 + jnp.exp(-jnp.abs(z)))
                l1m = jnp.where(strict, sp, 0.0) if masked else sp
                l_hi = l1m.astype(BF16)
                l_lo = (l1m - l_hi.astype(F32)).astype(BF16)
                ws[p] = _dot(jnp.concatenate([l_hi, l_lo], axis=1), t_ref[...])
                lbs[p] = z - sp
            for p in pairs:
                arg = lbs[p] + ws[p][:, :TK]
                if not first:
                    arg = arg + run_ref[p]
                a = jnp.exp(arg)
                if masked:
                    a = jnp.where(strict, a, 0.0)
                a = a.astype(BF16)
                v_lo, v_hi = _split_heads(v_ref[pl.ds(k0, TK), p * LANES:(p + 1) * LANES], lane_k)
                upd = _dot(jnp.concatenate([a[:tq], a[tq:]], axis=1),
                           jnp.concatenate([v_lo, v_hi], axis=0))
                if first:
                    acc_ref[p] = upd
                    run_ref[p] = ws[p][:, TK:]
                else:
                    acc_ref[p] += upd
                    run_ref[p] += ws[p][:, TK:]

        for pairs in SB_PAIR_GROUPS:
            group(pairs)

    last = (i + 1) * n_diag - 1
    for jj in range(n_diag):
        block(pl.multiple_of((last - jj) * TK, TK), (n_diag - 1 - jj) * TK, jj == 0)

    def more_blocks(jj):
        run_max = jnp.max(functools.reduce(jnp.maximum, [run_ref[p] for p in range(N_PAIRS)]))
        return jnp.logical_and(jj <= last, run_max > -F32_EXP_UNDERFLOW)

    def body(state):
        jj, _ = state
        block(pl.multiple_of((last - jj) * TK, TK), None, False)
        return jj + 1, more_blocks(jj + 1)

    lax.while_loop(lambda state: state[1], body, (jnp.int32(n_diag), i >= 1))
    for p in range(N_PAIRS):
        o_ref[:, p * LANES:(p + 1) * LANES] = acc_ref[p].astype(o_ref.dtype)


def _sb_call(q, k, v, batch, seq):
    q, k, v = (t.reshape(batch, seq, W_HEADS) for t in (q, k, v))
    j_idx = np.arange(TK)
    suffix = (j_idx[:, None] > j_idx[None, :]).astype(np.float32)
    tmat = -np.concatenate([suffix, np.ones((TK, LANES), np.float32)], axis=1)
    tmat = jnp.asarray(np.concatenate([tmat, tmat], axis=0), BF16)
    blk = pl.BlockSpec((None, TQ_SB, W_HEADS), lambda b, i: (b, i, 0))
    allk = pl.BlockSpec((None, seq, W_HEADS), lambda b, i: (b, 0, 0))
    return pl.pallas_call(
        _sb_kernel,
        out_shape=jax.ShapeDtypeStruct((batch, seq, W_HEADS), BF16),
        grid=(batch, seq // TQ_SB),
        in_specs=[blk, allk, allk, _const_spec(tmat.shape)],
        out_specs=blk,
        scratch_shapes=[pltpu.VMEM((N_PAIRS, 2 * TQ_SB, LANES), BF16),
                        pltpu.VMEM((N_PAIRS, 2 * TQ_SB, LANES), F32),
                        pltpu.VMEM((N_PAIRS, TQ_SB, LANES), F32)],
        compiler_params=_cparams(2),
        name="sb",
    )(q, k, v, tmat)


def _dsa_kernel(qd_st, qi_st, wit_ref, kk_ref, vt_ref, kki_ref, o_ref,
                sc_ref, top_ref, m_ref, acc_ref, buf_ref, qit_ref, qdt_ref, *, topk):
    TQ = qd_st.shape[1]
    i = pl.program_id(1)
    t0 = i * TQ

    n_s = (i + CH_S // TQ) // (CH_S // TQ)

    for q_st, qt_ref in ((qi_st, qit_ref), (qd_st, qdt_ref)):
        for p in range(N_PAIRS):
            qt_ref[p] = jnp.concatenate(
                [q_st[h].astype(F32).T for h in (2 * p, 2 * p + 1)], axis=1).astype(BF16)

    def products(qt_ref, k_ref):
        def produce(k0):
            kch = k_ref[pl.ds(k0, CH_H), :]
            return [_dot(kch, qt_ref[p]) for p in range(N_PAIRS)]
        return produce

    def store_products(vals):
        for p in range(N_PAIRS):
            buf_ref[2 * p] = vals[p][:, :TQ]
            buf_ref[2 * p + 1] = vals[p][:, TQ:]

    def sweep(produce, consume):
        from_buf = lambda h: buf_ref[h]

        def trip(c, prefetch):
            k_a = pl.multiple_of(c * CH_S, CH_S)
            k_b = pl.multiple_of(k_a + CH_H, CH_H)
            vals_b = produce(k_b)
            consume(from_buf, k_a)
            if prefetch:
                vals_next = produce(pl.multiple_of(k_a + CH_S, CH_S))
            consume(lambda h: vals_b[h // 2][:, (h % 2) * TQ:(h % 2 + 1) * TQ], k_b)
            if prefetch:
                store_products(vals_next)

        store_products(produce(0))

        def body(c, _):
            trip(c, True)
            return 0

        lax.fori_loop(0, n_s - 1, body, 0)
        trip(n_s - 1, False)

    wit = wit_ref[...]

    def score_consume(prod, k0):
        score = None
        for h in range(N_HEADS):
            term = jnp.maximum(prod(h), 0.0) * wit[h:h + 1, :]
            score = term if score is None else score + term
        causal = (k0 + lax.broadcasted_iota(jnp.int32, (CH_H, TQ), 0)
                  <= t0 + lax.broadcasted_iota(jnp.int32, (CH_H, TQ), 1))
        score = jnp.where(causal, score, -jnp.inf)
        sc_ref[pl.ds(k0, CH_H), :] = score
        top_ref[pl.ds(k0, CH_H), :] = score.astype(BF16)

    sweep(products(qit_ref, kki_ref), score_consume)

    def float_of_rank(u):
        key = u ^ INT_MIN
        return pltpu.bitcast(jnp.where(key < 0, key ^ np.int32(0x7FFFFFFF), key), F32)

    def count_top(pred):
        one, zero = jnp.ones((BF16_SUBLANES, TQ), BF16), jnp.zeros((BF16_SUBLANES, TQ), BF16)

        def cbody(c, acc):
            k0 = pl.multiple_of(c * CH_S, CH_S)
            tops = top_ref[pl.ds(k0, CH_S), :]
            sums = [acc, zero]
            for j in range(CH_S // BF16_SUBLANES):
                hit = jnp.where(pred(tops[j * BF16_SUBLANES:(j + 1) * BF16_SUBLANES]), one, zero)
                sums[j % 2] = sums[j % 2] + hit
            return sums[0] + sums[1]

        acc = lax.fori_loop(0, n_s, cbody, zero)
        return jnp.sum(acc.astype(F32), axis=0, keepdims=True)

    def top_body(b, u16):
        cand = u16 | jnp.left_shift(jnp.int32(1), 15 - b)
        cand_f = float_of_rank(jnp.left_shift(cand, 16)).astype(BF16)
        cand_t = jnp.broadcast_to(cand_f, (BF16_SUBLANES, TQ))
        return jnp.where(count_top(lambda t: t >= cand_t) >= topk, cand, u16)

    u16 = lax.fori_loop(0, 16, top_body, jnp.zeros((1, TQ), jnp.int32))

    zero_t = jnp.zeros((BF16_SUBLANES, TQ), BF16)
    n_nonneg = jnp.broadcast_to(count_top(lambda t: t >= zero_t), (SUBLANES, TQ))
    n_pos = jnp.broadcast_to(count_top(lambda t: t > zero_t), (SUBLANES, TQ))

    def count(pred):
        def cbody(c, acc):
            k0 = pl.multiple_of(c * CH_S, CH_S)
            sch = sc_ref[pl.ds(k0, CH_S), :].reshape(CH_S // SUBLANES, SUBLANES, TQ)
            return acc + jnp.sum(jnp.where(pred(sch), 1, 0), axis=0)
        acc = lax.fori_loop(0, n_s, cbody, jnp.zeros((SUBLANES, TQ), jnp.int32))
        tot = jnp.sum(acc.astype(F32), axis=0, keepdims=True)
        return jnp.broadcast_to(tot, (SUBLANES, TQ))

    base = jnp.broadcast_to(jnp.left_shift(u16, 16) - (1 << 15), (SUBLANES, TQ))

    def window_body(b, state):
        off, n_off = state
        cand = off | jnp.left_shift(jnp.int32(1), SELECT_WINDOW_BITS - 1 - b)
        cand_f = float_of_rank(base + cand)
        n_cand = count(lambda sch: sch >= cand_f[None])
        take = n_cand >= topk
        return jnp.where(take, cand, off), jnp.where(take, n_cand, n_off)

    few_keys = t0 + lax.broadcasted_iota(jnp.int32, (SUBLANES, TQ), 1) < topk
    unknown = jnp.full((SUBLANES, TQ), -1.0, F32)
    state = lax.fori_loop(0, SELECT_EARLY_PASSES, window_body,
                          (jnp.zeros((SUBLANES, TQ), jnp.int32), unknown))
    zero_kth = jnp.logical_and(jnp.logical_and(n_pos < topk, n_nonneg >= topk),
                               jnp.logical_not(few_keys))
    settled = jnp.logical_or(jnp.logical_or(state[1] == topk, few_keys), zero_kth)
    still_open = jnp.max(jnp.where(settled, 0.0, 1.0))
    offset, n_off = lax.cond(
        still_open > 0.0,
        lambda st: lax.fori_loop(SELECT_EARLY_PASSES, SELECT_WINDOW_BITS, window_body, st),
        lambda st: st, state)
    thr = jnp.where(few_keys, jnp.finfo(F32).min,
                    jnp.where(zero_kth, 0.0, float_of_rank(base + offset)))
    thr_row = thr[0:1, :]

    n_ge = jnp.where(few_keys, 0.0,
                     jnp.where(zero_kth, n_nonneg, jnp.where(n_off < 0.0, topk + 1.0, n_off)))

    @pl.when(jnp.max(n_ge) > topk)
    def _():
        need = topk - count(lambda sch: sch > thr[None])[0:1, :]
        before = jnp.where(lax.broadcasted_iota(jnp.int32, (CH_H, CH_H), 1)
                           < lax.broadcasted_iota(jnp.int32, (CH_H, CH_H), 0), 1.0, 0.0).astype(BF16)

        def tie_body(c, seen):
            k0s = [pl.multiple_of(c * CH_S + j * CH_H, CH_H) for j in range(CH_S // CH_H)]
            schs = [sc_ref[pl.ds(k0, CH_H), :] for k0 in k0s]
            eqs = [jnp.where(sch == thr_row, 1.0, 0.0) for sch in schs]
            ranks = [_dot(before, eq.astype(BF16)) for eq in eqs]
            for k0, sch, eq, rank in zip(k0s, schs, eqs, ranks):
                tie_rank = jnp.where(sch == thr_row, rank + seen, -1.0)
                sc_ref[pl.ds(k0, CH_H), :] = jnp.where(tie_rank >= need, -jnp.inf, sch)
                seen = seen + jnp.sum(eq, axis=0, keepdims=True)
            return seen

        lax.fori_loop(0, n_s, tie_body, jnp.zeros((1, TQ), F32))

    m_ref[...] = jnp.full(m_ref.shape, NEG_BIG, F32)
    acc_ref[...] = jnp.zeros(acc_ref.shape, F32)

    def attn_consume(prod, k0):
        vch = vt_ref[:, pl.ds(k0, CH_H)]
        sel = sc_ref[pl.ds(k0, CH_H), :] >= thr_row
        for p in range(N_PAIRS):
            probs, alphas = [], []
            for h in (2 * p, 2 * p + 1):
                lg = jnp.where(sel, prod(h), NEG_BIG)
                m_old = m_ref[h]
                m_new = jnp.maximum(m_old, jnp.max(lg, axis=0, keepdims=True))
                m_ref[h] = m_new
                alphas.append(jnp.exp2(m_old - m_new))
                probs.append(jnp.exp2(lg - m_new).astype(BF16))
            upd = _dot(vch, jnp.concatenate(probs, axis=1))
            acc_ref[p] = acc_ref[p] * jnp.concatenate(alphas, axis=1) + upd

    sweep(products(qdt_ref, kk_ref), attn_consume)
    for p in range(N_PAIRS):
        acc = acc_ref[p]
        out_t = acc[:HEAD_DIM] / acc[HEAD_DIM:HEAD_DIM + 1]
        pair = jnp.concatenate([out_t[:, :TQ], out_t[:, TQ:]], axis=0)
        o_ref[:, p * LANES:(p + 1) * LANES] = pair.T.astype(o_ref.dtype)


def _dsa_call(qd_rows, qi_rows, wit, kk, vt, kki, batch, seq):
    r3 = lambda t: t.reshape(batch, seq, t.shape[-1])
    kk, kki = map(r3, (kk, kki))
    topk = min(TOPK_MAX, seq // 4)
    assert seq // BF16_SUBLANES <= BF16_MAX_EXACT_COUNT
    TQ = TQ_DSA
    nq = seq // TQ
    rows = pl.BlockSpec((None, N_HEADS, TQ, LANES), lambda b, i: (b * nq + i, 0, 0, 0))
    allk = pl.BlockSpec((None, seq, LANES), lambda b, i: (b, 0, 0))
    return pl.pallas_call(
        functools.partial(_dsa_kernel, topk=topk),
        out_shape=jax.ShapeDtypeStruct((batch, seq, W_HEADS), BF16),
        grid=(batch, nq),
        in_specs=[rows, rows,
                  pl.BlockSpec((None, N_HEADS, TQ), lambda b, i: (b, 0, i)),
                  allk, pl.BlockSpec((None, VT_ROWS, seq), lambda b, i: (b, 0, 0)), allk],
        out_specs=pl.BlockSpec((None, TQ, W_HEADS), lambda b, i: (b, i, 0)),
        scratch_shapes=[pltpu.VMEM((seq, TQ), F32),
                        pltpu.VMEM((seq, TQ), BF16),
                        pltpu.VMEM((N_HEADS, 1, TQ), F32),
                        pltpu.VMEM((N_PAIRS, VT_ROWS, 2 * TQ), F32),
                        pltpu.VMEM((N_HEADS, CH_H, TQ), F32),
                        pltpu.VMEM((N_PAIRS, LANES, 2 * TQ), BF16),
                        pltpu.VMEM((N_PAIRS, LANES, 2 * TQ), BF16)],
        compiler_params=_cparams(2),
        name="dsa",
    )(qd_rows, qi_rows, wit, kk, vt, kki)


def _mix_kernel(h_ref, n_ref, ysb_ref, yd_ref, mod_ref, wps_ref, wpd_ref, wg_ref, bg_ref,
                wo_ref, o_ref):
    d = h_ref.shape[1]
    y_sb = _dot(ysb_ref[...], wps_ref[...])
    y_d = _dot(yd_ref[...], wpd_ref[...])
    gates = jax.nn.sigmoid(_dot(n_ref[...], wg_ref[...]) + bg_ref[...])
    merged = (gates[:, :d] * y_sb + gates[:, d:] * y_d).astype(BF16)
    o_ref[...] = h_ref[...] + mod_ref[5] * _dot(merged, wo_ref[...])


def _mix_call(h, n2, y_sb, y_d, mod, w_proj_sb, w_proj_dsa, w_gate, b_gate, w_out, *, seq, tm=512):
    n_tok, d = h.shape
    tok = lambda w: pl.BlockSpec((tm, w), lambda i: (i, 0))
    ws = [w.astype(BF16) for w in (w_proj_sb, w_proj_dsa, w_gate)]
    bg = b_gate.reshape(1, -1)
    wo = w_out.astype(BF16)
    return pl.pallas_call(
        _mix_kernel,
        out_shape=jax.ShapeDtypeStruct((n_tok, d), F32),
        grid=(n_tok // tm,),
        in_specs=[tok(d), tok(d), tok(W_HEADS), tok(W_HEADS), _mod_spec(seq // tm, d)]
                 + [_const_spec(w.shape) for w in (*ws, bg, wo)],
        out_specs=tok(d),
        compiler_params=_cparams(1),
        name="mix",
    )(h, n2, y_sb, y_d, mod, *ws, bg, wo)


def kernel(x, c, w_ada, b_ada, g_ffn1, w1_ffn1, w3_ffn1, w2_ffn1, g_mix, w_in, g_q_dsa, g_k_dsa,
           w_proj_sb, w_proj_dsa, w_gate, b_gate, w_out, g_ffn2, w1_ffn2, w3_ffn2, w2_ffn2):
    batch, seq, d = x.shape
    h = x.reshape(batch * seq, d)
    for l in range(w_ada.shape[0]):
        bf = lambda w: w[l].astype(BF16)
        mod = _mod_call(c, w_ada[l], b_ada[l])
        h, n2 = _ffn_call(h, mod, g_ffn1[l], bf(w1_ffn1), bf(w3_ffn1), bf(w2_ffn1), g_mix[l],
                          mod_base=0, emit_next=True, seq=seq)
        q_sb, k_sb, v_sb, q_d, kk, q_i, kki, vt, wit = _inproj_call(
            n2, w_in[l], g_q_dsa[l], g_k_dsa[l], batch, seq)
        y_sb = _sb_call(q_sb, k_sb, v_sb, batch, seq).reshape(batch * seq, W_HEADS)
        y_d = _dsa_call(q_d, q_i, wit, kk, vt, kki, batch, seq).reshape(batch * seq, W_HEADS)
        h = _mix_call(h, n2, y_sb, y_d, mod, w_proj_sb[l], w_proj_dsa[l], w_gate[l], b_gate[l],
                      w_out[l], seq=seq)
        (h,) = _ffn_call(h, mod, g_ffn2[l], bf(w1_ffn2), bf(w3_ffn2), bf(w2_ffn2), g_ffn2[l],
                         mod_base=6, emit_next=False, seq=seq)
    return h.reshape(batch, seq, d)
```

```python
import functools

import numpy as np
import jax
import jax.numpy as jnp
from jax import lax
from jax.experimental import pallas as pl
from jax.experimental.pallas import tpu as pltpu

F32 = jnp.float32
BF16 = jnp.bfloat16

HEAD_DIM = 64
N_HEADS = 8
W_HEADS = N_HEADS * HEAD_DIM
N_PAIRS = N_HEADS // 2
LANES = 128
SUBLANES = 8
BF16_SUBLANES = 16
BF16_MAX_EXACT_COUNT = 256
SELECT_WINDOW_BITS = 17
SELECT_EARLY_PASSES = 11
TOPK_MAX = 256
N_MOD = 9
RMS_EPS = 1e-6
D_IDX = 64

TQ_SB = 256
TQ_DSA = 512
TK = 128
SB_PAIR_GROUPS = ((0, 1, 2, 3),)
CH_S = 512
CH_H = 256
VT_ROWS = 80
LANE_POS_HI, LANE_POS_LO = HEAD_DIM, HEAD_DIM + 1
N_POS_LANES = 2
N_SLOPE_PARTS = 3
LOG2E = float(np.log2(np.e))
NEG_BIG = -1e30
F32_EXP_UNDERFLOW = 104.0
INT_MIN = np.int32(-2**31)
VMEM_LIMIT = 56 * 1024 * 1024


def _cparams(n_axes):
    return pltpu.CompilerParams(
        dimension_semantics=("arbitrary",) * n_axes, vmem_limit_bytes=VMEM_LIMIT)


def _const_spec(shape):
    nd = len(shape)
    return pl.BlockSpec(shape, lambda *_: (0,) * nd, pipeline_mode=pl.Buffered(1))


def _dot(a, b):
    return jnp.dot(a, b, preferred_element_type=F32)


def _dot_nt(a, b):
    return lax.dot_general(a, b, (((1,), (1,)), ((), ())), preferred_element_type=F32)


def _mod_kernel(c_ref, w_ref, b_ref, o_ref, sc_ref):
    nb, kdim, _ = c_ref.shape
    tn = w_ref.shape[1]
    nch = tn // LANES

    @pl.when(pl.program_id(0) == 0)
    def _():
        c = c_ref[...]
        sc_ref[...] = c * jax.nn.sigmoid(c)

    def body(kb, accs):
        k0 = pl.multiple_of(kb * SUBLANES, SUBLANES)
        wblk = w_ref[pl.ds(k0, SUBLANES), :]
        out = []
        for b in range(nb):
            sb = sc_ref[b, pl.ds(k0, SUBLANES), :]
            for ch in range(nch):
                out.append(accs[b * nch + ch] + wblk[:, ch * LANES:(ch + 1) * LANES] * sb)
        return tuple(out)

    accs = lax.fori_loop(0, kdim // SUBLANES, body,
                         tuple(jnp.zeros((SUBLANES, LANES), F32) for _ in range(nb * nch)))
    for b in range(nb):
        row = jnp.concatenate(
            [jnp.sum(accs[b * nch + ch], axis=0, keepdims=True) for ch in range(nch)], axis=1)
        o_ref[0, b] = row + b_ref[...]


def _mod_call(c, w_ada, b_ada):
    nb, d = c.shape
    n_out = w_ada.shape[1]
    c_rep = jnp.broadcast_to(c[:, :, None], (nb, d, LANES))
    return pl.pallas_call(
        _mod_kernel,
        out_shape=jax.ShapeDtypeStruct((n_out // d, nb, 1, d), F32),
        grid=(n_out // d,),
        in_specs=[_const_spec((nb, d, LANES)),
                  pl.BlockSpec((d, d), lambda j: (0, j)),
                  pl.BlockSpec((1, d), lambda j: (0, j))],
        out_specs=pl.BlockSpec((1, nb, 1, d), lambda j: (j, 0, 0, 0)),
        scratch_shapes=[pltpu.VMEM((nb, d, LANES), F32)],
        compiler_params=_cparams(1),
        name="mod",
    )(c_rep, w_ada, b_ada.reshape(1, n_out))


def _norm_mod(h, g, shift, scale):
    y = h * lax.rsqrt(jnp.mean(h * h, axis=-1, keepdims=True) + RMS_EPS)
    return (y * g) * (1.0 + scale) + shift


def _ffn_kernel(h_ref, mod_ref, g_ref, w1_ref, w3_ref, w2_ref, gn_ref, *out_refs,
                mod_base, emit_next, n_chunks):
    o_ref = out_refs[0]
    h = h_ref[...]
    n = _norm_mod(h, g_ref[...], mod_ref[mod_base], mod_ref[mod_base + 1]).astype(BF16)
    fc = w1_ref.shape[1] // n_chunks
    acc = jnp.zeros(h.shape, F32)
    for c in range(n_chunks):
        a = _dot(n, w1_ref[:, c * fc:(c + 1) * fc])
        b = _dot(n, w3_ref[:, c * fc:(c + 1) * fc])
        act = (a * jax.nn.sigmoid(a) * b).astype(BF16)
        acc = acc + _dot(act, w2_ref[c * fc:(c + 1) * fc, :])
    hn = h + (0.5 * mod_ref[mod_base + 2]) * acc
    o_ref[...] = hn
    if emit_next:
        out_refs[1][...] = _norm_mod(hn, gn_ref[...], mod_ref[mod_base + 3],
                                     mod_ref[mod_base + 4]).astype(BF16)


def _mod_spec(tiles_per_batch, d):
    return pl.BlockSpec((N_MOD, None, 1, d), lambda i: (0, i // tiles_per_batch, 0, 0))


def _ffn_call(h, mod, g, w1, w3, w2, g_next, *, mod_base, emit_next, seq, tm=512):
    n_tok, d = h.shape
    f = w1.shape[1]
    tok = pl.BlockSpec((tm, d), lambda i: (i, 0))
    out_shape = [jax.ShapeDtypeStruct((n_tok, d), F32)]
    out_specs = [tok]
    if emit_next:
        out_shape.append(jax.ShapeDtypeStruct((n_tok, d), BF16))
        out_specs.append(tok)
    return pl.pallas_call(
        functools.partial(_ffn_kernel, mod_base=mod_base, emit_next=emit_next, n_chunks=2),
        out_shape=out_shape,
        grid=(n_tok // tm,),
        in_specs=[tok, _mod_spec(seq // tm, d), _const_spec((1, d)),
                  _const_spec((d, f)), _const_spec((d, f)), _const_spec((f, d)),
                  _const_spec((1, d))],
        out_specs=out_specs,
        compiler_params=_cparams(1),
        name="ffn",
    )(h, mod, g.reshape(1, d), w1, w3, w2, g_next.reshape(1, d))


def _alibi_slope_parts():
    out = []
    for h in range(N_HEADS):
        rest, parts = np.float32(2.0 ** (-8.0 * (h + 1) / N_HEADS) * LOG2E), []
        for _ in range(N_SLOPE_PARTS):
            parts.append(np.float32(np.asarray(rest, dtype=BF16)))
            rest = np.float32(rest - parts[-1])
        out.append(parts)
    return out


def _store_head_rows(o_ref, x, slope_parts):
    tm = x.shape[0]
    lane = lax.broadcasted_iota(jnp.int32, (tm, LANES), 1)
    for h in range(N_HEADS):
        rows = x[:, (h // 2) * LANES:(h // 2 + 1) * LANES]
        if h % 2:
            rows = pltpu.roll(rows, HEAD_DIM, axis=1)
        rows = jnp.where(lane < HEAD_DIM, rows, 0.0)
        if slope_parts is not None:
            for rep, part in enumerate(slope_parts[h]):
                rows = jnp.where(lane == LANE_POS_HI + rep * N_POS_LANES, part * HEAD_DIM, rows)
                rows = jnp.where(lane == LANE_POS_LO + rep * N_POS_LANES, part, rows)
        o_ref[:, h] = rows.astype(BF16).reshape(tm // TQ_DSA, TQ_DSA, LANES)


def _inproj_kernel(n_ref, wsb_ref, wqd_ref, wkd_ref, wqi_ref, wki_ref, wvt_ref, wwt_ref,
                   gq_ref, gk_ref, hm_ref,
                   qsb_ref, ksb_ref, vsb_ref, qd_ref, kk_ref, qi_ref, kki_ref, vt_ref, wit_ref,
                   *, tiles_per_batch):
    n = n_ref[...]
    tm = n.shape[0]
    qk_scale = HEAD_DIM ** -0.5

    qd = _dot(n, wqd_ref[...])
    ms = _dot((qd * qd).astype(BF16), hm_ref[...])
    _store_head_rows(qd_ref, qd * lax.rsqrt(ms + RMS_EPS) * (gq_ref[...] * (qk_scale * LOG2E)),
                     _alibi_slope_parts())

    kd = _dot(n, wkd_ref[...])
    kn = kd * lax.rsqrt(jnp.mean(kd * kd, axis=-1, keepdims=True) + RMS_EPS) * gk_ref[...]
    lane = lax.broadcasted_iota(jnp.int32, (tm, LANES), 1)
    pos = ((pl.program_id(0) % tiles_per_batch) * tm
           + lax.broadcasted_iota(jnp.int32, (tm, LANES), 0))
    feat = jnp.zeros((tm, LANES), jnp.int32)
    for rep in range(N_SLOPE_PARTS):
        feat = jnp.where(lane == LANE_POS_HI + rep * N_POS_LANES,
                         pos >> (HEAD_DIM.bit_length() - 1), feat)
        feat = jnp.where(lane == LANE_POS_LO + rep * N_POS_LANES, pos & (HEAD_DIM - 1), feat)
    feat = feat.astype(F32)
    kk_ref[...] = jnp.where(lane < HEAD_DIM, kn, feat).astype(BF16)

    _store_head_rows(qi_ref, _dot(n, wqi_ref[...]), None)
    kki_ref[...] = _dot(n, wki_ref[...]).astype(BF16)
    vt = _dot_nt(wvt_ref[...], n)
    row = lax.broadcasted_iota(jnp.int32, vt.shape, 0)
    vt_ref[...] = jnp.where(row == HEAD_DIM, 1.0, vt).astype(BF16)
    wit_ref[...] = _dot_nt(wwt_ref[...], n)[:N_HEADS]

    sb = _dot(n, wsb_ref[...])
    qsb_ref[...] = (sb[:, :W_HEADS] * qk_scale).astype(BF16)
    ksb_ref[...] = sb[:, W_HEADS:2 * W_HEADS].astype(BF16)
    vsb_ref[...] = sb[:, 2 * W_HEADS:].astype(BF16)


def _inproj_call(n2, w_in, g_q, g_k, batch, seq, tm=1024):
    n_tok, d = n2.shape
    sizes = (W_HEADS, W_HEADS, W_HEADS, W_HEADS, HEAD_DIM, HEAD_DIM, N_HEADS * D_IDX, D_IDX, N_HEADS)
    offs = np.concatenate([[0], np.cumsum(sizes)])
    col = lambda k: w_in[:, offs[k]:offs[k + 1]]
    idx_scale = (D_IDX ** -0.5) * (N_HEADS ** -0.5)
    w_sb = jnp.concatenate([col(0), col(1), col(2)], axis=1).astype(BF16)
    w_qd = col(3).astype(BF16)
    w_kd = jnp.concatenate([col(4), col(4)], axis=1).astype(BF16)
    w_qi = col(6).astype(BF16)
    w_ki = jnp.pad(col(7), ((0, 0), (0, LANES - D_IDX))).astype(BF16)
    w_vt = jnp.pad(col(5).T, ((0, VT_ROWS - HEAD_DIM), (0, 0))).astype(BF16)
    w_wt = jnp.pad((col(8) * idx_scale).T, ((0, 16 - N_HEADS), (0, 0))).astype(BF16)
    gq = jnp.tile(g_q, N_HEADS).reshape(1, W_HEADS)
    gk = jnp.tile(g_k, 2).reshape(1, LANES)
    head_of = np.arange(W_HEADS) // HEAD_DIM
    head_mean = jnp.asarray((head_of[:, None] == head_of[None, :]) / HEAD_DIM, BF16)

    tpb = seq // tm
    tok = lambda w: pl.BlockSpec((tm, w), lambda i: (i, 0))
    tr = lambda r: pl.BlockSpec((None, r, tm), lambda i: (i // tpb, 0, i % tpb))
    sd = lambda w, dt: jax.ShapeDtypeStruct((n_tok, w), dt)
    consts = (w_sb, w_qd, w_kd, w_qi, w_ki, w_vt, w_wt, gq, gk, head_mean)
    head_rows = jax.ShapeDtypeStruct((n_tok // TQ_DSA, N_HEADS, TQ_DSA, LANES), BF16)
    head_rows_spec = pl.BlockSpec((tm // TQ_DSA, N_HEADS, TQ_DSA, LANES), lambda i: (i, 0, 0, 0))
    return pl.pallas_call(
        functools.partial(_inproj_kernel, tiles_per_batch=tpb),
        out_shape=[sd(W_HEADS, BF16)] * 3 + [head_rows, sd(LANES, BF16), head_rows, sd(LANES, BF16),
                                             jax.ShapeDtypeStruct((batch, VT_ROWS, seq), BF16),
                                             jax.ShapeDtypeStruct((batch, N_HEADS, seq), F32)],
        grid=(n_tok // tm,),
        in_specs=[tok(d)] + [_const_spec(w.shape) for w in consts],
        out_specs=[tok(W_HEADS)] * 3 + [head_rows_spec, tok(LANES), head_rows_spec, tok(LANES),
                                        tr(VT_ROWS), tr(N_HEADS)],
        compiler_params=_cparams(1),
        name="inproj",
    )(n2, *consts)


def _split_heads(x, lane):
    xf = x.astype(F32)
    lo = jnp.where(lane < HEAD_DIM, xf, 0.0).astype(x.dtype)
    hi = jnp.where(lane >= HEAD_DIM, xf, 0.0).astype(x.dtype)
    return lo, hi


def _sb_kernel(q_ref, k_ref, v_ref, t_ref, o_ref, qs_ref, run_ref, acc_ref):
    tq = q_ref.shape[0]
    n_diag = tq // TK
    i = pl.program_id(1)
    lane = lax.broadcasted_iota(jnp.int32, (tq, LANES), 1)
    lane_k = lax.broadcasted_iota(jnp.int32, (TK, LANES), 1)
    for p in range(N_PAIRS):
        lo, hi = _split_heads(q_ref[:, p * LANES:(p + 1) * LANES], lane)
        qs_ref[p, :tq, :] = lo
        qs_ref[p, tq:, :] = hi

    def block(k0, key_offset, first):
        masked = key_offset is not None
        strict = (lax.broadcasted_iota(jnp.int32, (2 * tq, TK), 1) + key_offset
                  < lax.broadcasted_iota(jnp.int32, (2 * tq, TK), 0) % tq) if masked else None

        def group(pairs):
            zs = {p: _dot_nt(qs_ref[p], k_ref[pl.ds(k0, TK), p * LANES:(p + 1) * LANES])
                  for p in pairs}
            lbs, ws = {}, {}
            for p in pairs:
                z = zs[p]
                sp = jnp.maximum(z, 0.0) + jnp.log(1.0 + jnp.exp(-jnp.abs(z)))
                l1m = jnp.where(strict, sp, 0.0) if masked else sp
                l_hi = l1m.astype(BF16)
                l_lo = (l1m - l_hi.astype(F32)).astype(BF16)
                ws[p] = _dot(jnp.concatenate([l_hi, l_lo], axis=1), t_ref[...])
                lbs[p] = z - sp
            for p in pairs:
                arg = lbs[p] + ws[p][:, :TK]
                if not first:
                    arg = arg + run_ref[p]
                a = jnp.exp(arg)
                if masked:
                    a = jnp.where(strict, a, 0.0)
                a = a.astype(BF16)
                v_lo, v_hi = _split_heads(v_ref[pl.ds(k0, TK), p * LANES:(p + 1) * LANES], lane_k)
                upd = _dot(jnp.concatenate([a[:tq], a[tq:]], axis=1),
                           jnp.concatenate([v_lo, v_hi], axis=0))
                if first:
                    acc_ref[p] = upd
                    run_ref[p] = ws[p][:, TK:]
                else:
                    acc_ref[p] += upd
                    run_ref[p] += ws[p][:, TK:]

        for pairs in SB_PAIR_GROUPS:
            group(pairs)

    last = (i + 1) * n_diag - 1
    for jj in range(n_diag):
        block(pl.multiple_of((last - jj) * TK, TK), (n_diag - 1 - jj) * TK, jj == 0)

    def more_blocks(jj):
        run_max = jnp.max(functools.reduce(jnp.maximum, [run_ref[p] for p in range(N_PAIRS)]))
        return jnp.logical_and(jj <= last, run_max > -F32_EXP_UNDERFLOW)

    def body(state):
        jj, _ = state
        block(pl.multiple_of((last - jj) * TK, TK), None, False)
        return jj + 1, more_blocks(jj + 1)

    lax.while_loop(lambda state: state[1], body, (jnp.int32(n_diag), i >= 1))
    for p in range(N_PAIRS):
        o_ref[:, p * LANES:(p + 1) * LANES] = acc_ref[p].astype(o_ref.dtype)


def _sb_call(q, k, v, batch, seq):
    q, k, v = (t.reshape(batch, seq, W_HEADS) for t in (q, k, v))
    j_idx = np.arange(TK)
    suffix = (j_idx[:, None] > j_idx[None, :]).astype(np.float32)
    tmat = -np.concatenate([suffix, np.ones((TK, LANES), np.float32)], axis=1)
    tmat = jnp.asarray(np.concatenate([tmat, tmat], axis=0), BF16)
    blk = pl.BlockSpec((None, TQ_SB, W_HEADS), lambda b, i: (b, i, 0))
    allk = pl.BlockSpec((None, seq, W_HEADS), lambda b, i: (b, 0, 0))
    return pl.pallas_call(
        _sb_kernel,
        out_shape=jax.ShapeDtypeStruct((batch, seq, W_HEADS), BF16),
        grid=(batch, seq // TQ_SB),
        in_specs=[blk, allk, allk, _const_spec(tmat.shape)],
        out_specs=blk,
        scratch_shapes=[pltpu.VMEM((N_PAIRS, 2 * TQ_SB, LANES), BF16),
                        pltpu.VMEM((N_PAIRS, 2 * TQ_SB, LANES), F32),
                        pltpu.VMEM((N_PAIRS, TQ_SB, LANES), F32)],
        compiler_params=_cparams(2),
        name="sb",
    )(q, k, v, tmat)


def _dsa_kernel(qd_st, qi_st, wit_ref, kk_ref, vt_ref, kki_ref, o_ref,
                sc_ref, top_ref, m_ref, acc_ref, buf_ref, qit_ref, qdt_ref, *, topk):
    TQ = qd_st.shape[1]
    i = pl.program_id(1)
    t0 = i * TQ

    n_s = (i + CH_S // TQ) // (CH_S // TQ)

    for q_st, qt_ref in ((qi_st, qit_ref), (qd_st, qdt_ref)):
        for p in range(N_PAIRS):
            qt_ref[p] = jnp.concatenate(
                [q_st[h].astype(F32).T for h in (2 * p, 2 * p + 1)], axis=1).astype(BF16)

    def products(qt_ref, k_ref):
        def produce(k0):
            kch = k_ref[pl.ds(k0, CH_H), :]
            return [_dot(kch, qt_ref[p]) for p in range(N_PAIRS)]
        return produce

    def store_products(vals):
        for p in range(N_PAIRS):
            buf_ref[2 * p] = vals[p][:, :TQ]
            buf_ref[2 * p + 1] = vals[p][:, TQ:]

    def sweep(produce, consume):
        from_buf = lambda h: buf_ref[h]

        def trip(c, prefetch):
            k_a = pl.multiple_of(c * CH_S, CH_S)
            k_b = pl.multiple_of(k_a + CH_H, CH_H)
            vals_b = produce(k_b)
            consume(from_buf, k_a)
            if prefetch:
                vals_next = produce(pl.multiple_of(k_a + CH_S, CH_S))
            consume(lambda h: vals_b[h // 2][:, (h % 2) * TQ:(h % 2 + 1) * TQ], k_b)
            if prefetch:
                store_products(vals_next)

        store_products(produce(0))

        def body(c, _):
            trip(c, True)
            return 0

        lax.fori_loop(0, n_s - 1, body, 0)
        trip(n_s - 1, False)

    wit = wit_ref[...]

    def score_consume(prod, k0):
        score = None
        for h in range(N_HEADS):
            term = jnp.maximum(prod(h), 0.0) * wit[h:h + 1, :]
            score = term if score is None else score + term
        causal = (k0 + lax.broadcasted_iota(jnp.int32, (CH_H, TQ), 0)
                  <= t0 + lax.broadcasted_iota(jnp.int32, (CH_H, TQ), 1))
        score = jnp.where(causal, score, -jnp.inf)
        sc_ref[pl.ds(k0, CH_H), :] = score
        top_ref[pl.ds(k0, CH_H), :] = score.astype(BF16)

    sweep(products(qit_ref, kki_ref), score_consume)

    def float_of_rank(u):
        key = u ^ INT_MIN
        return pltpu.bitcast(jnp.where(key < 0, key ^ np.int32(0x7FFFFFFF), key), F32)

    def count_top(pred):
        one, zero = jnp.ones((BF16_SUBLANES, TQ), BF16), jnp.zeros((BF16_SUBLANES, TQ), BF16)

        def cbody(c, acc):
            k0 = pl.multiple_of(c * CH_S, CH_S)
            tops = top_ref[pl.ds(k0, CH_S), :]
            sums = [acc, zero]
            for j in range(CH_S // BF16_SUBLANES):
                hit = jnp.where(pred(tops[j * BF16_SUBLANES:(j + 1) * BF16_SUBLANES]), one, zero)
                sums[j % 2] = sums[j % 2] + hit
            return sums[0] + sums[1]

        acc = lax.fori_loop(0, n_s, cbody, zero)
        return jnp.sum(acc.astype(F32), axis=0, keepdims=True)

    def top_body(b, u16):
        cand = u16 | jnp.left_shift(jnp.int32(1), 15 - b)
        cand_f = float_of_rank(jnp.left_shift(cand, 16)).astype(BF16)
        cand_t = jnp.broadcast_to(cand_f, (BF16_SUBLANES, TQ))
        return jnp.where(count_top(lambda t: t >= cand_t) >= topk, cand, u16)

    u16 = lax.fori_loop(0, 16, top_body, jnp.zeros((1, TQ), jnp.int32))

    zero_t = jnp.zeros((BF16_SUBLANES, TQ), BF16)
    n_nonneg = jnp.broadcast_to(count_top(lambda t: t >= zero_t), (SUBLANES, TQ))
    n_pos = jnp.broadcast_to(count_top(lambda t: t > zero_t), (SUBLANES, TQ))

    def count(pred):
        def cbody(c, acc):
            k0 = pl.multiple_of(c * CH_S, CH_S)
            sch = sc_ref[pl.ds(k0, CH_S), :].reshape(CH_S // SUBLANES, SUBLANES, TQ)
            return acc + jnp.sum(jnp.where(pred(sch), 1, 0), axis=0)
        acc = lax.fori_loop(0, n_s, cbody, jnp.zeros((SUBLANES, TQ), jnp.int32))
        tot = jnp.sum(acc.astype(F32), axis=0, keepdims=True)
        return jnp.broadcast_to(tot, (SUBLANES, TQ))

    base = jnp.broadcast_to(jnp.left_shift(u16, 16) - (1 << 15), (SUBLANES, TQ))

    def window_body(b, state):
        off, n_off = state
        cand = off | jnp.left_shift(jnp.int32(1), SELECT_WINDOW_BITS - 1 - b)
        cand_f = float_of_rank(base + cand)
        n_cand = count(lambda sch: sch >= cand_f[None])
        take = n_cand >= topk
        return jnp.where(take, cand, off), jnp.where(take, n_cand, n_off)

    few_keys = t0 + lax.broadcasted_iota(jnp.int32, (SUBLANES, TQ), 1) < topk
    unknown = jnp.full((SUBLANES, TQ), -1.0, F32)
    state = lax.fori_loop(0, SELECT_EARLY_PASSES, window_body,
                          (jnp.zeros((SUBLANES, TQ), jnp.int32), unknown))
    zero_kth = jnp.logical_and(jnp.logical_and(n_pos < topk, n_nonneg >= topk),
                               jnp.logical_not(few_keys))
    exempt = jnp.logical_or(few_keys, zero_kth)

    def any_open(n_at_off):
        settled = jnp.logical_or(n_at_off == topk, exempt)
        return jnp.max(jnp.where(settled, 0.0, 1.0)) > 0.0

    def late_pass(carry):
        b, st, _ = carry
        st = window_body(b, st)
        return b + 1, st, jnp.logical_and(b + 1 < SELECT_WINDOW_BITS, any_open(st[1]))

    _, (offset, n_off), _ = lax.while_loop(
        lambda carry: carry[2], late_pass,
        (jnp.int32(SELECT_EARLY_PASSES), state, any_open(state[1])))
    thr = jnp.where(few_keys, jnp.finfo(F32).min,
                    jnp.where(zero_kth, 0.0, float_of_rank(base + offset)))
    thr_row = thr[0:1, :]

    n_ge = jnp.where(few_keys, 0.0,
                     jnp.where(zero_kth, n_nonneg, jnp.where(n_off < 0.0, topk + 1.0, n_off)))

    @pl.when(jnp.max(n_ge) > topk)
    def _():
        need = topk - count(lambda sch: sch > thr[None])[0:1, :]
        before = jnp.where(lax.broadcasted_iota(jnp.int32, (CH_H, CH_H), 1)
                           < lax.broadcasted_iota(jnp.int32, (CH_H, CH_H), 0), 1.0, 0.0).astype(BF16)

        def tie_body(c, seen):
            k0s = [pl.multiple_of(c * CH_S + j * CH_H, CH_H) for j in range(CH_S // CH_H)]
            schs = [sc_ref[pl.ds(k0, CH_H), :] for k0 in k0s]
            eqs = [jnp.where(sch == thr_row, 1.0, 0.0) for sch in schs]
            ranks = [_dot(before, eq.astype(BF16)) for eq in eqs]
            for k0, sch, eq, rank in zip(k0s, schs, eqs, ranks):
                tie_rank = jnp.where(sch == thr_row, rank + seen, -1.0)
                sc_ref[pl.ds(k0, CH_H), :] = jnp.where(tie_rank >= need, -jnp.inf, sch)
                seen = seen + jnp.sum(eq, axis=0, keepdims=True)
            return seen

        lax.fori_loop(0, n_s, tie_body, jnp.zeros((1, TQ), F32))

    m_ref[...] = jnp.full(m_ref.shape, NEG_BIG, F32)
    acc_ref[...] = jnp.zeros(acc_ref.shape, F32)

    def attn_consume(prod, k0):
        vch = vt_ref[:, pl.ds(k0, CH_H)]
        sel = sc_ref[pl.ds(k0, CH_H), :] >= thr_row
        for p in range(N_PAIRS):
            probs, alphas = [], []
            for h in (2 * p, 2 * p + 1):
                lg = jnp.where(sel, prod(h), NEG_BIG)
                m_old = m_ref[h]
                m_new = jnp.maximum(m_old, jnp.max(lg, axis=0, keepdims=True))
                m_ref[h] = m_new
                alphas.append(jnp.exp2(m_old - m_new))
                probs.append(jnp.exp2(lg - m_new).astype(BF16))
            upd = _dot(vch, jnp.concatenate(probs, axis=1))
            acc_ref[p] = acc_ref[p] * jnp.concatenate(alphas, axis=1) + upd

    sweep(products(qdt_ref, kk_ref), attn_consume)
    for p in range(N_PAIRS):
        acc = acc_ref[p]
        out_t = acc[:HEAD_DIM] / acc[HEAD_DIM:HEAD_DIM + 1]
        pair = jnp.concatenate([out_t[:, :TQ], out_t[:, TQ:]], axis=0)
        o_ref[:, p * LANES:(p + 1) * LANES] = pair.T.astype(o_ref.dtype)


def _dsa_call(qd_rows, qi_rows, wit, kk, vt, kki, batch, seq):
    r3 = lambda t: t.reshape(batch, seq, t.shape[-1])
    kk, kki = map(r3, (kk, kki))
    topk = min(TOPK_MAX, seq // 4)
    assert seq // BF16_SUBLANES <= BF16_MAX_EXACT_COUNT
    TQ = TQ_DSA
    nq = seq // TQ
    rows = pl.BlockSpec((None, N_HEADS, TQ, LANES), lambda b, i: (b * nq + i, 0, 0, 0))
    allk = pl.BlockSpec((None, seq, LANES), lambda b, i: (b, 0, 0))
    return pl.pallas_call(
        functools.partial(_dsa_kernel, topk=topk),
        out_shape=jax.ShapeDtypeStruct((batch, seq, W_HEADS), BF16),
        grid=(batch, nq),
        in_specs=[rows, rows,
                  pl.BlockSpec((None, N_HEADS, TQ), lambda b, i: (b, 0, i)),
                  allk, pl.BlockSpec((None, VT_ROWS, seq), lambda b, i: (b, 0, 0)), allk],
        out_specs=pl.BlockSpec((None, TQ, W_HEADS), lambda b, i: (b, i, 0)),
        scratch_shapes=[pltpu.VMEM((seq, TQ), F32),
                        pltpu.VMEM((seq, TQ), BF16),
                        pltpu.VMEM((N_HEADS, 1, TQ), F32),
                        pltpu.VMEM((N_PAIRS, VT_ROWS, 2 * TQ), F32),
                        pltpu.VMEM((N_HEADS, CH_H, TQ), F32),
                        pltpu.VMEM((N_PAIRS, LANES, 2 * TQ), BF16),
                        pltpu.VMEM((N_PAIRS, LANES, 2 * TQ), BF16)],
        compiler_params=_cparams(2),
        name="dsa",
    )(qd_rows, qi_rows, wit, kk, vt, kki)


def _mix_kernel(h_ref, n_ref, ysb_ref, yd_ref, mod_ref, wps_ref, wpd_ref, wg_ref, bg_ref,
                wo_ref, o_ref):
    d = h_ref.shape[1]
    y_sb = _dot(ysb_ref[...], wps_ref[...])
    y_d = _dot(yd_ref[...], wpd_ref[...])
    gates = jax.nn.sigmoid(_dot(n_ref[...], wg_ref[...]) + bg_ref[...])
    merged = (gates[:, :d] * y_sb + gates[:, d:] * y_d).astype(BF16)
    o_ref[...] = h_ref[...] + mod_ref[5] * _dot(merged, wo_ref[...])


def _mix_call(h, n2, y_sb, y_d, mod, w_proj_sb, w_proj_dsa, w_gate, b_gate, w_out, *, seq, tm=512):
    n_tok, d = h.shape
    tok = lambda w: pl.BlockSpec((tm, w), lambda i: (i, 0))
    ws = [w.astype(BF16) for w in (w_proj_sb, w_proj_dsa, w_gate)]
    bg = b_gate.reshape(1, -1)
    wo = w_out.astype(BF16)
    return pl.pallas_call(
        _mix_kernel,
        out_shape=jax.ShapeDtypeStruct((n_tok, d), F32),
        grid=(n_tok // tm,),
        in_specs=[tok(d), tok(d), tok(W_HEADS), tok(W_HEADS), _mod_spec(seq // tm, d)]
                 + [_const_spec(w.shape) for w in (*ws, bg, wo)],
        out_specs=tok(d),
        compiler_params=_cparams(1),
        name="mix",
    )(h, n2, y_sb, y_d, mod, *ws, bg, wo)


def kernel(x, c, w_ada, b_ada, g_ffn1, w1_ffn1, w3_ffn1, w2_ffn1, g_mix, w_in, g_q_dsa, g_k_dsa,
           w_proj_sb, w_proj_dsa, w_gate, b_gate, w_out, g_ffn2, w1_ffn2, w3_ffn2, w2_ffn2):
    batch, seq, d = x.shape
    h = x.reshape(batch * seq, d)
    for l in range(w_ada.shape[0]):
        bf = lambda w: w[l].astype(BF16)
        mod = _mod_call(c, w_ada[l], b_ada[l])
        h, n2 = _ffn_call(h, mod, g_ffn1[l], bf(w1_ffn1), bf(w3_ffn1), bf(w2_ffn1), g_mix[l],
                          mod_base=0, emit_next=True, seq=seq)
        q_sb, k_sb, v_sb, q_d, kk, q_i, kki, vt, wit = _inproj_call(
            n2, w_in[l], g_q_dsa[l], g_k_dsa[l], batch, seq)
        y_sb = _sb_call(q_sb, k_sb, v_sb, batch, seq).reshape(batch * seq, W_HEADS)
        y_d = _dsa_call(q_d, q_i, wit, kk, vt, kki, batch, seq).reshape(batch * seq, W_HEADS)
        h = _mix_call(h, n2, y_sb, y_d, mod, w_proj_sb[l], w_proj_dsa[l], w_gate[l], b_gate[l],
                      w_out[l], seq=seq)
        (h,) = _ffn_call(h, mod, g_ffn2[l], bf(w1_ffn2), bf(w3_ffn2), bf(w2_ffn2), g_ffn2[l],
                         mod_base=6, emit_next=False, seq=seq)
    return h.reshape(batch, seq, d)
```

```python
import functools

import numpy as np
import jax
import jax.numpy as jnp
from jax import lax
from jax.experimental import pallas as pl
from jax.experimental.pallas import tpu as pltpu

F32 = jnp.float32
BF16 = jnp.bfloat16

HEAD_DIM = 64
N_HEADS = 8
W_HEADS = N_HEADS * HEAD_DIM
N_PAIRS = N_HEADS // 2
LANES = 128
SUBLANES = 8
BF16_SUBLANES = 16
BF16_MAX_EXACT_COUNT = 256
SELECT_WINDOW_BITS = 17
SELECT_EARLY_PASSES = 11
TOPK_MAX = 256
N_MOD = 9
RMS_EPS = 1e-6
D_IDX = 64

TQ_SB = 256
TQ_DSA = 512
TK = 128
SB_PAIR_GROUPS = ((0, 1, 2, 3),)
CH_S = 512
CH_H = 256
VT_ROWS = 80
LANE_POS_HI, LANE_POS_LO = HEAD_DIM, HEAD_DIM + 1
N_POS_LANES = 2
N_SLOPE_PARTS = 3
LOG2E = float(np.log2(np.e))
NEG_BIG = -1e30
F32_EXP_UNDERFLOW = 104.0
INT_MIN = np.int32(-2**31)
VMEM_LIMIT = 56 * 1024 * 1024


def _cparams(n_axes):
    return pltpu.CompilerParams(
        dimension_semantics=("arbitrary",) * n_axes, vmem_limit_bytes=VMEM_LIMIT)


def _const_spec(shape):
    nd = len(shape)
    return pl.BlockSpec(shape, lambda *_: (0,) * nd, pipeline_mode=pl.Buffered(1))


def _dot(a, b):
    return jnp.dot(a, b, preferred_element_type=F32)


def _dot_nt(a, b):
    return lax.dot_general(a, b, (((1,), (1,)), ((), ())), preferred_element_type=F32)


def _mod_kernel(c_ref, w_ref, b_ref, o_ref, sc_ref):
    nb, kdim, _ = c_ref.shape
    tn = w_ref.shape[1]
    nch = tn // LANES

    @pl.when(pl.program_id(0) == 0)
    def _():
        c = c_ref[...]
        sc_ref[...] = c * jax.nn.sigmoid(c)

    def body(kb, accs):
        k0 = pl.multiple_of(kb * SUBLANES, SUBLANES)
        wblk = w_ref[pl.ds(k0, SUBLANES), :]
        out = []
        for b in range(nb):
            sb = sc_ref[b, pl.ds(k0, SUBLANES), :]
            for ch in range(nch):
                out.append(accs[b * nch + ch] + wblk[:, ch * LANES:(ch + 1) * LANES] * sb)
        return tuple(out)

    accs = lax.fori_loop(0, kdim // SUBLANES, body,
                         tuple(jnp.zeros((SUBLANES, LANES), F32) for _ in range(nb * nch)))
    for b in range(nb):
        row = jnp.concatenate(
            [jnp.sum(accs[b * nch + ch], axis=0, keepdims=True) for ch in range(nch)], axis=1)
        o_ref[0, b] = row + b_ref[...]


def _mod_call(c, w_ada, b_ada):
    nb, d = c.shape
    n_out = w_ada.shape[1]
    c_rep = jnp.broadcast_to(c[:, :, None], (nb, d, LANES))
    return pl.pallas_call(
        _mod_kernel,
        out_shape=jax.ShapeDtypeStruct((n_out // d, nb, 1, d), F32),
        grid=(n_out // d,),
        in_specs=[_const_spec((nb, d, LANES)),
                  pl.BlockSpec((d, d), lambda j: (0, j)),
                  pl.BlockSpec((1, d), lambda j: (0, j))],
        out_specs=pl.BlockSpec((1, nb, 1, d), lambda j: (j, 0, 0, 0)),
        scratch_shapes=[pltpu.VMEM((nb, d, LANES), F32)],
        compiler_params=_cparams(1),
        name="mod",
    )(c_rep, w_ada, b_ada.reshape(1, n_out))


def _norm_mod(h, g, shift, scale):
    y = h * lax.rsqrt(jnp.mean(h * h, axis=-1, keepdims=True) + RMS_EPS)
    return (y * g) * (1.0 + scale) + shift


def _ffn_kernel(h_ref, mod_ref, g_ref, w1_ref, w3_ref, w2_ref, gn_ref, *out_refs,
                mod_base, emit_next, n_chunks):
    o_ref = out_refs[0]
    h = h_ref[...]
    n = _norm_mod(h, g_ref[...], mod_ref[mod_base], mod_ref[mod_base + 1]).astype(BF16)
    fc = w1_ref.shape[1] // n_chunks
    acc = jnp.zeros(h.shape, F32)
    for c in range(n_chunks):
        a = _dot(n, w1_ref[:, c * fc:(c + 1) * fc])
        b = _dot(n, w3_ref[:, c * fc:(c + 1) * fc])
        act = (a * jax.nn.sigmoid(a) * b).astype(BF16)
        acc = acc + _dot(act, w2_ref[c * fc:(c + 1) * fc, :])
    hn = h + (0.5 * mod_ref[mod_base + 2]) * acc
    o_ref[...] = hn
    if emit_next:
        out_refs[1][...] = _norm_mod(hn, gn_ref[...], mod_ref[mod_base + 3],
                                     mod_ref[mod_base + 4]).astype(BF16)


def _mod_spec(tiles_per_batch, d):
    return pl.BlockSpec((N_MOD, None, 1, d), lambda i: (0, i // tiles_per_batch, 0, 0))


def _ffn_call(h, mod, g, w1, w3, w2, g_next, *, mod_base, emit_next, seq, tm=512):
    n_tok, d = h.shape
    f = w1.shape[1]
    tok = pl.BlockSpec((tm, d), lambda i: (i, 0))
    out_shape = [jax.ShapeDtypeStruct((n_tok, d), F32)]
    out_specs = [tok]
    if emit_next:
        out_shape.append(jax.ShapeDtypeStruct((n_tok, d), BF16))
        out_specs.append(tok)
    return pl.pallas_call(
        functools.partial(_ffn_kernel, mod_base=mod_base, emit_next=emit_next, n_chunks=2),
        out_shape=out_shape,
        grid=(n_tok // tm,),
        in_specs=[tok, _mod_spec(seq // tm, d), _const_spec((1, d)),
                  _const_spec((d, f)), _const_spec((d, f)), _const_spec((f, d)),
                  _const_spec((1, d))],
        out_specs=out_specs,
        compiler_params=_cparams(1),
        name="ffn",
    )(h, mod, g.reshape(1, d), w1, w3, w2, g_next.reshape(1, d))


def _alibi_slope_parts():
    out = []
    for h in range(N_HEADS):
        rest, parts = np.float32(2.0 ** (-8.0 * (h + 1) / N_HEADS) * LOG2E), []
        for _ in range(N_SLOPE_PARTS):
            parts.append(np.float32(np.asarray(rest, dtype=BF16)))
            rest = np.float32(rest - parts[-1])
        out.append(parts)
    return out


def _store_head_rows(o_ref, x, slope_parts):
    tm = x.shape[0]
    lane = lax.broadcasted_iota(jnp.int32, (tm, LANES), 1)
    for h in range(N_HEADS):
        rows = x[:, (h // 2) * LANES:(h // 2 + 1) * LANES]
        if h % 2:
            rows = pltpu.roll(rows, HEAD_DIM, axis=1)
        rows = jnp.where(lane < HEAD_DIM, rows, 0.0)
        if slope_parts is not None:
            for rep, part in enumerate(slope_parts[h]):
                rows = jnp.where(lane == LANE_POS_HI + rep * N_POS_LANES, part * HEAD_DIM, rows)
                rows = jnp.where(lane == LANE_POS_LO + rep * N_POS_LANES, part, rows)
        o_ref[:, h] = rows.astype(BF16).reshape(tm // TQ_DSA, TQ_DSA, LANES)


def _inproj_kernel(n_ref, wsb_ref, wqd_ref, wkd_ref, wqi_ref, wki_ref, wvt_ref, wwt_ref,
                   gq_ref, gk_ref, hm_ref,
                   qsb_ref, ksb_ref, vsb_ref, qd_ref, kk_ref, qi_ref, kki_ref, vt_ref, wit_ref,
                   *, tiles_per_batch):
    n = n_ref[...]
    tm = n.shape[0]
    qk_scale = HEAD_DIM ** -0.5

    qd = _dot(n, wqd_ref[...])
    ms = _dot((qd * qd).astype(BF16), hm_ref[...])
    _store_head_rows(qd_ref, qd * lax.rsqrt(ms + RMS_EPS) * (gq_ref[...] * (qk_scale * LOG2E)),
                     _alibi_slope_parts())

    kd = _dot(n, wkd_ref[...])
    kn = kd * lax.rsqrt(jnp.mean(kd * kd, axis=-1, keepdims=True) + RMS_EPS) * gk_ref[...]
    lane = lax.broadcasted_iota(jnp.int32, (tm, LANES), 1)
    pos = ((pl.program_id(0) % tiles_per_batch) * tm
           + lax.broadcasted_iota(jnp.int32, (tm, LANES), 0))
    feat = jnp.zeros((tm, LANES), jnp.int32)
    for rep in range(N_SLOPE_PARTS):
        feat = jnp.where(lane == LANE_POS_HI + rep * N_POS_LANES,
                         pos >> (HEAD_DIM.bit_length() - 1), feat)
        feat = jnp.where(lane == LANE_POS_LO + rep * N_POS_LANES, pos & (HEAD_DIM - 1), feat)
    feat = feat.astype(F32)
    kk_ref[...] = jnp.where(lane < HEAD_DIM, kn, feat).astype(BF16)

    _store_head_rows(qi_ref, _dot(n, wqi_ref[...]), None)
    kki_ref[...] = _dot(n, wki_ref[...]).astype(BF16)
    vt = _dot_nt(wvt_ref[...], n)
    row = lax.broadcasted_iota(jnp.int32, vt.shape, 0)
    vt_ref[...] = jnp.where(row == HEAD_DIM, 1.0, vt).astype(BF16)
    wit_ref[...] = _dot_nt(wwt_ref[...], n)[:N_HEADS]

    sb = _dot(n, wsb_ref[...])
    qsb_ref[...] = (sb[:, :W_HEADS] * qk_scale).astype(BF16)
    ksb_ref[...] = sb[:, W_HEADS:2 * W_HEADS].astype(BF16)
    vsb_ref[...] = sb[:, 2 * W_HEADS:].astype(BF16)


def _inproj_call(n2, w_in, g_q, g_k, batch, seq, tm=1024):
    n_tok, d = n2.shape
    sizes = (W_HEADS, W_HEADS, W_HEADS, W_HEADS, HEAD_DIM, HEAD_DIM, N_HEADS * D_IDX, D_IDX, N_HEADS)
    offs = np.concatenate([[0], np.cumsum(sizes)])
    col = lambda k: w_in[:, offs[k]:offs[k + 1]]
    idx_scale = (D_IDX ** -0.5) * (N_HEADS ** -0.5)
    w_sb = jnp.concatenate([col(0), col(1), col(2)], axis=1).astype(BF16)
    w_qd = col(3).astype(BF16)
    w_kd = jnp.concatenate([col(4), col(4)], axis=1).astype(BF16)
    w_qi = col(6).astype(BF16)
    w_ki = jnp.pad(col(7), ((0, 0), (0, LANES - D_IDX))).astype(BF16)
    w_vt = jnp.pad(col(5).T, ((0, VT_ROWS - HEAD_DIM), (0, 0))).astype(BF16)
    w_wt = jnp.pad((col(8) * idx_scale).T, ((0, 16 - N_HEADS), (0, 0))).astype(BF16)
    gq = jnp.tile(g_q, N_HEADS).reshape(1, W_HEADS)
    gk = jnp.tile(g_k, 2).reshape(1, LANES)
    head_of = np.arange(W_HEADS) // HEAD_DIM
    head_mean = jnp.asarray((head_of[:, None] == head_of[None, :]) / HEAD_DIM, BF16)

    tpb = seq // tm
    tok = lambda w: pl.BlockSpec((tm, w), lambda i: (i, 0))
    tr = lambda r: pl.BlockSpec((None, r, tm), lambda i: (i // tpb, 0, i % tpb))
    sd = lambda w, dt: jax.ShapeDtypeStruct((n_tok, w), dt)
    consts = (w_sb, w_qd, w_kd, w_qi, w_ki, w_vt, w_wt, gq, gk, head_mean)
    head_rows = jax.ShapeDtypeStruct((n_tok // TQ_DSA, N_HEADS, TQ_DSA, LANES), BF16)
    head_rows_spec = pl.BlockSpec((tm // TQ_DSA, N_HEADS, TQ_DSA, LANES), lambda i: (i, 0, 0, 0))
    return pl.pallas_call(
        functools.partial(_inproj_kernel, tiles_per_batch=tpb),
        out_shape=[sd(W_HEADS, BF16)] * 3 + [head_rows, sd(LANES, BF16), head_rows, sd(LANES, BF16),
                                             jax.ShapeDtypeStruct((batch, VT_ROWS, seq), BF16),
                                             jax.ShapeDtypeStruct((batch, N_HEADS, seq), F32)],
        grid=(n_tok // tm,),
        in_specs=[tok(d)] + [_const_spec(w.shape) for w in consts],
        out_specs=[tok(W_HEADS)] * 3 + [head_rows_spec, tok(LANES), head_rows_spec, tok(LANES),
                                        tr(VT_ROWS), tr(N_HEADS)],
        compiler_params=_cparams(1),
        name="inproj",
    )(n2, *consts)


def _split_heads(x, lane):
    xf = x.astype(F32)
    lo = jnp.where(lane < HEAD_DIM, xf, 0.0).astype(x.dtype)
    hi = jnp.where(lane >= HEAD_DIM, xf, 0.0).astype(x.dtype)
    return lo, hi


def _sb_kernel(q_ref, k_ref, v_ref, t_ref, o_ref, qs_ref, run_ref, acc_ref):
    tq = q_ref.shape[0]
    n_diag = tq // TK
    i = pl.program_id(1)
    lane = lax.broadcasted_iota(jnp.int32, (tq, LANES), 1)
    lane_k = lax.broadcasted_iota(jnp.int32, (TK, LANES), 1)
    for p in range(N_PAIRS):
        lo, hi = _split_heads(q_ref[:, p * LANES:(p + 1) * LANES], lane)
        qs_ref[p, :tq, :] = lo
        qs_ref[p, tq:, :] = hi

    def block(k0, key_offset, first):
        masked = key_offset is not None
        strict = (lax.broadcasted_iota(jnp.int32, (2 * tq, TK), 1) + key_offset
                  < lax.broadcasted_iota(jnp.int32, (2 * tq, TK), 0) % tq) if masked else None

        def group(pairs):
            zs = {p: _dot_nt(qs_ref[p], k_ref[pl.ds(k0, TK), p * LANES:(p + 1) * LANES])
                  for p in pairs}
            lbs, ws = {}, {}
            for p in pairs:
                z = zs[p]
                sp = jnp.maximum(z, 0.0) + jnp.log(1.0 + jnp.exp(-jnp.abs(z)))
                l1m = jnp.where(strict, sp, 0.0) if masked else sp
                l_hi = l1m.astype(BF16)
                l_lo = (l1m - l_hi.astype(F32)).astype(BF16)
                ws[p] = _dot(jnp.concatenate([l_hi, l_lo], axis=1), t_ref[...])
                lbs[p] = z - sp
            for p in pairs:
                arg = lbs[p] + ws[p][:, :TK]
                if not first:
                    arg = arg + run_ref[p]
                a = jnp.exp(arg)
                if masked:
                    a = jnp.where(strict, a, 0.0)
                a = a.astype(BF16)
                v_lo, v_hi = _split_heads(v_ref[pl.ds(k0, TK), p * LANES:(p + 1) * LANES], lane_k)
                upd = _dot(jnp.concatenate([a[:tq], a[tq:]], axis=1),
                           jnp.concatenate([v_lo, v_hi], axis=0))
                if first:
                    acc_ref[p] = upd
                    run_ref[p] = ws[p][:, TK:]
                else:
                    acc_ref[p] += upd
                    run_ref[p] += ws[p][:, TK:]

        for pairs in SB_PAIR_GROUPS:
            group(pairs)

    last = (i + 1) * n_diag - 1
    for jj in range(n_diag):
        block(pl.multiple_of((last - jj) * TK, TK), (n_diag - 1 - jj) * TK, jj == 0)

    def more_blocks(jj):
        run_max = jnp.max(functools.reduce(jnp.maximum, [run_ref[p] for p in range(N_PAIRS)]))
        return jnp.logical_and(jj <= last, run_max > -F32_EXP_UNDERFLOW)

    def body(state):
        jj, _ = state
        block(pl.multiple_of((last - jj) * TK, TK), None, False)
        return jj + 1, more_blocks(jj + 1)

    lax.while_loop(lambda state: state[1], body, (jnp.int32(n_diag), i >= 1))
    for p in range(N_PAIRS):
        o_ref[:, p * LANES:(p + 1) * LANES] = acc_ref[p].astype(o_ref.dtype)


def _sb_call(q, k, v, batch, seq):
    q, k, v = (t.reshape(batch, seq, W_HEADS) for t in (q, k, v))
    j_idx = np.arange(TK)
    suffix = (j_idx[:, None] > j_idx[None, :]).astype(np.float32)
    tmat = -np.concatenate([suffix, np.ones((TK, LANES), np.float32)], axis=1)
    tmat = jnp.asarray(np.concatenate([tmat, tmat], axis=0), BF16)
    blk = pl.BlockSpec((None, TQ_SB, W_HEADS), lambda b, i: (b, i, 0))
    allk = pl.BlockSpec((None, seq, W_HEADS), lambda b, i: (b, 0, 0))
    return pl.pallas_call(
        _sb_kernel,
        out_shape=jax.ShapeDtypeStruct((batch, seq, W_HEADS), BF16),
        grid=(batch, seq // TQ_SB),
        in_specs=[blk, allk, allk, _const_spec(tmat.shape)],
        out_specs=blk,
        scratch_shapes=[pltpu.VMEM((N_PAIRS, 2 * TQ_SB, LANES), BF16),
                        pltpu.VMEM((N_PAIRS, 2 * TQ_SB, LANES), F32),
                        pltpu.VMEM((N_PAIRS, TQ_SB, LANES), F32)],
        compiler_params=_cparams(2),
        name="sb",
    )(q, k, v, tmat)


def _dsa_kernel(qd_st, qi_st, wit_ref, kk_ref, vt_ref, kki_ref, o_ref,
                sc_ref, top_ref, m_ref, acc_ref, buf_ref, qit_ref, qdt_ref, *, topk):
    TQ = qd_st.shape[1]
    i = pl.program_id(1)
    t0 = i * TQ

    n_s = (i + CH_S // TQ) // (CH_S // TQ)

    for q_st, qt_ref in ((qi_st, qit_ref), (qd_st, qdt_ref)):
        for p in range(N_PAIRS):
            qt_ref[p] = jnp.concatenate(
                [q_st[h].astype(F32).T for h in (2 * p, 2 * p + 1)], axis=1).astype(BF16)

    def products(qt_ref, k_ref):
        def produce(k0):
            kch = k_ref[pl.ds(k0, CH_H), :]
            return [_dot(kch, qt_ref[p]) for p in range(N_PAIRS)]
        return produce

    def store_products(vals):
        for p in range(N_PAIRS):
            buf_ref[2 * p] = vals[p][:, :TQ]
            buf_ref[2 * p + 1] = vals[p][:, TQ:]

    def sweep(produce, consume):
        from_buf = lambda h: buf_ref[h]

        def trip(c, prefetch):
            k_a = pl.multiple_of(c * CH_S, CH_S)
            k_b = pl.multiple_of(k_a + CH_H, CH_H)
            vals_b = produce(k_b)
            consume(from_buf, k_a)
            if prefetch:
                vals_next = produce(pl.multiple_of(k_a + CH_S, CH_S))
            consume(lambda h: vals_b[h // 2][:, (h % 2) * TQ:(h % 2 + 1) * TQ], k_b)
            if prefetch:
                store_products(vals_next)

        store_products(produce(0))

        def body(c, _):
            trip(c, True)
            return 0

        lax.fori_loop(0, n_s - 1, body, 0)
        trip(n_s - 1, False)

    wit = wit_ref[...]

    def score_consume(prod, k0):
        score = None
        for h in range(N_HEADS):
            term = jnp.maximum(prod(h), 0.0) * wit[h:h + 1, :]
            score = term if score is None else score + term
        causal = (k0 + lax.broadcasted_iota(jnp.int32, (CH_H, TQ), 0)
                  <= t0 + lax.broadcasted_iota(jnp.int32, (CH_H, TQ), 1))
        score = jnp.where(causal, score, -jnp.inf)
        sc_ref[pl.ds(k0, CH_H), :] = score
        top_ref[pl.ds(k0, CH_H), :] = score.astype(BF16)

    sweep(products(qit_ref, kki_ref), score_consume)

    def float_of_rank(u):
        key = u ^ INT_MIN
        return pltpu.bitcast(jnp.where(key < 0, key ^ np.int32(0x7FFFFFFF), key), F32)

    def count_top(pred):
        one, zero = jnp.ones((BF16_SUBLANES, TQ), BF16), jnp.zeros((BF16_SUBLANES, TQ), BF16)

        def cbody(c, acc):
            k0 = pl.multiple_of(c * CH_S, CH_S)
            tops = top_ref[pl.ds(k0, CH_S), :]
            sums = [acc, zero]
            for j in range(CH_S // BF16_SUBLANES):
                hit = jnp.where(pred(tops[j * BF16_SUBLANES:(j + 1) * BF16_SUBLANES]), one, zero)
                sums[j % 2] = sums[j % 2] + hit
            return sums[0] + sums[1]

        acc = lax.fori_loop(0, n_s, cbody, zero)
        return jnp.sum(acc.astype(F32), axis=0, keepdims=True)

    def top_body(b, u16):
        cand = u16 | jnp.left_shift(jnp.int32(1), 15 - b)
        cand_f = float_of_rank(jnp.left_shift(cand, 16)).astype(BF16)
        cand_t = jnp.broadcast_to(cand_f, (BF16_SUBLANES, TQ))
        return jnp.where(count_top(lambda t: t >= cand_t) >= topk, cand, u16)

    zero_t = jnp.zeros((BF16_SUBLANES, TQ), BF16)
    n_nonneg_row = count_top(lambda t: t >= zero_t)
    n_nonneg = jnp.broadcast_to(n_nonneg_row, (SUBLANES, TQ))
    n_pos = jnp.broadcast_to(count_top(lambda t: t > zero_t), (SUBLANES, TQ))

    u16 = lax.fori_loop(1, 16, top_body, jnp.where(n_nonneg_row >= topk, 1 << 15, 0))

    def count(pred):
        def cbody(c, acc):
            k0 = pl.multiple_of(c * CH_S, CH_S)
            sch = sc_ref[pl.ds(k0, CH_S), :].reshape(CH_S // SUBLANES, SUBLANES, TQ)
            return acc + jnp.sum(jnp.where(pred(sch), 1, 0), axis=0)
        acc = lax.fori_loop(0, n_s, cbody, jnp.zeros((SUBLANES, TQ), jnp.int32))
        tot = jnp.sum(acc.astype(F32), axis=0, keepdims=True)
        return jnp.broadcast_to(tot, (SUBLANES, TQ))

    base = jnp.broadcast_to(jnp.left_shift(u16, 16) - (1 << 15), (SUBLANES, TQ))

    def window_body(b, state):
        off, n_off = state
        cand = off | jnp.left_shift(jnp.int32(1), SELECT_WINDOW_BITS - 1 - b)
        cand_f = float_of_rank(base + cand)
        n_cand = count(lambda sch: sch >= cand_f[None])
        take = n_cand >= topk
        return jnp.where(take, cand, off), jnp.where(take, n_cand, n_off)

    few_keys = t0 + lax.broadcasted_iota(jnp.int32, (SUBLANES, TQ), 1) < topk
    unknown = jnp.full((SUBLANES, TQ), -1.0, F32)
    state = lax.fori_loop(0, SELECT_EARLY_PASSES, window_body,
                          (jnp.zeros((SUBLANES, TQ), jnp.int32), unknown))
    zero_kth = jnp.logical_and(jnp.logical_and(n_pos < topk, n_nonneg >= topk),
                               jnp.logical_not(few_keys))
    exempt = jnp.logical_or(few_keys, zero_kth)

    def any_open(n_at_off):
        settled = jnp.logical_or(n_at_off == topk, exempt)
        return jnp.max(jnp.where(settled, 0.0, 1.0)) > 0.0

    def late_pass(carry):
        b, st, _ = carry
        st = window_body(b, st)
        return b + 1, st, jnp.logical_and(b + 1 < SELECT_WINDOW_BITS, any_open(st[1]))

    _, (offset, n_off), _ = lax.while_loop(
        lambda carry: carry[2], late_pass,
        (jnp.int32(SELECT_EARLY_PASSES), state, any_open(state[1])))
    thr = jnp.where(few_keys, jnp.finfo(F32).min,
                    jnp.where(zero_kth, 0.0, float_of_rank(base + offset)))
    thr_row = thr[0:1, :]

    n_ge = jnp.where(few_keys, 0.0,
                     jnp.where(zero_kth, n_nonneg, jnp.where(n_off < 0.0, topk + 1.0, n_off)))

    @pl.when(jnp.max(n_ge) > topk)
    def _():
        need = topk - count(lambda sch: sch > thr[None])[0:1, :]
        before = jnp.where(lax.broadcasted_iota(jnp.int32, (CH_H, CH_H), 1)
                           < lax.broadcasted_iota(jnp.int32, (CH_H, CH_H), 0), 1.0, 0.0).astype(BF16)

        def tie_body(c, seen):
            k0s = [pl.multiple_of(c * CH_S + j * CH_H, CH_H) for j in range(CH_S // CH_H)]
            schs = [sc_ref[pl.ds(k0, CH_H), :] for k0 in k0s]
            eqs = [jnp.where(sch == thr_row, 1.0, 0.0) for sch in schs]
            ranks = [_dot(before, eq.astype(BF16)) for eq in eqs]
            for k0, sch, eq, rank in zip(k0s, schs, eqs, ranks):
                tie_rank = jnp.where(sch == thr_row, rank + seen, -1.0)
                sc_ref[pl.ds(k0, CH_H), :] = jnp.where(tie_rank >= need, -jnp.inf, sch)
                seen = seen + jnp.sum(eq, axis=0, keepdims=True)
            return seen

        lax.fori_loop(0, n_s, tie_body, jnp.zeros((1, TQ), F32))

    m_ref[...] = jnp.full(m_ref.shape, NEG_BIG, F32)
    acc_ref[...] = jnp.zeros(acc_ref.shape, F32)

    def attn_consume(prod, k0):
        vch = vt_ref[:, pl.ds(k0, CH_H)]
        sel = sc_ref[pl.ds(k0, CH_H), :] >= thr_row
        for p in range(N_PAIRS):
            probs, alphas = [], []
            for h in (2 * p, 2 * p + 1):
                lg = jnp.where(sel, prod(h), NEG_BIG)
                m_old = m_ref[h]
                m_new = jnp.maximum(m_old, jnp.max(lg, axis=0, keepdims=True))
                m_ref[h] = m_new
                alphas.append(jnp.exp2(m_old - m_new))
                probs.append(jnp.exp2(lg - m_new).astype(BF16))
            upd = _dot(vch, jnp.concatenate(probs, axis=1))
            acc_ref[p] = acc_ref[p] * jnp.concatenate(alphas, axis=1) + upd

    sweep(products(qdt_ref, kk_ref), attn_consume)
    for p in range(N_PAIRS):
        acc = acc_ref[p]
        out_t = acc[:HEAD_DIM] / acc[HEAD_DIM:HEAD_DIM + 1]
        pair = jnp.concatenate([out_t[:, :TQ], out_t[:, TQ:]], axis=0)
        o_ref[:, p * LANES:(p + 1) * LANES] = pair.T.astype(o_ref.dtype)


def _dsa_call(qd_rows, qi_rows, wit, kk, vt, kki, batch, seq):
    r3 = lambda t: t.reshape(batch, seq, t.shape[-1])
    kk, kki = map(r3, (kk, kki))
    topk = min(TOPK_MAX, seq // 4)
    assert seq // BF16_SUBLANES <= BF16_MAX_EXACT_COUNT
    TQ = TQ_DSA
    nq = seq // TQ
    rows = pl.BlockSpec((None, N_HEADS, TQ, LANES), lambda b, i: (b * nq + i, 0, 0, 0))
    allk = pl.BlockSpec((None, seq, LANES), lambda b, i: (b, 0, 0))
    return pl.pallas_call(
        functools.partial(_dsa_kernel, topk=topk),
        out_shape=jax.ShapeDtypeStruct((batch, seq, W_HEADS), BF16),
        grid=(batch, nq),
        in_specs=[rows, rows,
                  pl.BlockSpec((None, N_HEADS, TQ), lambda b, i: (b, 0, i)),
                  allk, pl.BlockSpec((None, VT_ROWS, seq), lambda b, i: (b, 0, 0)), allk],
        out_specs=pl.BlockSpec((None, TQ, W_HEADS), lambda b, i: (b, i, 0)),
        scratch_shapes=[pltpu.VMEM((seq, TQ), F32),
                        pltpu.VMEM((seq, TQ), BF16),
                        pltpu.VMEM((N_HEADS, 1, TQ), F32),
                        pltpu.VMEM((N_PAIRS, VT_ROWS, 2 * TQ), F32),
                        pltpu.VMEM((N_HEADS, CH_H, TQ), F32),
                        pltpu.VMEM((N_PAIRS, LANES, 2 * TQ), BF16),
                        pltpu.VMEM((N_PAIRS, LANES, 2 * TQ), BF16)],
        compiler_params=_cparams(2),
        name="dsa",
    )(qd_rows, qi_rows, wit, kk, vt, kki)


def _mix_kernel(h_ref, n_ref, ysb_ref, yd_ref, mod_ref, wps_ref, wpd_ref, wg_ref, bg_ref,
                wo_ref, o_ref):
    d = h_ref.shape[1]
    y_sb = _dot(ysb_ref[...], wps_ref[...])
    y_d = _dot(yd_ref[...], wpd_ref[...])
    gates = jax.nn.sigmoid(_dot(n_ref[...], wg_ref[...]) + bg_ref[...])
    merged = (gates[:, :d] * y_sb + gates[:, d:] * y_d).astype(BF16)
    o_ref[...] = h_ref[...] + mod_ref[5] * _dot(merged, wo_ref[...])


def _mix_call(h, n2, y_sb, y_d, mod, w_proj_sb, w_proj_dsa, w_gate, b_gate, w_out, *, seq, tm=512):
    n_tok, d = h.shape
    tok = lambda w: pl.BlockSpec((tm, w), lambda i: (i, 0))
    ws = [w.astype(BF16) for w in (w_proj_sb, w_proj_dsa, w_gate)]
    bg = b_gate.reshape(1, -1)
    wo = w_out.astype(BF16)
    return pl.pallas_call(
        _mix_kernel,
        out_shape=jax.ShapeDtypeStruct((n_tok, d), F32),
        grid=(n_tok // tm,),
        in_specs=[tok(d), tok(d), tok(W_HEADS), tok(W_HEADS), _mod_spec(seq // tm, d)]
                 + [_const_spec(w.shape) for w in (*ws, bg, wo)],
        out_specs=tok(d),
        compiler_params=_cparams(1),
        name="mix",
    )(h, n2, y_sb, y_d, mod, *ws, bg, wo)


def kernel(x, c, w_ada, b_ada, g_ffn1, w1_ffn1, w3_ffn1, w2_ffn1, g_mix, w_in, g_q_dsa, g_k_dsa,
           w_proj_sb, w_proj_dsa, w_gate, b_gate, w_out, g_ffn2, w1_ffn2, w3_ffn2, w2_ffn2):
    batch, seq, d = x.shape
    h = x.reshape(batch * seq, d)
    for l in range(w_ada.shape[0]):
        bf = lambda w: w[l].astype(BF16)
        mod = _mod_call(c, w_ada[l], b_ada[l])
        h, n2 = _ffn_call(h, mod, g_ffn1[l], bf(w1_ffn1), bf(w3_ffn1), bf(w2_ffn1), g_mix[l],
                          mod_base=0, emit_next=True, seq=seq)
        q_sb, k_sb, v_sb, q_d, kk, q_i, kki, vt, wit = _inproj_call(
            n2, w_in[l], g_q_dsa[l], g_k_dsa[l], batch, seq)
        y_sb = _sb_call(q_sb, k_sb, v_sb, batch, seq).reshape(batch * seq, W_HEADS)
        y_d = _dsa_call(q_d, q_i, wit, kk, vt, kki, batch, seq).reshape(batch * seq, W_HEADS)
        h = _mix_call(h, n2, y_sb, y_d, mod, w_proj_sb[l], w_proj_dsa[l], w_gate[l], b_gate[l],
                      w_out[l], seq=seq)
        (h,) = _ffn_call(h, mod, g_ffn2[l], bf(w1_ffn2), bf(w3_ffn2), bf(w2_ffn2), g_ffn2[l],
                         mod_base=6, emit_next=False, seq=seq)
    return h.reshape(batch, seq, d)
```

```python
import functools

import numpy as np
import jax
import jax.numpy as jnp
from jax import lax
from jax.experimental import pallas as pl
from jax.experimental.pallas import tpu as pltpu

F32 = jnp.float32
BF16 = jnp.bfloat16

HEAD_DIM = 64
N_HEADS = 8
W_HEADS = N_HEADS * HEAD_DIM
N_PAIRS = N_HEADS // 2
LANES = 128
SUBLANES = 8
BF16_SUBLANES = 16
BF16_MAX_EXACT_COUNT = 256
SELECT_WINDOW_BITS = 17
SELECT_EARLY_PASSES = 11
TOPK_MAX = 256
N_MOD = 9
RMS_EPS = 1e-6
D_IDX = 64

TQ_SB = 256
TQ_DSA = 512
TK = 128
SB_PAIR_GROUPS = ((0, 1, 2, 3),)
CH_S = 512
CH_H = 256
VT_ROWS = 80
LANE_POS_HI, LANE_POS_LO = HEAD_DIM, HEAD_DIM + 1
N_POS_LANES = 2
N_SLOPE_PARTS = 3
LOG2E = float(np.log2(np.e))
NEG_BIG = -1e30
F32_EXP_UNDERFLOW = 104.0
INT_MIN = np.int32(-2**31)
VMEM_LIMIT = 56 * 1024 * 1024


def _cparams(n_axes):
    return pltpu.CompilerParams(
        dimension_semantics=("arbitrary",) * n_axes, vmem_limit_bytes=VMEM_LIMIT)


def _const_spec(shape):
    nd = len(shape)
    return pl.BlockSpec(shape, lambda *_: (0,) * nd, pipeline_mode=pl.Buffered(1))


def _dot(a, b):
    return jnp.dot(a, b, preferred_element_type=F32)


def _dot_nt(a, b):
    return lax.dot_general(a, b, (((1,), (1,)), ((), ())), preferred_element_type=F32)


def _mod_kernel(c_ref, w_ref, b_ref, o_ref, sc_ref):
    nb, kdim, _ = c_ref.shape
    tn = w_ref.shape[1]
    nch = tn // LANES

    @pl.when(pl.program_id(0) == 0)
    def _():
        c = c_ref[...]
        sc_ref[...] = c * jax.nn.sigmoid(c)

    def body(kb, accs):
        k0 = pl.multiple_of(kb * SUBLANES, SUBLANES)
        wblk = w_ref[pl.ds(k0, SUBLANES), :]
        out = []
        for b in range(nb):
            sb = sc_ref[b, pl.ds(k0, SUBLANES), :]
            for ch in range(nch):
                out.append(accs[b * nch + ch] + wblk[:, ch * LANES:(ch + 1) * LANES] * sb)
        return tuple(out)

    accs = lax.fori_loop(0, kdim // SUBLANES, body,
                         tuple(jnp.zeros((SUBLANES, LANES), F32) for _ in range(nb * nch)))
    for b in range(nb):
        row = jnp.concatenate(
            [jnp.sum(accs[b * nch + ch], axis=0, keepdims=True) for ch in range(nch)], axis=1)
        o_ref[0, b] = row + b_ref[...]


def _mod_call(c, w_ada, b_ada):
    nb, d = c.shape
    n_out = w_ada.shape[1]
    c_rep = jnp.broadcast_to(c[:, :, None], (nb, d, LANES))
    return pl.pallas_call(
        _mod_kernel,
        out_shape=jax.ShapeDtypeStruct((n_out // d, nb, 1, d), F32),
        grid=(n_out // d,),
        in_specs=[_const_spec((nb, d, LANES)),
                  pl.BlockSpec((d, d), lambda j: (0, j)),
                  pl.BlockSpec((1, d), lambda j: (0, j))],
        out_specs=pl.BlockSpec((1, nb, 1, d), lambda j: (j, 0, 0, 0)),
        scratch_shapes=[pltpu.VMEM((nb, d, LANES), F32)],
        compiler_params=_cparams(1),
        name="mod",
    )(c_rep, w_ada, b_ada.reshape(1, n_out))


def _norm_mod(h, g, shift, scale):
    y = h * lax.rsqrt(jnp.mean(h * h, axis=-1, keepdims=True) + RMS_EPS)
    return (y * g) * (1.0 + scale) + shift


def _ffn_kernel(h_ref, mod_ref, g_ref, w1_ref, w3_ref, w2_ref, gn_ref, *out_refs,
                mod_base, emit_next, n_chunks):
    o_ref = out_refs[0]
    h = h_ref[...]
    n = _norm_mod(h, g_ref[...], mod_ref[mod_base], mod_ref[mod_base + 1]).astype(BF16)
    fc = w1_ref.shape[1] // n_chunks
    acc = jnp.zeros(h.shape, F32)
    for c in range(n_chunks):
        a = _dot(n, w1_ref[:, c * fc:(c + 1) * fc])
        b = _dot(n, w3_ref[:, c * fc:(c + 1) * fc])
        act = (a * jax.nn.sigmoid(a) * b).astype(BF16)
        acc = acc + _dot(act, w2_ref[c * fc:(c + 1) * fc, :])
    hn = h + (0.5 * mod_ref[mod_base + 2]) * acc
    o_ref[...] = hn
    if emit_next:
        out_refs[1][...] = _norm_mod(hn, gn_ref[...], mod_ref[mod_base + 3],
                                     mod_ref[mod_base + 4]).astype(BF16)


def _mod_spec(tiles_per_batch, d):
    return pl.BlockSpec((N_MOD, None, 1, d), lambda i: (0, i // tiles_per_batch, 0, 0))


def _ffn_call(h, mod, g, w1, w3, w2, g_next, *, mod_base, emit_next, seq, tm=512):
    n_tok, d = h.shape
    f = w1.shape[1]
    tok = pl.BlockSpec((tm, d), lambda i: (i, 0))
    out_shape = [jax.ShapeDtypeStruct((n_tok, d), F32)]
    out_specs = [tok]
    if emit_next:
        out_shape.append(jax.ShapeDtypeStruct((n_tok, d), BF16))
        out_specs.append(tok)
    return pl.pallas_call(
        functools.partial(_ffn_kernel, mod_base=mod_base, emit_next=emit_next, n_chunks=2),
        out_shape=out_shape,
        grid=(n_tok // tm,),
        in_specs=[tok, _mod_spec(seq // tm, d), _const_spec((1, d)),
                  _const_spec((d, f)), _const_spec((d, f)), _const_spec((f, d)),
                  _const_spec((1, d))],
        out_specs=out_specs,
        compiler_params=_cparams(1),
        name="ffn",
    )(h, mod, g.reshape(1, d), w1, w3, w2, g_next.reshape(1, d))


def _alibi_slope_parts():
    out = []
    for h in range(N_HEADS):
        rest, parts = np.float32(2.0 ** (-8.0 * (h + 1) / N_HEADS) * LOG2E), []
        for _ in range(N_SLOPE_PARTS):
            parts.append(np.float32(np.asarray(rest, dtype=BF16)))
            rest = np.float32(rest - parts[-1])
        out.append(parts)
    return out


def _store_head_rows(o_ref, x, slope_parts):
    tm = x.shape[0]
    lane = lax.broadcasted_iota(jnp.int32, (tm, LANES), 1)
    for h in range(N_HEADS):
        rows = x[:, (h // 2) * LANES:(h // 2 + 1) * LANES]
        if h % 2:
            rows = pltpu.roll(rows, HEAD_DIM, axis=1)
        rows = jnp.where(lane < HEAD_DIM, rows, 0.0)
        if slope_parts is not None:
            for rep, part in enumerate(slope_parts[h]):
                rows = jnp.where(lane == LANE_POS_HI + rep * N_POS_LANES, part * HEAD_DIM, rows)
                rows = jnp.where(lane == LANE_POS_LO + rep * N_POS_LANES, part, rows)
        o_ref[:, h] = rows.astype(BF16).reshape(tm // TQ_DSA, TQ_DSA, LANES)


def _inproj_kernel(n_ref, wsb_ref, wqd_ref, wkd_ref, wqi_ref, wki_ref, wvt_ref, wwt_ref,
                   gq_ref, gk_ref, hm_ref,
                   qsb_ref, ksb_ref, vsb_ref, qd_ref, kk_ref, qi_ref, kki_ref, vt_ref, wit_ref,
                   *, tiles_per_batch):
    n = n_ref[...]
    tm = n.shape[0]
    qk_scale = HEAD_DIM ** -0.5

    qd = _dot(n, wqd_ref[...])
    ms = _dot((qd * qd).astype(BF16), hm_ref[...])
    _store_head_rows(qd_ref, qd * lax.rsqrt(ms + RMS_EPS) * (gq_ref[...] * (qk_scale * LOG2E)),
                     _alibi_slope_parts())

    kd = _dot(n, wkd_ref[...])
    kn = kd * lax.rsqrt(jnp.mean(kd * kd, axis=-1, keepdims=True) + RMS_EPS) * gk_ref[...]
    lane = lax.broadcasted_iota(jnp.int32, (tm, LANES), 1)
    pos = ((pl.program_id(0) % tiles_per_batch) * tm
           + lax.broadcasted_iota(jnp.int32, (tm, LANES), 0))
    feat = jnp.zeros((tm, LANES), jnp.int32)
    for rep in range(N_SLOPE_PARTS):
        feat = jnp.where(lane == LANE_POS_HI + rep * N_POS_LANES,
                         pos >> (HEAD_DIM.bit_length() - 1), feat)
        feat = jnp.where(lane == LANE_POS_LO + rep * N_POS_LANES, pos & (HEAD_DIM - 1), feat)
    feat = feat.astype(F32)
    kk_ref[...] = jnp.where(lane < HEAD_DIM, kn, feat).astype(BF16)

    _store_head_rows(qi_ref, _dot(n, wqi_ref[...]), None)
    kki_ref[...] = _dot(n, wki_ref[...]).astype(BF16)
    vt = _dot_nt(wvt_ref[...], n)
    row = lax.broadcasted_iota(jnp.int32, vt.shape, 0)
    vt_ref[...] = jnp.where(row == HEAD_DIM, 1.0, vt).astype(BF16)
    wit_ref[...] = _dot_nt(wwt_ref[...], n)[:N_HEADS]

    sb = _dot(n, wsb_ref[...])
    qsb_ref[...] = (sb[:, :W_HEADS] * qk_scale).astype(BF16)
    ksb_ref[...] = sb[:, W_HEADS:2 * W_HEADS].astype(BF16)
    vsb_ref[...] = sb[:, 2 * W_HEADS:].astype(BF16)


def _inproj_call(n2, w_in, g_q, g_k, batch, seq, tm=1024):
    n_tok, d = n2.shape
    sizes = (W_HEADS, W_HEADS, W_HEADS, W_HEADS, HEAD_DIM, HEAD_DIM, N_HEADS * D_IDX, D_IDX, N_HEADS)
    offs = np.concatenate([[0], np.cumsum(sizes)])
    col = lambda k: w_in[:, offs[k]:offs[k + 1]]
    idx_scale = (D_IDX ** -0.5) * (N_HEADS ** -0.5)
    w_sb = jnp.concatenate([col(0), col(1), col(2)], axis=1).astype(BF16)
    w_qd = col(3).astype(BF16)
    w_kd = jnp.concatenate([col(4), col(4)], axis=1).astype(BF16)
    w_qi = col(6).astype(BF16)
    w_ki = jnp.pad(col(7), ((0, 0), (0, LANES - D_IDX))).astype(BF16)
    w_vt = jnp.pad(col(5).T, ((0, VT_ROWS - HEAD_DIM), (0, 0))).astype(BF16)
    w_wt = jnp.pad((col(8) * idx_scale).T, ((0, 16 - N_HEADS), (0, 0))).astype(BF16)
    gq = jnp.tile(g_q, N_HEADS).reshape(1, W_HEADS)
    gk = jnp.tile(g_k, 2).reshape(1, LANES)
    head_of = np.arange(W_HEADS) // HEAD_DIM
    head_mean = jnp.asarray((head_of[:, None] == head_of[None, :]) / HEAD_DIM, BF16)

    tpb = seq // tm
    tok = lambda w: pl.BlockSpec((tm, w), lambda i: (i, 0))
    tr = lambda r: pl.BlockSpec((None, r, tm), lambda i: (i // tpb, 0, i % tpb))
    sd = lambda w, dt: jax.ShapeDtypeStruct((n_tok, w), dt)
    consts = (w_sb, w_qd, w_kd, w_qi, w_ki, w_vt, w_wt, gq, gk, head_mean)
    head_rows = jax.ShapeDtypeStruct((n_tok // TQ_DSA, N_HEADS, TQ_DSA, LANES), BF16)
    head_rows_spec = pl.BlockSpec((tm // TQ_DSA, N_HEADS, TQ_DSA, LANES), lambda i: (i, 0, 0, 0))
    return pl.pallas_call(
        functools.partial(_inproj_kernel, tiles_per_batch=tpb),
        out_shape=[sd(W_HEADS, BF16)] * 3 + [head_rows, sd(LANES, BF16), head_rows, sd(LANES, BF16),
                                             jax.ShapeDtypeStruct((batch, VT_ROWS, seq), BF16),
                                             jax.ShapeDtypeStruct((batch, N_HEADS, seq), F32)],
        grid=(n_tok // tm,),
        in_specs=[tok(d)] + [_const_spec(w.shape) for w in consts],
        out_specs=[tok(W_HEADS)] * 3 + [head_rows_spec, tok(LANES), head_rows_spec, tok(LANES),
                                        tr(VT_ROWS), tr(N_HEADS)],
        compiler_params=_cparams(1),
        name="inproj",
    )(n2, *consts)


def _split_heads(x, lane):
    xf = x.astype(F32)
    lo = jnp.where(lane < HEAD_DIM, xf, 0.0).astype(x.dtype)
    hi = jnp.where(lane >= HEAD_DIM, xf, 0.0).astype(x.dtype)
    return lo, hi


def _sb_kernel(q_ref, k_ref, v_ref, t_ref, o_ref, qs_ref, run_ref, acc_ref):
    tq = q_ref.shape[0]
    n_diag = tq // TK
    i = pl.program_id(1)
    lane = lax.broadcasted_iota(jnp.int32, (tq, LANES), 1)
    lane_k = lax.broadcasted_iota(jnp.int32, (TK, LANES), 1)
    for p in range(N_PAIRS):
        lo, hi = _split_heads(q_ref[:, p * LANES:(p + 1) * LANES], lane)
        qs_ref[p, :tq, :] = lo
        qs_ref[p, tq:, :] = hi

    def block(k0, key_offset, first):
        masked = key_offset is not None
        strict = (lax.broadcasted_iota(jnp.int32, (2 * tq, TK), 1) + key_offset
                  < lax.broadcasted_iota(jnp.int32, (2 * tq, TK), 0) % tq) if masked else None

        def group(pairs):
            zs = {p: _dot_nt(qs_ref[p], k_ref[pl.ds(k0, TK), p * LANES:(p + 1) * LANES])
                  for p in pairs}
            lbs, ws = {}, {}
            for p in pairs:
                z = zs[p]
                sp = jnp.maximum(z, 0.0) + jnp.log(1.0 + jnp.exp(-jnp.abs(z)))
                l1m = jnp.where(strict, sp, 0.0) if masked else sp
                l_hi = l1m.astype(BF16)
                l_lo = (l1m - l_hi.astype(F32)).astype(BF16)
                ws[p] = _dot(jnp.concatenate([l_hi, l_lo], axis=1), t_ref[...])
                lbs[p] = z - sp
            for p in pairs:
                arg = lbs[p] + ws[p][:, :TK]
                if not first:
                    arg = arg + run_ref[p]
                a = jnp.exp(arg)
                if masked:
                    a = jnp.where(strict, a, 0.0)
                a = a.astype(BF16)
                v_lo, v_hi = _split_heads(v_ref[pl.ds(k0, TK), p * LANES:(p + 1) * LANES], lane_k)
                upd = _dot(jnp.concatenate([a[:tq], a[tq:]], axis=1),
                           jnp.concatenate([v_lo, v_hi], axis=0))
                if first:
                    acc_ref[p] = upd
                    run_ref[p] = ws[p][:, TK:]
                else:
                    acc_ref[p] += upd
                    run_ref[p] += ws[p][:, TK:]

        for pairs in SB_PAIR_GROUPS:
            group(pairs)

    last = (i + 1) * n_diag - 1
    for jj in range(n_diag):
        block(pl.multiple_of((last - jj) * TK, TK), (n_diag - 1 - jj) * TK, jj == 0)

    def more_blocks(jj):
        run_max = jnp.max(functools.reduce(jnp.maximum, [run_ref[p] for p in range(N_PAIRS)]))
        return jnp.logical_and(jj <= last, run_max > -F32_EXP_UNDERFLOW)

    def body(state):
        jj, _ = state
        block(pl.multiple_of((last - jj) * TK, TK), None, False)
        return jj + 1, more_blocks(jj + 1)

    lax.while_loop(lambda state: state[1], body, (jnp.int32(n_diag), i >= 1))
    for p in range(N_PAIRS):
        o_ref[:, p * LANES:(p + 1) * LANES] = acc_ref[p].astype(o_ref.dtype)


def _sb_call(q, k, v, batch, seq):
    q, k, v = (t.reshape(batch, seq, W_HEADS) for t in (q, k, v))
    j_idx = np.arange(TK)
    suffix = (j_idx[:, None] > j_idx[None, :]).astype(np.float32)
    tmat = -np.concatenate([suffix, np.ones((TK, LANES), np.float32)], axis=1)
    tmat = jnp.asarray(np.concatenate([tmat, tmat], axis=0), BF16)
    blk = pl.BlockSpec((None, TQ_SB, W_HEADS), lambda b, i: (b, i, 0))
    allk = pl.BlockSpec((None, seq, W_HEADS), lambda b, i: (b, 0, 0))
    return pl.pallas_call(
        _sb_kernel,
        out_shape=jax.ShapeDtypeStruct((batch, seq, W_HEADS), BF16),
        grid=(batch, seq // TQ_SB),
        in_specs=[blk, allk, allk, _const_spec(tmat.shape)],
        out_specs=blk,
        scratch_shapes=[pltpu.VMEM((N_PAIRS, 2 * TQ_SB, LANES), BF16),
                        pltpu.VMEM((N_PAIRS, 2 * TQ_SB, LANES), F32),
                        pltpu.VMEM((N_PAIRS, TQ_SB, LANES), F32)],
        compiler_params=_cparams(2),
        name="sb",
    )(q, k, v, tmat)


def _dsa_kernel(qd_st, qi_st, wit_ref, kk_ref, vt_ref, kki_ref, o_ref,
                sc_ref, top_ref, m_ref, acc_ref, buf_ref, qit_ref, qdt_ref, *, topk):
    TQ = qd_st.shape[1]
    i = pl.program_id(1)
    t0 = i * TQ

    n_s = (i + CH_S // TQ) // (CH_S // TQ)

    for q_st, qt_ref in ((qi_st, qit_ref), (qd_st, qdt_ref)):
        for p in range(N_PAIRS):
            qt_ref[p] = jnp.concatenate(
                [q_st[h].astype(F32).T for h in (2 * p, 2 * p + 1)], axis=1).astype(BF16)

    def products(qt_ref, k_ref):
        def produce(k0):
            kch = k_ref[pl.ds(k0, CH_H), :]
            return [_dot(kch, qt_ref[p]) for p in range(N_PAIRS)]
        return produce

    def store_products(vals):
        for p in range(N_PAIRS):
            buf_ref[2 * p] = vals[p][:, :TQ]
            buf_ref[2 * p + 1] = vals[p][:, TQ:]

    def sweep(produce, consume):
        from_buf = lambda h: buf_ref[h]

        def trip(c, prefetch):
            k_a = pl.multiple_of(c * CH_S, CH_S)
            k_b = pl.multiple_of(k_a + CH_H, CH_H)
            vals_b = produce(k_b)
            consume(from_buf, k_a)
            if prefetch:
                vals_next = produce(pl.multiple_of(k_a + CH_S, CH_S))
            consume(lambda h: vals_b[h // 2][:, (h % 2) * TQ:(h % 2 + 1) * TQ], k_b)
            if prefetch:
                store_products(vals_next)

        store_products(produce(0))

        def body(c, _):
            trip(c, True)
            return 0

        lax.fori_loop(0, n_s - 1, body, 0)
        trip(n_s - 1, False)

    wit = wit_ref[...]

    def score_consume(prod, k0):
        score = None
        for h in range(N_HEADS):
            term = jnp.maximum(prod(h), 0.0) * wit[h:h + 1, :]
            score = term if score is None else score + term
        causal = (k0 + lax.broadcasted_iota(jnp.int32, (CH_H, TQ), 0)
                  <= t0 + lax.broadcasted_iota(jnp.int32, (CH_H, TQ), 1))
        score = jnp.where(causal, score, -jnp.inf)
        sc_ref[pl.ds(k0, CH_H), :] = score
        top_ref[pl.ds(k0, CH_H), :] = score.astype(BF16)

    sweep(products(qit_ref, kki_ref), score_consume)

    def float_of_rank(u):
        key = u ^ INT_MIN
        return pltpu.bitcast(jnp.where(key < 0, key ^ np.int32(0x7FFFFFFF), key), F32)

    def count_top(pred):
        one, zero = jnp.ones((BF16_SUBLANES, TQ), BF16), jnp.zeros((BF16_SUBLANES, TQ), BF16)

        def cbody(c, acc):
            k0 = pl.multiple_of(c * CH_S, CH_S)
            tops = top_ref[pl.ds(k0, CH_S), :]
            sums = [acc, zero]
            for j in range(CH_S // BF16_SUBLANES):
                hit = jnp.where(pred(tops[j * BF16_SUBLANES:(j + 1) * BF16_SUBLANES]), one, zero)
                sums[j % 2] = sums[j % 2] + hit
            return sums[0] + sums[1]

        acc = lax.fori_loop(0, n_s, cbody, zero)
        return jnp.sum(acc.astype(F32), axis=0, keepdims=True)

    def top_body(b, u16):
        cand = u16 | jnp.left_shift(jnp.int32(1), 15 - b)
        cand_f = float_of_rank(jnp.left_shift(cand, 16)).astype(BF16)
        cand_t = jnp.broadcast_to(cand_f, (BF16_SUBLANES, TQ))
        return jnp.where(count_top(lambda t: t >= cand_t) >= topk, cand, u16)

    u16 = lax.fori_loop(0, 16, top_body, jnp.zeros((1, TQ), jnp.int32))

    zero_t = jnp.zeros((BF16_SUBLANES, TQ), BF16)
    n_nonneg = jnp.broadcast_to(count_top(lambda t: t >= zero_t), (SUBLANES, TQ))
    n_pos = jnp.broadcast_to(count_top(lambda t: t > zero_t), (SUBLANES, TQ))

    def count(pred):
        def cbody(c, acc):
            k0 = pl.multiple_of(c * CH_S, CH_S)
            sch = sc_ref[pl.ds(k0, CH_S), :].reshape(CH_S // SUBLANES, SUBLANES, TQ)
            return acc + jnp.sum(jnp.where(pred(sch), 1, 0), axis=0)
        acc = lax.fori_loop(0, n_s, cbody, jnp.zeros((SUBLANES, TQ), jnp.int32))
        tot = jnp.sum(acc.astype(F32), axis=0, keepdims=True)
        return jnp.broadcast_to(tot, (SUBLANES, TQ))

    base = jnp.broadcast_to(jnp.left_shift(u16, 16) - (1 << 15), (SUBLANES, TQ))

    def window_body(b, state):
        off, n_off = state
        cand = off | jnp.left_shift(jnp.int32(1), SELECT_WINDOW_BITS - 1 - b)
        cand_f = float_of_rank(base + cand)
        n_cand = count(lambda sch: sch >= cand_f[None])
        take = n_cand >= topk
        return jnp.where(take, cand, off), jnp.where(take, n_cand, n_off)

    few_keys = t0 + lax.broadcasted_iota(jnp.int32, (SUBLANES, TQ), 1) < topk
    unknown = jnp.full((SUBLANES, TQ), -1.0, F32)
    state = lax.fori_loop(0, SELECT_EARLY_PASSES, window_body,
                          (jnp.zeros((SUBLANES, TQ), jnp.int32), unknown))
    zero_kth = jnp.logical_and(jnp.logical_and(n_pos < topk, n_nonneg >= topk),
                               jnp.logical_not(few_keys))
    exempt = jnp.logical_or(few_keys, zero_kth)

    def any_open(n_at_off):
        settled = jnp.logical_or(n_at_off == topk, exempt)
        return jnp.max(jnp.where(settled, 0.0, 1.0)) > 0.0

    def late_pass(carry):
        b, st, _ = carry
        st = window_body(b, st)
        return b + 1, st, jnp.logical_and(b + 1 < SELECT_WINDOW_BITS, any_open(st[1]))

    _, (offset, n_off), _ = lax.while_loop(
        lambda carry: carry[2], late_pass,
        (jnp.int32(SELECT_EARLY_PASSES), state, any_open(state[1])))
    thr = jnp.where(few_keys, jnp.finfo(F32).min,
                    jnp.where(zero_kth, 0.0, float_of_rank(base + offset)))
    thr_row = thr[0:1, :]

    n_ge = jnp.where(few_keys, 0.0,
                     jnp.where(zero_kth, n_nonneg, jnp.where(n_off < 0.0, topk + 1.0, n_off)))

    @pl.when(jnp.max(n_ge) > topk)
    def _():
        need = topk - count(lambda sch: sch > thr[None])[0:1, :]
        before = jnp.where(lax.broadcasted_iota(jnp.int32, (CH_H, CH_H), 1)
                           < lax.broadcasted_iota(jnp.int32, (CH_H, CH_H), 0), 1.0, 0.0).astype(BF16)

        def tie_body(c, seen):
            k0s = [pl.multiple_of(c * CH_S + j * CH_H, CH_H) for j in range(CH_S // CH_H)]
            schs = [sc_ref[pl.ds(k0, CH_H), :] for k0 in k0s]
            eqs = [jnp.where(sch == thr_row, 1.0, 0.0) for sch in schs]
            ranks = [_dot(before, eq.astype(BF16)) for eq in eqs]
            for k0, sch, eq, rank in zip(k0s, schs, eqs, ranks):
                tie_rank = jnp.where(sch == thr_row, rank + seen, -1.0)
                sc_ref[pl.ds(k0, CH_H), :] = jnp.where(tie_rank >= need, -jnp.inf, sch)
                seen = seen + jnp.sum(eq, axis=0, keepdims=True)
            return seen

        lax.fori_loop(0, n_s, tie_body, jnp.zeros((1, TQ), F32))

    m_ref[...] = jnp.full(m_ref.shape, NEG_BIG, F32)
    acc_ref[...] = jnp.zeros(acc_ref.shape, F32)

    def attn_consume(prod, k0):
        vch = vt_ref[:, pl.ds(k0, CH_H)]
        sel = sc_ref[pl.ds(k0, CH_H), :] >= thr_row
        for p in range(N_PAIRS):
            probs, alphas = [], []
            for h in (2 * p, 2 * p + 1):
                lg = jnp.where(sel, prod(h), NEG_BIG)
                m_old = m_ref[h]
                m_new = jnp.maximum(m_old, jnp.max(lg, axis=0, keepdims=True))
                m_ref[h] = m_new
                alphas.append(jnp.exp2(m_old - m_new))
                probs.append(jnp.exp2(lg - m_new).astype(BF16))
            upd = _dot(vch, jnp.concatenate(probs, axis=1))
            acc_ref[p] = acc_ref[p] * jnp.concatenate(alphas, axis=1) + upd

    sweep(products(qdt_ref, kk_ref), attn_consume)
    for p in range(N_PAIRS):
        acc = acc_ref[p]
        out_t = acc[:HEAD_DIM] / acc[HEAD_DIM:HEAD_DIM + 1]
        pair = jnp.concatenate([out_t[:, :TQ], out_t[:, TQ:]], axis=0)
        o_ref[:, p * LANES:(p + 1) * LANES] = pair.T.astype(o_ref.dtype)


def _dsa_call(qd_rows, qi_rows, wit, kk, vt, kki, batch, seq):
    r3 = lambda t: t.reshape(batch, seq, t.shape[-1])
    kk, kki = map(r3, (kk, kki))
    topk = min(TOPK_MAX, seq // 4)
    assert seq // BF16_SUBLANES <= BF16_MAX_EXACT_COUNT
    TQ = TQ_DSA
    nq = seq // TQ
    rows = pl.BlockSpec((None, N_HEADS, TQ, LANES), lambda b, i: (b * nq + i, 0, 0, 0))
    allk = pl.BlockSpec((None, seq, LANES), lambda b, i: (b, 0, 0))
    return pl.pallas_call(
        functools.partial(_dsa_kernel, topk=topk),
        out_shape=jax.ShapeDtypeStruct((batch, seq, W_HEADS), BF16),
        grid=(batch, nq),
        in_specs=[rows, rows,
                  pl.BlockSpec((None, N_HEADS, TQ), lambda b, i: (b, 0, i)),
                  allk, pl.BlockSpec((None, VT_ROWS, seq), lambda b, i: (b, 0, 0)), allk],
        out_specs=pl.BlockSpec((None, TQ, W_HEADS), lambda b, i: (b, i, 0)),
        scratch_shapes=[pltpu.VMEM((seq, TQ), F32),
                        pltpu.VMEM((seq, TQ), BF16),
                        pltpu.VMEM((N_HEADS, 1, TQ), F32),
                        pltpu.VMEM((N_PAIRS, VT_ROWS, 2 * TQ), F32),
                        pltpu.VMEM((N_HEADS, CH_H, TQ), F32),
                        pltpu.VMEM((N_PAIRS, LANES, 2 * TQ), BF16),
                        pltpu.VMEM((N_PAIRS, LANES, 2 * TQ), BF16)],
        compiler_params=_cparams(2),
        name="dsa",
    )(qd_rows, qi_rows, wit, kk, vt, kki)


def _mix_kernel(h_ref, n_ref, ysb_ref, yd_ref, mod_ref, wps_ref, wpd_ref, wg_ref, bg_ref,
                wo_ref, o_ref):
    d = h_ref.shape[1]
    y_sb = _dot(ysb_ref[...], wps_ref[...])
    y_d = _dot(yd_ref[...], wpd_ref[...])
    gates = jax.nn.sigmoid(_dot(n_ref[...], wg_ref[...]) + bg_ref[...])
    merged = (gates[:, :d] * y_sb + gates[:, d:] * y_d).astype(BF16)
    o_ref[...] = h_ref[...] + mod_ref[5] * _dot(merged, wo_ref[...])


def _mix_call(h, n2, y_sb, y_d, mod, w_proj_sb, w_proj_dsa, w_gate, b_gate, w_out, *, seq, tm=512):
    n_tok, d = h.shape
    tok = lambda w: pl.BlockSpec((tm, w), lambda i: (i, 0))
    ws = [w.astype(BF16) for w in (w_proj_sb, w_proj_dsa, w_gate)]
    bg = b_gate.reshape(1, -1)
    wo = w_out.astype(BF16)
    return pl.pallas_call(
        _mix_kernel,
        out_shape=jax.ShapeDtypeStruct((n_tok, d), F32),
        grid=(n_tok // tm,),
        in_specs=[tok(d), tok(d), tok(W_HEADS), tok(W_HEADS), _mod_spec(seq // tm, d)]
                 + [_const_spec(w.shape) for w in (*ws, bg, wo)],
        out_specs=tok(d),
        compiler_params=_cparams(1),
        name="mix",
    )(h, n2, y_sb, y_d, mod, *ws, bg, wo)


def _mixffn_kernel(h_ref, n_ref, ysb_ref, yd_ref, mod_ref, wps_ref, wpd_ref, wg_ref, bg_ref,
                   wo_ref, g_ref, w1_ref, w3_ref, w2_ref, o_ref, *, n_chunks):
    d = h_ref.shape[1]
    y_sb = _dot(ysb_ref[...], wps_ref[...])
    y_d = _dot(yd_ref[...], wpd_ref[...])
    gates = jax.nn.sigmoid(_dot(n_ref[...], wg_ref[...]) + bg_ref[...])
    merged = (gates[:, :d] * y_sb + gates[:, d:] * y_d).astype(BF16)
    h = h_ref[...] + mod_ref[5] * _dot(merged, wo_ref[...])
    n = _norm_mod(h, g_ref[...], mod_ref[6], mod_ref[7]).astype(BF16)
    fc = w1_ref.shape[1] // n_chunks
    acc = jnp.zeros(h.shape, F32)
    for c in range(n_chunks):
        a = _dot(n, w1_ref[:, c * fc:(c + 1) * fc])
        b = _dot(n, w3_ref[:, c * fc:(c + 1) * fc])
        act = (a * jax.nn.sigmoid(a) * b).astype(BF16)
        acc = acc + _dot(act, w2_ref[c * fc:(c + 1) * fc, :])
    o_ref[...] = h + (0.5 * mod_ref[8]) * acc


def _mixffn_call(h, n2, y_sb, y_d, mod, w_proj_sb, w_proj_dsa, w_gate, b_gate, w_out,
                 g, w1, w3, w2, *, seq, tm=512):
    n_tok, d = h.shape
    tok = lambda w: pl.BlockSpec((tm, w), lambda i: (i, 0))
    ws = [w.astype(BF16) for w in (w_proj_sb, w_proj_dsa, w_gate)]
    consts = (*ws, b_gate.reshape(1, -1), w_out.astype(BF16), g.reshape(1, d), w1, w3, w2)
    return pl.pallas_call(
        functools.partial(_mixffn_kernel, n_chunks=2),
        out_shape=jax.ShapeDtypeStruct((n_tok, d), F32),
        grid=(n_tok // tm,),
        in_specs=[tok(d), tok(d), tok(W_HEADS), tok(W_HEADS), _mod_spec(seq // tm, d)]
                 + [_const_spec(w.shape) for w in consts],
        out_specs=tok(d),
        compiler_params=_cparams(1),
        name="mixffn",
    )(h, n2, y_sb, y_d, mod, *consts)


def kernel(x, c, w_ada, b_ada, g_ffn1, w1_ffn1, w3_ffn1, w2_ffn1, g_mix, w_in, g_q_dsa, g_k_dsa,
           w_proj_sb, w_proj_dsa, w_gate, b_gate, w_out, g_ffn2, w1_ffn2, w3_ffn2, w2_ffn2):
    batch, seq, d = x.shape
    h = x.reshape(batch * seq, d)
    for l in range(w_ada.shape[0]):
        bf = lambda w: w[l].astype(BF16)
        mod = _mod_call(c, w_ada[l], b_ada[l])
        h, n2 = _ffn_call(h, mod, g_ffn1[l], bf(w1_ffn1), bf(w3_ffn1), bf(w2_ffn1), g_mix[l],
                          mod_base=0, emit_next=True, seq=seq)
        q_sb, k_sb, v_sb, q_d, kk, q_i, kki, vt, wit = _inproj_call(
            n2, w_in[l], g_q_dsa[l], g_k_dsa[l], batch, seq)
        y_sb = _sb_call(q_sb, k_sb, v_sb, batch, seq).reshape(batch * seq, W_HEADS)
        y_d = _dsa_call(q_d, q_i, wit, kk, vt, kki, batch, seq).reshape(batch * seq, W_HEADS)
        h = _mixffn_call(h, n2, y_sb, y_d, mod, w_proj_sb[l], w_proj_dsa[l], w_gate[l], b_gate[l],
                         w_out[l], g_ffn2[l], bf(w1_ffn2), bf(w3_ffn2), bf(w2_ffn2), seq=seq)
    return h.reshape(batch, seq, d)
```
